```python
import math
import jax
import jax.numpy as jnp
from jax import lax
import numpy as np

D_MODEL = 1024
BATCH = 32
SEQ = 256
DEPTH = 1
DEC_BATCH = 8
DEC_SEQ = 2048
PAST_LEN = 512

GRID_W = 64
A_HEADS = 8
A_HEAD_DIM = 64
A_WIDTH = A_HEADS * A_HEAD_DIM
DECAY_LORA = 64
ICLR_LORA = 64
GATE_LORA = 128
SHIFT_WIDTH = 3
DECAY_SCALE = 0.6065306597126334
GN_EPS = 64e-5
B_HEADS = 8
Q_LORA = 256
KV_LORA = 128
QK_NOPE = 64
QK_ROPE = 32
V_HEAD = 64
B_WIDTH = B_HEADS * V_HEAD
ROPE_BASE = 10000.0
ATTN_SCALE = 1.0 / math.sqrt(QK_NOPE + QK_ROPE)
Q_BLOCK = 128
N_EXPERTS = 16
EXPERT_FF = 1024
CAPACITY_FACTOR = 2
EPS = 1e-6
RWKV_COLS = 3 * A_WIDTH + 2 * DECAY_LORA + 2 * ICLR_LORA + GATE_LORA
MLA_COLS = Q_LORA + KV_LORA + QK_ROPE
GATE_COLS = 2 * D_MODEL
IN_COLS = RWKV_COLS + MLA_COLS + GATE_COLS
RWKV_SPLITS = (A_WIDTH, 2 * A_WIDTH, 3 * A_WIDTH, 3 * A_WIDTH + 2 * DECAY_LORA,
               3 * A_WIDTH + 2 * DECAY_LORA + 2 * ICLR_LORA)

kernel_name = 'hybrid_rwkv7_mla_ecmoe_diffusion_step'


def _rmsnorm(x, g):
    x32 = x.astype(jnp.float32)
    y = x32 * lax.rsqrt(jnp.mean(x32 * x32, axis=-1, keepdims=True) + EPS)
    return (y * g.astype(jnp.float32)).astype(x.dtype)


def _modulation(c_vec, w_mod, b_mod):
    m = jax.nn.silu(c_vec) @ w_mod + b_mod
    return [t[:, None, :] for t in jnp.split(m, 6, axis=-1)]


def _token_shift(x, w):
    xp = jnp.pad(x, ((0, 0), (1, 1), (0, 0)))
    return w[0] * xp[:, :-2] + w[1] * xp[:, 1:-1] + w[2] * xp[:, 2:]


def _wkv_scan(r, w, k, v, kk, a, s0, reverse):
    f32 = jnp.float32
    xs = tuple(jnp.moveaxis(t.astype(f32), 1, 0) for t in (r, w, k, v, kk, a))

    def step(s, inp):
        r_t, w_t, k_t, v_t, kk_t, a_t = inp
        s_kk = jnp.einsum('bhvk,bhk->bhv', s, kk_t)
        s = (s * w_t[:, :, None, :] - s_kk[..., None] * (kk_t * a_t)[:, :, None, :]
             + v_t[..., None] * k_t[:, :, None, :])
        return s, jnp.einsum('bhvk,bhk->bhv', s, r_t)

    s_fin, o = lax.scan(step, s0.astype(f32), xs, reverse=reverse)
    return jnp.moveaxis(o, 0, 1), s_fin


def _rwkv_branch(u, s0_f, s0_b, lw):
    B, T, _ = u.shape
    dt = u.dtype
    r, k, v, dw, da, dg = jnp.split(u, RWKV_SPLITS, axis=-1)
    hd = lambda t: t.reshape(B, T, A_HEADS, A_HEAD_DIM)
    r, k, v = hd(r), hd(k), hd(v)
    kk = (k * lw['k_k']).astype(jnp.float32)
    kk = kk * lax.rsqrt(jnp.sum(kk * kk, axis=-1, keepdims=True) + 1e-12)
    dws = jnp.split(dw, 2, axis=-1)
    das = jnp.split(da, 2, axis=-1)
    outs, finals = [], []
    for d, s0 in enumerate((s0_f, s0_b)):
        w = jnp.exp(-DECAY_SCALE * jax.nn.sigmoid(
            (lw['w0'][d] + jnp.tanh(dws[d]) @ lw['w_up'][d]).astype(jnp.float32)))
        a = jax.nn.sigmoid(lw['a0'][d] + das[d] @ lw['a_up'][d])
        w, a = hd(w), hd(a)
        k_d = k * (1.0 + (a - 1.0) * lw['k_a'])
        o, s_fin = _wkv_scan(r, w, k_d, v, kk, a, s0, reverse=(d == 1))
        outs.append(o)
        finals.append(s_fin.astype(dt))
    o = outs[0] + outs[1]
    mu = jnp.mean(o, axis=-1, keepdims=True)
    var = jnp.mean(jnp.square(o - mu), axis=-1, keepdims=True)
    o = ((o - mu) * lax.rsqrt(var + GN_EPS)).reshape(B, T, A_WIDTH)
    o = (o * lw['ln_x_w'] + lw['ln_x_b']).astype(dt)
    bonus = jnp.sum(r * k * lw['r_k'], axis=-1, keepdims=True) * v
    g = jax.nn.sigmoid(dg) @ lw['g_up']
    y = ((o + bonus.reshape(B, T, A_WIDTH)) * g) @ lw['w_branch_a']
    return y, finals[0], finals[1]


def _axial_rope(n):
    rows = n // GRID_W
    t = jnp.arange(rows * GRID_W)
    row = (t // GRID_W).astype(jnp.float32)
    col = (t % GRID_W).astype(jnp.float32)
    per_axis = QK_ROPE // 2
    inv = ROPE_BASE ** (-jnp.arange(0, per_axis, 2, dtype=jnp.float32) / per_axis)
    ang = jnp.concatenate([row[:, None] * inv, col[:, None] * inv], axis=-1)
    return jnp.cos(ang), jnp.sin(ang)


def _rope(x, cos, sin):
    x1, x2 = x[..., 0::2], x[..., 1::2]
    cos, sin = cos.astype(x.dtype), sin.astype(x.dtype)
    return jnp.stack([x1 * cos - x2 * sin, x1 * sin + x2 * cos], axis=-1).reshape(x.shape)


def _mla_latents(m, lw):
    B, T, _ = m.shape
    q_lat, kv_lat, k_pe = jnp.split(m, [Q_LORA, Q_LORA + KV_LORA], axis=-1)
    q = (_rmsnorm(q_lat, lw['g_qnorm']) @ lw['w_uq']).reshape(B, T, B_HEADS, QK_NOPE + QK_ROPE)
    ckv = _rmsnorm(kv_lat, lw['g_kvnorm'])
    return q[..., :QK_NOPE], q[..., QK_NOPE:], ckv, k_pe


def _mla_keys(ckv, k_pe, lw):
    B, S, _ = ckv.shape
    kv = (ckv @ lw['w_ukv']).reshape(B, S, B_HEADS, QK_NOPE + V_HEAD)
    k_rep = jnp.broadcast_to(k_pe[:, :, None, :], (B, S, B_HEADS, QK_ROPE))
    return jnp.concatenate([kv[..., :QK_NOPE], k_rep], axis=-1), kv[..., QK_NOPE:]


def _attend(q_list, k_list, v_list):
    B, T, H, _ = q_list[0].shape
    nb = T // Q_BLOCK
    qb = tuple(jnp.moveaxis(q.reshape(B, nb, Q_BLOCK, H, q.shape[-1]), 1, 0) for q in q_list)
    v = jnp.concatenate(v_list, axis=1)

    def one_block(qs):
        s = jnp.concatenate([jnp.einsum('bqhe,bshe->bhqs', qi, ki) for qi, ki in zip(qs, k_list)], axis=-1)
        p = jax.nn.softmax(s.astype(jnp.float32) * ATTN_SCALE, axis=-1).astype(v.dtype)
        return jnp.einsum('bhqs,bshd->bqhd', p, v)

    o = lax.map(one_block, qb)
    return jnp.moveaxis(o, 0, 1).reshape(B, T, H, V_HEAD)


def _merge(gates, y_a, y_b, lw):
    g_a, g_b = jnp.split(gates, 2, axis=-1)
    return (jax.nn.sigmoid(g_a) * y_a + jax.nn.sigmoid(g_b) * y_b) @ lw['w_out']


def _mixer_context(h, lw):
    B, T, _ = h.shape
    u, m, gates = jnp.split(h @ lw['w_in'], [RWKV_COLS, RWKV_COLS + MLA_COLS], axis=-1)
    s0 = jnp.zeros((B, A_HEADS, A_HEAD_DIM, A_HEAD_DIM), h.dtype)
    y_a, s_f, s_b = _rwkv_branch(_token_shift(u, lw['shift_w']), s0, s0, lw)
    q_nope, q_pe, ckv, k_pe = _mla_latents(m, lw)
    k, v = _mla_keys(ckv, k_pe, lw)
    att = _attend([jnp.concatenate([q_nope, q_pe], axis=-1)], [k], [v])
    y_b = att.reshape(B, T, B_WIDTH) @ lw['w_branch_b']
    return _merge(gates, y_a, y_b, lw), (ckv, k_pe, s_f, s_b)


def _mixer_latent(h, ckv_ctx, kpe_ctx, s_f, s_b, lw):
    B, T, _ = h.shape
    u, m, gates = jnp.split(h @ lw['w_in'], [RWKV_COLS, RWKV_COLS + MLA_COLS], axis=-1)
    y_a, _, _ = _rwkv_branch(_token_shift(u, lw['shift_w']), s_f, s_b, lw)
    q_nope, q_pe, ckv, k_pe = _mla_latents(m, lw)
    cos, sin = _axial_rope(T)
    q_rot = jnp.concatenate([q_nope, _rope(q_pe, cos[:, None, :], sin[:, None, :])], axis=-1)
    q_raw = jnp.concatenate([q_nope, q_pe], axis=-1)
    k_lat, v_lat = _mla_keys(ckv, _rope(k_pe, cos, sin), lw)
    k_ctx, v_ctx = _mla_keys(ckv_ctx, kpe_ctx, lw)
    att = _attend([q_rot, q_raw], [k_lat, k_ctx], [v_lat, v_ctx])
    y_b = att.reshape(B, T, B_WIDTH) @ lw['w_branch_b']
    return _merge(gates, y_a, y_b, lw), None


def _expert_choice_ffn(h, lw):
    B, T, D = h.shape
    xt = h.reshape(B * T, D)
    n = B * T
    cap = CAPACITY_FACTOR * n // N_EXPERTS
    aff = jax.nn.softmax((xt @ lw['w_router']).astype(jnp.float32), axis=-1)
    top_aff, top_idx = lax.top_k(aff.T, cap)
    xe = xt[top_idx]
    hid = jax.nn.silu(jnp.einsum('ecd,edf->ecf', xe, lw['w_exp_gate'])) * jnp.einsum('ecd,edf->ecf', xe, lw['w_exp_up'])
    ye = jnp.einsum('ecf,efd->ecd', hid, lw['w_exp_down']) * top_aff[..., None].astype(h.dtype)
    out = jnp.zeros((n, D), h.dtype).at[top_idx.reshape(-1)].add(ye.reshape(-1, D))
    return out.reshape(B, T, D)


def _sandwich_layer(x, c_vec, mixer, lw):
    sh1, sc1, g1, sh2, sc2, g2 = _modulation(c_vec, lw['w_mod'], lw['b_mod'])
    h = _rmsnorm(x, lw['g_pre_mix']) * (1.0 + sc1) + sh1
    y, st = mixer(h)
    x = x + g1 * _rmsnorm(y, lw['g_post_mix'])
    h = _rmsnorm(x, lw['g_pre_ffn']) * (1.0 + sc2) + sh2
    y = _expert_choice_ffn(h, lw)
    x = x + g2 * _rmsnorm(y, lw['g_post_ffn'])
    return x, st


def setup_inputs(seed: int = 0) -> dict:
    key = jax.random.key(seed)
    ks = iter(jax.random.split(key, 48))
    nrm = lambda shape, scale: scale * jax.random.normal(next(ks), shape, jnp.float32)
    L, D = DEPTH, D_MODEL
    shift_base = jnp.array([0.0, 1.0, 0.0], jnp.float32)[None, :, None]
    return {
        'x_prompt': nrm((BATCH, SEQ, D), 1.0),
        'x_sample': nrm((DEC_BATCH, DEC_SEQ, D), 1.0),
        'cache_ckv': nrm((DEC_BATCH, L, PAST_LEN, KV_LORA), 1.0),
        'cache_kpe': nrm((DEC_BATCH, L, PAST_LEN, QK_ROPE), 1.0),
        'state_wkv_fwd': nrm((DEC_BATCH, L, A_HEADS, A_HEAD_DIM, A_HEAD_DIM), 0.5),
        'state_wkv_bwd': nrm((DEC_BATCH, L, A_HEADS, A_HEAD_DIM, A_HEAD_DIM), 0.5),
        'c': nrm((DEC_BATCH, D), 1.0),
        'c_ctx': nrm((D,), 1.0),
        'w_mod': nrm((L, D, 6 * D), 0.5 * D ** -0.5),
        'b_mod': nrm((L, 6 * D), 0.05),
        'g_pre_mix': 1.0 + nrm((L, D), 0.05),
        'g_post_mix': 1.0 + nrm((L, D), 0.05),
        'g_pre_ffn': 1.0 + nrm((L, D), 0.05),
        'g_post_ffn': 1.0 + nrm((L, D), 0.05),
        'w_in': nrm((L, D, IN_COLS), D ** -0.5),
        'shift_w': shift_base + nrm((L, SHIFT_WIDTH, RWKV_COLS), 0.2),
        'w0': nrm((L, 2, A_WIDTH), 0.5),
        'w_up': nrm((L, 2, DECAY_LORA, A_WIDTH), DECAY_LORA ** -0.5),
        'a0': nrm((L, 2, A_WIDTH), 0.1),
        'a_up': nrm((L, 2, ICLR_LORA, A_WIDTH), ICLR_LORA ** -0.5),
        'g_up': nrm((L, GATE_LORA, A_WIDTH), GATE_LORA ** -0.5),
        'k_k': 0.85 + nrm((L, A_HEADS, A_HEAD_DIM), 0.05),
        'k_a': 1.0 + nrm((L, A_HEADS, A_HEAD_DIM), 0.05),
        'r_k': nrm((L, A_HEADS, A_HEAD_DIM), 0.1),
        'ln_x_w': 1.0 + nrm((L, A_WIDTH), 0.05),
        'ln_x_b': nrm((L, A_WIDTH), 0.02),
        'w_branch_a': nrm((L, A_WIDTH, D), A_WIDTH ** -0.5),
        'g_qnorm': 1.0 + nrm((L, Q_LORA), 0.05),
        'w_uq': nrm((L, Q_LORA, B_HEADS * (QK_NOPE + QK_ROPE)), Q_LORA ** -0.5),
        'g_kvnorm': 1.0 + nrm((L, KV_LORA), 0.05),
        'w_ukv': nrm((L, KV_LORA, B_HEADS * (QK_NOPE + V_HEAD)), KV_LORA ** -0.5),
        'w_branch_b': nrm((L, B_WIDTH, D), B_WIDTH ** -0.5),
        'w_out': nrm((L, D, D), D ** -0.5),
        'w_router': nrm((L, D, N_EXPERTS), D ** -0.5),
        'w_exp_gate': nrm((L, N_EXPERTS, D, EXPERT_FF), D ** -0.5),
        'w_exp_up': nrm((L, N_EXPERTS, D, EXPERT_FF), D ** -0.5),
        'w_exp_down': nrm((L, N_EXPERTS, EXPERT_FF, D), EXPERT_FF ** -0.5),
    }


def reference(x_prompt, x_sample, cache_ckv, cache_kpe, state_wkv_fwd, state_wkv_bwd, c, c_ctx,
              w_mod, b_mod, g_pre_mix, g_post_mix, g_pre_ffn, g_post_ffn, w_in, shift_w,
              w0, w_up, a0, a_up, g_up, k_k, k_a, r_k, ln_x_w, ln_x_b, w_branch_a,
              g_qnorm, w_uq, g_kvnorm, w_ukv, w_branch_b, w_out,
              w_router, w_exp_gate, w_exp_up, w_exp_down):
    xp, xs = x_prompt, x_sample
    c_ctx_b = jnp.broadcast_to(c_ctx[None, :], (x_prompt.shape[0], D_MODEL))
    ckv_l, kpe_l, sf_l, sb_l = [], [], [], []
    for l in range(DEPTH):
        lw = {
            'w_mod': w_mod[l], 'b_mod': b_mod[l],
            'g_pre_mix': g_pre_mix[l], 'g_post_mix': g_post_mix[l],
            'g_pre_ffn': g_pre_ffn[l], 'g_post_ffn': g_post_ffn[l],
            'w_in': w_in[l], 'shift_w': shift_w[l],
            'w0': w0[l], 'w_up': w_up[l], 'a0': a0[l], 'a_up': a_up[l], 'g_up': g_up[l],
            'k_k': k_k[l], 'k_a': k_a[l], 'r_k': r_k[l], 'ln_x_w': ln_x_w[l], 'ln_x_b': ln_x_b[l],
            'w_branch_a': w_branch_a[l],
            'g_qnorm': g_qnorm[l], 'w_uq': w_uq[l], 'g_kvnorm': g_kvnorm[l], 'w_ukv': w_ukv[l],
            'w_branch_b': w_branch_b[l], 'w_out': w_out[l],
            'w_router': w_router[l], 'w_exp_gate': w_exp_gate[l], 'w_exp_up': w_exp_up[l],
            'w_exp_down': w_exp_down[l],
        }
        xp, (ckv, kpe, s_f, s_b) = _sandwich_layer(xp, c_ctx_b, lambda h: _mixer_context(h, lw), lw)
        ckv_l.append(ckv)
        kpe_l.append(kpe)
        sf_l.append(s_f)
        sb_l.append(s_b)
        ckv_c, kpe_c = cache_ckv[:, l], cache_kpe[:, l]
        sf_c, sb_c = state_wkv_fwd[:, l], state_wkv_bwd[:, l]
        xs, _ = _sandwich_layer(xs, c, lambda h: _mixer_latent(h, ckv_c, kpe_c, sf_c, sb_c, lw), lw)
    new_ckv = jnp.stack(ckv_l, axis=1)
    new_kpe = jnp.stack(kpe_l, axis=1)
    new_state_fwd = jnp.stack(sf_l, axis=1)
    new_state_bwd = jnp.stack(sb_l, axis=1)
    return (xp, xs, new_ckv, new_kpe, new_state_fwd, new_state_bwd)
```

```python
import functools
import math

import jax
import jax.numpy as jnp
from jax import lax
from jax.experimental import pallas as pl
from jax.experimental.pallas import tpu as pltpu

F32 = jnp.float32
BF16 = jnp.bfloat16
I32 = jnp.int32

GRID_W = 64
A_HEADS = 8
A_HEAD_DIM = 64
A_WIDTH = A_HEADS * A_HEAD_DIM
DECAY_LORA = 64
ICLR_LORA = 64
GATE_LORA = 128
DECAY_SCALE = 0.6065306597126334
GN_EPS = 64e-5
B_HEADS = 8
Q_LORA = 256
KV_LORA = 128
QK_NOPE = 64
QK_ROPE = 32
V_HEAD = 64
ROPE_BASE = 10000.0
ATTN_SCALE = 1.0 / math.sqrt(QK_NOPE + QK_ROPE)
N_EXPERTS = 16
CAPACITY_FACTOR = 2
EPS = 1e-6
RWKV_COLS = 3 * A_WIDTH + 2 * DECAY_LORA + 2 * ICLR_LORA + GATE_LORA
MLA_COLS = Q_LORA + KV_LORA + QK_ROPE

LANES = 128
HEAD_PAD = 128
M_COLS = 512
VMEM_LIMIT = 56 * 1024 * 1024

SCAN_CHUNK = 64
TOKEN_BLOCK = 256
EXPERT_TILE = 256


def _cparams(*sem):
    return pltpu.CompilerParams(dimension_semantics=sem, vmem_limit_bytes=VMEM_LIMIT)


def _dot(a, b):
    return jnp.dot(a, b, preferred_element_type=F32)


def _dot_nt(a, b):
    return lax.dot_general(a, b, (((1,), (1,)), ((), ())), preferred_element_type=F32)


def _dot_tn(a, b):
    return lax.dot_general(a, b, (((0,), (0,)), ((), ())), preferred_element_type=F32)


def _split2(x):
    hi = x.astype(BF16)
    lo = (x - hi.astype(F32)).astype(BF16)
    return hi, lo


def _dot_x2(a, b_bf16):
    hi, lo = _split2(a)
    return _dot(hi, b_bf16) + _dot(lo, b_bf16)


def _dot_f32(a, b):
    ah, al = _split2(a)
    bh, bl = _split2(b)
    return _dot(ah, bh) + (_dot(ah, bl) + _dot(al, bh))


def _rms(x, g):
    return (x * lax.rsqrt(jnp.mean(x * x, axis=-1, keepdims=True) + EPS)) * g


def _mod_kernel(c_ref, w_ref, b_ref, o_ref):
    c = c_ref[...]
    s = c * jax.nn.sigmoid(c)
    o_ref[...] = _dot_f32(s, w_ref[...]) + b_ref[...]


def _modulation(c_rows, w_mod, b_mod):
    rows, d = c_rows.shape
    n_out = w_mod.shape[1]
    tn = n_out // 8
    return pl.pallas_call(
        _mod_kernel,
        grid=(n_out // tn,),
        in_specs=[pl.BlockSpec((rows, d), lambda j: (0, 0)),
                  pl.BlockSpec((d, tn), lambda j: (0, j)),
                  pl.BlockSpec((1, tn), lambda j: (0, j))],
        out_specs=pl.BlockSpec((rows, tn), lambda j: (0, j)),
        out_shape=jax.ShapeDtypeStruct((rows, n_out), F32),
        compiler_params=_cparams("arbitrary"),
        name="modulation",
    )(c_rows, w_mod, b_mod.reshape(1, n_out))


def _inproj_kernel(x_ref, sc_ref, sh_ref, g_ref, w_ref, u_ref, m_ref, gt_ref):
    h = _rms(x_ref[...], g_ref[...]) * (1.0 + sc_ref[0]) + sh_ref[0]
    hb = h.astype(BF16)
    u_ref[...] = _dot(hb, w_ref[:, :RWKV_COLS])
    m_ref[...] = _dot(hb, w_ref[:, RWKV_COLS:RWKV_COLS + M_COLS])
    gt_ref[...] = _dot(hb, w_ref[:, RWKV_COLS + M_COLS:])


def _mod_spec(mod, blocks_per_batch):
    d = mod.shape[-1]
    if mod.shape[0] == 1:
        return pl.BlockSpec((1, 1, d), lambda i: (0, 0, 0))
    return pl.BlockSpec((1, 1, d), lambda i: (i // blocks_per_batch, 0, 0))


def _inproj(x2, sc, sh, g, w_in_p, tb, blocks_per_batch):
    n, d = x2.shape
    cols = w_in_p.shape[1]
    gate_cols = cols - RWKV_COLS - M_COLS
    return pl.pallas_call(
        _inproj_kernel,
        grid=(n // tb,),
        in_specs=[pl.BlockSpec((tb, d), lambda i: (i, 0)),
                  _mod_spec(sc, blocks_per_batch), _mod_spec(sh, blocks_per_batch),
                  pl.BlockSpec((1, d), lambda i: (0, 0)),
                  pl.BlockSpec((d, cols), lambda i: (0, 0))],
        out_specs=[pl.BlockSpec((tb, RWKV_COLS), lambda i: (i, 0)),
                   pl.BlockSpec((tb, M_COLS), lambda i: (i, 0)),
                   pl.BlockSpec((tb, gate_cols), lambda i: (i, 0))],
        out_shape=[jax.ShapeDtypeStruct((n, RWKV_COLS), F32),
                   jax.ShapeDtypeStruct((n, M_COLS), F32),
                   jax.ShapeDtypeStruct((n, gate_cols), F32)],
        compiler_params=_cparams("arbitrary"),
        name="inproj",
    )(x2, sc, sh, g, w_in_p)


def _rwkv_prep_kernel(u_ref, hp_ref, hn_ref, sw_ref, w0_ref, a0_ref, wup_ref, aup_ref, gup_ref,
                      kk_ref, ka_ref, rk_ref, bd_ref,
                      r_o, v_o, nkk_o, kdf_o, kdb_o, bf_o, bb_o, lwf_o, lwb_o, g_o, bonus_o):
    u = u_ref[0]
    tb = u.shape[0]
    row = lax.broadcasted_iota(I32, u.shape, 0)
    prev = jnp.where(row == 0, hp_ref[0, 0], pltpu.roll(u, 1, 0))
    nxt = jnp.where(row == tb - 1, hn_ref[0, 0], pltpu.roll(u, tb - 1, 0))
    xs = sw_ref[0:1] * prev + sw_ref[1:2] * u + sw_ref[2:3] * nxt
    aw = A_WIDTH
    r = xs[:, 0:aw]
    k = xs[:, aw:2 * aw]
    v = xs[:, 2 * aw:3 * aw]
    o = 3 * aw
    dw = xs[:, o:o + 2 * DECAY_LORA]
    da = xs[:, o + 2 * DECAY_LORA:o + 2 * DECAY_LORA + 2 * ICLR_LORA]
    dg = xs[:, o + 2 * DECAY_LORA + 2 * ICLR_LORA:]
    bd = bd_ref[...]
    kkr = k * kk_ref[...]
    kk = kkr * lax.rsqrt(_dot_x2(kkr * kkr, bd) + 1e-12)
    tw = jnp.tanh(dw).astype(BF16)
    dab = da.astype(BF16)
    r_o[0] = r
    v_o[0] = v
    nkk_o[0] = -kk
    for d, (lw_o, kd_o, b_o) in enumerate(((lwf_o, kdf_o, bf_o), (lwb_o, kdb_o, bb_o))):
        lw_o[0] = -DECAY_SCALE * jax.nn.sigmoid(w0_ref[d:d + 1] + _dot(tw, wup_ref[d]))
        a = jax.nn.sigmoid(a0_ref[d:d + 1] + _dot(dab, aup_ref[d]))
        kd_o[0] = k * (1.0 + (a - 1.0) * ka_ref[...])
        b_o[0] = kk * a
    g_o[0] = _dot(jax.nn.sigmoid(dg).astype(BF16), gup_ref[...])
    bonus_o[0] = _dot_x2(r * k * rk_ref[...], bd) * v


def _rwkv_prep(u3, shift_w, w0, a0, wup_p, aup_p, gup, k_k, k_a, r_k, bd, tb):
    b, t, cols = u3.shape
    nb = t // tb
    zero = jnp.zeros((b, 1, cols), F32)
    halo_prev = jnp.concatenate([zero, u3[:, tb - 1:t - 1:tb]], axis=1).reshape(b, nb, 1, cols)
    halo_next = jnp.concatenate([u3[:, tb::tb], zero], axis=1).reshape(b, nb, 1, cols)
    aw = A_WIDTH
    full2 = lambda s: pl.BlockSpec(s, lambda i, j: (0, 0))
    full3 = lambda s: pl.BlockSpec(s, lambda i, j: (0, 0, 0))
    out_spec = pl.BlockSpec((1, tb, aw), lambda i, j: (i, j, 0))
    out_sds = jax.ShapeDtypeStruct((b, t, aw), F32)
    return pl.pallas_call(
        _rwkv_prep_kernel,
        grid=(b, nb),
        in_specs=[pl.BlockSpec((1, tb, cols), lambda i, j: (i, j, 0)),
                  pl.BlockSpec((1, 1, 1, cols), lambda i, j: (i, j, 0, 0)),
                  pl.BlockSpec((1, 1, 1, cols), lambda i, j: (i, j, 0, 0)),
                  full2((3, cols)), full2((2, aw)), full2((2, aw)),
                  full3(wup_p.shape), full3(aup_p.shape), full2(gup.shape),
                  full2((1, aw)), full2((1, aw)), full2((1, aw)), full2((aw, aw))],
        out_specs=[out_spec] * 11,
        out_shape=[out_sds] * 11,
        compiler_params=_cparams("arbitrary", "arbitrary"),
        name="rwkv_prep",
    )(u3, halo_prev, halo_next, shift_w, w0, a0, wup_p, aup_p, gup, k_k, k_a, r_k, bd)


def _scan_kernel(r_ref, v_ref, nkk_ref, kd_ref, b_ref, lw_ref, s0_ref, o_ref, sf_ref, s_scr, *, reverse):
    c = pl.program_id(1)

    @pl.when(c == 0)
    def _():
        s_scr[...] = s0_ref[0]

    lw = lw_ref[0]
    ch = lw.shape[0]
    ri = lax.broadcasted_iota(I32, (ch, ch), 0)
    ci = lax.broadcasted_iota(I32, (ch, ch), 1)
    if reverse:
        incl, strict = ci >= ri, ci > ri
    else:
        incl, strict = ci <= ri, ci < ri
    tri = jnp.where(incl, 1.0, 0.0).astype(BF16)
    hi = lw.astype(BF16)
    rem = lw - hi.astype(F32)
    mid = rem.astype(BF16)
    lo = (rem - mid.astype(F32)).astype(BF16)
    li = _dot(tri, hi) + (_dot(tri, mid) + _dot(tri, lo))
    lt = li[0:1] if reverse else li[ch - 1:ch]
    rho = 0.5 * lt
    e1 = jnp.exp(li - rho)
    e2 = jnp.exp(rho - li)
    er = jnp.exp(rho)
    egt = jnp.exp(lt)
    a_rel = nkk_ref[0] * (e1 * jnp.exp(-lw))
    r_rel = r_ref[0] * e1
    b_rel = b_ref[0] * e2
    k_rel = kd_ref[0] * e2
    a_abs = (a_rel * er).astype(BF16)
    r_abs = (r_rel * er).astype(BF16)
    b_end = (b_rel * er).astype(BF16)
    k_end = (k_rel * er).astype(BF16)
    a_rel = a_rel.astype(BF16)
    r_rel = r_rel.astype(BF16)
    b_rel = b_rel.astype(BF16)
    k_rel = k_rel.astype(BF16)
    v = v_ref[0].astype(BF16)
    eye = jnp.where(ri == ci, 1.0, 0.0)
    n_double = int(math.log2(ch)) - 1
    outs = []
    for h in range(A_HEADS):
        sl = slice(h * A_HEAD_DIM, (h + 1) * A_HEAD_DIM)
        a_ab = jnp.where(strict, _dot_nt(a_rel[:, sl], b_rel[:, sl]), 0.0)
        a_ak = jnp.where(strict, _dot_nt(a_rel[:, sl], k_rel[:, sl]), 0.0)
        a_rb = jnp.where(incl, _dot_nt(r_rel[:, sl], b_rel[:, sl]), 0.0)
        a_rk = jnp.where(incl, _dot_nt(r_rel[:, sl], k_rel[:, sl]), 0.0)
        inv = eye + a_ab
        lp = a_ab
        for _ in range(n_double):
            lpb = lp.astype(BF16)
            lp = _dot(lpb, lpb)
            inv = inv + _dot(inv.astype(BF16), lp.astype(BF16))
        s = s_scr[h]
        sb = s.astype(BF16)
        vh = v[:, sl]
        x = _dot_nt(a_abs[:, sl], sb) + _dot(a_ak.astype(BF16), vh)
        u = _dot(inv.astype(BF16), x.astype(BF16)).astype(BF16)
        outs.append(_dot_nt(r_abs[:, sl], sb) + _dot(a_rb.astype(BF16), u) + _dot(a_rk.astype(BF16), vh))
        s_scr[h] = s * egt[:, sl] + _dot_tn(u, b_end[:, sl]) + _dot_tn(vh, k_end[:, sl])
    o_ref[0] = jnp.concatenate(outs, axis=1)

    @pl.when(c == pl.num_programs(1) - 1)
    def _():
        sf_ref[0] = s_scr[...]


def _wkv_scan(r, v, nkk, kd, b, lw, s0, reverse):
    bsz, t, aw = r.shape
    ch = min(SCAN_CHUNK, t)
    nc = t // ch
    tmap = (lambda i, c: (i, nc - 1 - c, 0)) if reverse else (lambda i, c: (i, c, 0))
    seq = pl.BlockSpec((1, ch, aw), tmap)
    st = pl.BlockSpec((1, A_HEADS, A_HEAD_DIM, A_HEAD_DIM), lambda i, c: (i, 0, 0, 0))
    return pl.pallas_call(
        functools.partial(_scan_kernel, reverse=reverse),
        grid=(bsz, nc),
        in_specs=[seq] * 6 + [st],
        out_specs=[seq, st],
        out_shape=[jax.ShapeDtypeStruct((bsz, t, aw), F32),
                   jax.ShapeDtypeStruct(s0.shape, F32)],
        scratch_shapes=[pltpu.VMEM((A_HEADS, A_HEAD_DIM, A_HEAD_DIM), F32)],
        compiler_params=_cparams("arbitrary", "arbitrary"),
        name="wkv_scan_bwd" if reverse else "wkv_scan_fwd",
    )(r, v, nkk, kd, b, lw, s0)


def _rope_lanes(x, cos_t, sin_up, sin_dn):
    w = x.shape[1]
    half = QK_ROPE // 2
    return x * cos_t + pltpu.roll(x, half, 1) * sin_up + pltpu.roll(x, w - half, 1) * sin_dn


def _mla_prep_kernel(*refs, rope):
    if rope:
        (m_ref, gq_ref, gkv_ref, wuq_ref, wk_ref, wv_ref, p_ref,
         qc_ref, qu_ref, qd_ref, kc_ref, ku_ref, kd_ref, q_o, k_o, v_o, ckv_o) = refs
    else:
        (m_ref, gq_ref, gkv_ref, wuq_ref, wk_ref, wv_ref, p_ref, q_o, k_o, v_o, ckv_o) = refs
    m = m_ref[...]
    qn = _rms(m[:, :Q_LORA], gq_ref[...])
    ckv = _rms(m[:, Q_LORA:Q_LORA + KV_LORA], gkv_ref[...])
    kp = m[:, Q_LORA + KV_LORA:]
    q = _dot(qn.astype(BF16), wuq_ref[...]) * ATTN_SCALE
    if rope:
        tile = lambda ref: jnp.concatenate([ref[...]] * B_HEADS, axis=1)
        q = _rope_lanes(q, tile(qc_ref), tile(qu_ref), tile(qd_ref))
        kp = _rope_lanes(kp, kc_ref[...], ku_ref[...], kd_ref[...])
    cb = ckv.astype(BF16)
    q_o[...] = q.astype(BF16)
    k_o[...] = (_dot(cb, wk_ref[...]) + _dot(kp.astype(BF16), p_ref[...])).astype(BF16)
    v_o[...] = _dot(cb, wv_ref[...]).astype(BF16)
    ckv_o[...] = ckv


def _mla_prep(m2, g_q, g_kv, wuq_p, wk_p, wv_p, place, rope_tabs, tb, t):
    n = m2.shape[0]
    hw = B_HEADS * HEAD_PAD
    full = lambda a: pl.BlockSpec(a.shape, lambda i: (0, 0))
    ins = [m2, g_q, g_kv, wuq_p, wk_p, wv_p, place]
    specs = [pl.BlockSpec((tb, M_COLS), lambda i: (i, 0))] + [full(a) for a in ins[1:]]
    if rope_tabs is not None:
        nbt = t // tb
        ins += list(rope_tabs)
        specs += [pl.BlockSpec((tb, HEAD_PAD), lambda i: (i % nbt, 0))] * 6
    big = pl.BlockSpec((tb, hw), lambda i: (i, 0))
    return pl.pallas_call(
        functools.partial(_mla_prep_kernel, rope=rope_tabs is not None),
        grid=(n // tb,),
        in_specs=specs,
        out_specs=[big, big, big, pl.BlockSpec((tb, KV_LORA), lambda i: (i, 0))],
        out_shape=[jax.ShapeDtypeStruct((n, hw), BF16)] * 3 + [jax.ShapeDtypeStruct((n, KV_LORA), F32)],
        compiler_params=_cparams("arbitrary"),
        name="mla_prep",
    )(*ins)


def _kv_up_kernel(ckv_ref, kp_ref, wk_ref, wv_ref, p_ref, k_o, v_o):
    cb = ckv_ref[...].astype(BF16)
    k_o[...] = (_dot(cb, wk_ref[...]) + _dot(kp_ref[...].astype(BF16), p_ref[...])).astype(BF16)
    v_o[...] = _dot(cb, wv_ref[...]).astype(BF16)


def _kv_up(ckv2, kpe_pad, wk_p, wv_p, place, tb):
    n = ckv2.shape[0]
    hw = B_HEADS * HEAD_PAD
    full = lambda a: pl.BlockSpec(a.shape, lambda i: (0, 0))
    big = pl.BlockSpec((tb, hw), lambda i: (i, 0))
    return pl.pallas_call(
        _kv_up_kernel,
        grid=(n // tb,),
        in_specs=[pl.BlockSpec((tb, KV_LORA), lambda i: (i, 0)),
                  pl.BlockSpec((tb, HEAD_PAD), lambda i: (i, 0)),
                  full(wk_p), full(wv_p), full(place)],
        out_specs=[big, big],
        out_shape=[jax.ShapeDtypeStruct((n, hw), BF16)] * 2,
        compiler_params=_cparams("arbitrary"),
        name="kv_up",
    )(ckv2, kpe_pad, wk_p, wv_p, place)


def _attn_kernel(*refs, two):
    if two:
        q_ref, k1_ref, v1_ref, k2_ref, v2_ref, o_ref = refs
    else:
        q_ref, k1_ref, v1_ref, o_ref = refs
    for h in range(B_HEADS):
        hs = slice(h * HEAD_PAD, (h + 1) * HEAD_PAD)
        q = q_ref[0, :, hs]
        s1 = _dot_nt(q, k1_ref[0, :, hs])
        mx = jnp.max(s1, axis=-1, keepdims=True)
        if two:
            s2 = _dot_nt(q, k2_ref[0, :, hs])
            mx = jnp.maximum(mx, jnp.max(s2, axis=-1, keepdims=True))
        p1 = jnp.exp(s1 - mx)
        den = jnp.sum(p1, axis=-1, keepdims=True)
        acc = _dot(p1.astype(BF16), v1_ref[0, :, hs])
        if two:
            p2 = jnp.exp(s2 - mx)
            den = den + jnp.sum(p2, axis=-1, keepdims=True)
            acc = acc + _dot(p2.astype(BF16), v2_ref[0, :, hs])
        o_ref[0, :, hs] = (acc / den).astype(BF16)


def _attention(q3, k1, v1, k2, v2, tq):
    b, t, hw = q3.shape
    two = k2 is not None
    qspec = pl.BlockSpec((1, tq, hw), lambda i, j: (i, j, 0))
    kv = lambda a: pl.BlockSpec((1, a.shape[1], hw), lambda i, j: (i, 0, 0))
    ins = [q3, k1, v1] + ([k2, v2] if two else [])
    return pl.pallas_call(
        functools.partial(_attn_kernel, two=two),
        grid=(b, t // tq),
        in_specs=[qspec] + [kv(a) for a in ins[1:]],
        out_specs=qspec,
        out_shape=jax.ShapeDtypeStruct((b, t, hw), BF16),
        compiler_params=_cparams("arbitrary", "arbitrary"),
        name="attention",
    )(*ins)


def _post_mix_kernel(of_ref, ob_ref, bonus_ref, g_ref, att_ref, gates_ref, x_ref,
                     g1_ref, sc2_ref, sh2_ref, lnw_ref, lnb_ref, bd_ref, wa_ref, wb_ref, wo_ref,
                     gpost_ref, gpre_ref, wr_ref, x1_o, h2_o, aff_o):
    bd = bd_ref[...]
    inv_n = 1.0 / A_HEAD_DIM
    o = of_ref[...] + ob_ref[...]
    mu = _dot_x2(o, bd) * inv_n
    oc = o - mu
    var = _dot_x2(oc * oc, bd) * inv_n
    on = (oc * lax.rsqrt(var + GN_EPS)) * lnw_ref[...] + lnb_ref[...]
    ya = _dot(((on + bonus_ref[...]) * g_ref[...]).astype(BF16), wa_ref[...])
    yb = _dot(att_ref[...], wb_ref[...])
    d = ya.shape[1]
    gates = gates_ref[...]
    mix = jax.nn.sigmoid(gates[:, :d]) * ya + jax.nn.sigmoid(gates[:, d:]) * yb
    y = _dot(mix.astype(BF16), wo_ref[...])
    x1 = x_ref[...] + g1_ref[0] * _rms(y, gpost_ref[...])
    x1_o[...] = x1
    h2 = _rms(x1, gpre_ref[...]) * (1.0 + sc2_ref[0]) + sh2_ref[0]
    h2_o[...] = h2.astype(BF16)
    hh, hl = _split2(h2)
    wh, wl = _split2(wr_ref[...])
    logits = _dot_nt(wh, hh) + (_dot_nt(wh, hl) + _dot_nt(wl, hh))
    z = jnp.exp(logits - jnp.max(logits, axis=0, keepdims=True))
    aff_o[...] = z / jnp.sum(z, axis=0, keepdims=True)


def _post_mix(o_f, o_b, bonus, g, att, gates, x2, g1, sc2, sh2, ln_w, ln_b, bd, wa, wb, wo,
              g_post, g_pre, w_router_t, tb, blocks_per_batch):
    n, d = x2.shape
    tok = lambda a: pl.BlockSpec((tb, a.shape[1]), lambda i: (i, 0))
    full = lambda a: pl.BlockSpec(a.shape, lambda i: (0, 0))
    ms = lambda a: _mod_spec(a, blocks_per_batch)
    return pl.pallas_call(
        _post_mix_kernel,
        grid=(n // tb,),
        in_specs=[tok(o_f), tok(o_b), tok(bonus), tok(g), tok(att), tok(gates), tok(x2),
                  ms(g1), ms(sc2), ms(sh2), full(ln_w), full(ln_b), full(bd), full(wa), full(wb),
                  full(wo), full(g_post), full(g_pre), full(w_router_t)],
        out_specs=[pl.BlockSpec((tb, d), lambda i: (i, 0)),
                   pl.BlockSpec((tb, d), lambda i: (i, 0)),
                   pl.BlockSpec((N_EXPERTS, tb), lambda i: (0, i))],
        out_shape=[jax.ShapeDtypeStruct((n, d), F32),
                   jax.ShapeDtypeStruct((n, d), BF16),
                   jax.ShapeDtypeStruct((N_EXPERTS, n), F32)],
        compiler_params=_cparams("arbitrary"),
        name="post_mix",
    )(o_f, o_b, bonus, g, att, gates, x2, g1, sc2, sh2, ln_w, ln_b, bd, wa, wb, wo,
      g_post, g_pre, w_router_t)


def _select_kernel(a_ref, pos_ref, sel_ref, *, cap, n_chunks):
    a = a_ref[...]
    rows = a.shape[0]
    a3 = a.reshape(N_EXPERTS, n_chunks, LANES)

    def count(mask):
        c = jnp.sum(jnp.where(mask, 1.0, 0.0), axis=2, keepdims=True)
        return jnp.sum(c, axis=1, keepdims=True)

    def body(i, thr):
        cand = thr | jnp.left_shift(jnp.int32(1), 30 - i)
        return jnp.where(count(a3 >= pltpu.bitcast(cand, F32)) >= cap, cand, thr)

    thr = pltpu.bitcast(lax.fori_loop(0, 31, body, jnp.zeros((N_EXPERTS, 1, LANES), I32)), F32)
    gt = a3 > thr
    eq = a3 == thr
    need = cap - count(gt)

    li = lax.broadcasted_iota(I32, (LANES, LANES), 0)
    lj = lax.broadcasted_iota(I32, (LANES, LANES), 1)
    lane_before = jnp.where(li < lj, 1.0, 0.0).astype(BF16)
    lane_all = jnp.ones((LANES, LANES), BF16)
    ci = lax.broadcasted_iota(I32, (n_chunks, n_chunks), 0)
    cj = lax.broadcasted_iota(I32, (n_chunks, n_chunks), 1)
    chunk_before = jnp.where(cj < ci, 1.0, 0.0).astype(BF16)

    def prefix(flags3):
        f2 = flags3.reshape(rows, LANES).astype(BF16)
        within = _dot(f2, lane_before).reshape(N_EXPERTS, n_chunks, LANES)
        tot = _dot(f2, lane_all).astype(BF16).reshape(N_EXPERTS, n_chunks, LANES)
        offs = [_dot(chunk_before, tot[e]) for e in range(N_EXPERTS)]
        return within + jnp.stack(offs, axis=0)

    eqf = jnp.where(eq, 1.0, 0.0)
    tie_ok = jnp.where(prefix(eqf) < need, eqf, 0.0)
    sel = jnp.where(gt, 1.0, tie_ok)
    pos_ref[...] = prefix(sel).reshape(rows, LANES).astype(I32)
    sel_ref[...] = sel.reshape(rows, LANES).astype(I32)


def _select(aff_t, cap):
    e, n = aff_t.shape
    n_chunks = n // LANES
    rows = e * n_chunks
    a2 = aff_t.reshape(rows, LANES)
    spec = pl.BlockSpec((rows, LANES), lambda i: (0, 0))
    pos, sel = pl.pallas_call(
        functools.partial(_select_kernel, cap=cap, n_chunks=n_chunks),
        grid=(1,),
        in_specs=[spec],
        out_specs=[spec, spec],
        out_shape=[jax.ShapeDtypeStruct((rows, LANES), I32)] * 2,
        compiler_params=_cparams("arbitrary"),
        name="expert_select",
    )(a2)
    return pos.reshape(e, n), sel.reshape(e, n)


ITEM_VALID, ITEM_FIRST, ITEM_LAST = 1, 2, 4


def _ffn_kernel(blk_ref, tile_ref, flag_ref, slot_ref, h_ref, wg_ref, wu_ref, wd_ref, y_ref, acc_ref,
                *, steps):
    e = pl.program_id(0)
    s = pl.program_id(1)
    flags = flag_ref[e * steps + s]
    tile = tile_ref[e * steps + s]
    rows = acc_ref.shape[0]

    @pl.when((flags & ITEM_FIRST) != 0)
    def _():
        acc_ref[...] = jnp.zeros_like(acc_ref)

    @pl.when((flags & ITEM_VALID) != 0)
    def _():
        slot = slot_ref[0] - tile * rows
        r_id = lax.broadcasted_iota(I32, (rows, slot.shape[1]), 0)
        onehot = jnp.where(r_id == slot, 1.0, 0.0).astype(BF16)
        acc_ref[...] += _dot(onehot, h_ref[...])

    @pl.when((flags & ITEM_LAST) != 0)
    def _():
        xe = acc_ref[...].astype(BF16)
        gate = _dot(xe, wg_ref[0])
        up = _dot(xe, wu_ref[0])
        hid = (gate * jax.nn.sigmoid(gate)) * up
        y_ref[0] = _dot(hid.astype(BF16), wd_ref[0]).astype(BF16)


def _expert_ffn(items, slot3, h2, wg, wu, wd, cap, tb, rt):
    blk, tile, flags, steps = items
    n, d = h2.shape
    nb = n // tb
    e, _, f = wg.shape
    ix = lambda ei, si: ei * steps + si
    grid_spec = pltpu.PrefetchScalarGridSpec(
        num_scalar_prefetch=3,
        grid=(e, steps),
        in_specs=[pl.BlockSpec((1, 1, tb), lambda ei, si, b, t, fl: (ei * nb + b[ix(ei, si)], 0, 0)),
                  pl.BlockSpec((tb, d), lambda ei, si, b, t, fl: (b[ix(ei, si)], 0)),
                  pl.BlockSpec((1, d, f), lambda ei, si, b, t, fl: (ei, 0, 0)),
                  pl.BlockSpec((1, d, f), lambda ei, si, b, t, fl: (ei, 0, 0)),
                  pl.BlockSpec((1, f, d), lambda ei, si, b, t, fl: (ei, 0, 0))],
        out_specs=pl.BlockSpec((1, rt, d), lambda ei, si, b, t, fl: (ei, t[ix(ei, si)], 0)),
        scratch_shapes=[pltpu.VMEM((rt, d), F32)],
    )
    return pl.pallas_call(
        functools.partial(_ffn_kernel, steps=steps),
        grid_spec=grid_spec,
        out_shape=jax.ShapeDtypeStruct((e, cap, d), BF16),
        compiler_params=_cparams("arbitrary", "arbitrary"),
        name="expert_ffn",
    )(blk, tile, flags, slot3, h2, wg, wu, wd)


def _tile_ranges(starts, cap, rt):
    lo, hi = starts[:, :-1], starts[:, 1:]
    cnt = hi - lo
    n_tiles = cap // rt
    t_lo = jnp.minimum(lo // rt, n_tiles - 1)
    t_hi = jnp.where(cnt > 0, (hi - 1) // rt, t_lo)
    return t_lo, jnp.where(cnt > 0, t_hi - t_lo + 1, 0)


def _ffn_items(starts, cap, rt):
    e, nb1 = starts.shape
    nb = nb1 - 1
    steps = nb + cap // rt
    t_lo, nt = _tile_ranges(starts, cap, rt)

    def one(t_lo_e, nt_e):
        offs = jnp.cumsum(nt_e) - nt_e
        total = jnp.sum(nt_e)
        s = jnp.arange(steps, dtype=I32)
        sc = jnp.minimum(s, total - 1)
        b = (jnp.sum((offs[None, :] <= sc[:, None]).astype(I32), axis=1) - 1).astype(I32)
        tile = t_lo_e[b] + (sc - offs[b])
        valid = s < total
        prev_t = jnp.concatenate([jnp.full((1,), -1, I32), tile[:-1]])
        next_t = jnp.concatenate([tile[1:], jnp.full((1,), -1, I32)])
        first = valid & (tile != prev_t)
        last = valid & ((tile != next_t) | (s == total - 1))
        fl = valid * ITEM_VALID + first * ITEM_FIRST + last * ITEM_LAST
        return b, tile.astype(I32), fl.astype(I32)

    b, tile, fl = jax.vmap(one)(t_lo.astype(I32), nt.astype(I32))
    return b.reshape(-1), tile.reshape(-1), fl.reshape(-1), steps


def _combine_kernel(ib_ref, ie_ref, it_ref, fl_ref, slot_ref, aff_ref, y_ref, x_ref, g2_ref, gp_ref,
                    o_ref, acc_ref):
    s = pl.program_id(0)
    flags = fl_ref[s]
    e = ie_ref[s]
    tile = it_ref[s]
    rows = y_ref.shape[1]

    @pl.when((flags & ITEM_FIRST) != 0)
    def _():
        acc_ref[...] = jnp.zeros_like(acc_ref)

    @pl.when((flags & ITEM_VALID) != 0)
    def _():
        lane = lax.broadcasted_iota(I32, slot_ref.shape, 1)
        pick = lane == e
        slot = jnp.sum(jnp.where(pick, slot_ref[...], 0), axis=1, keepdims=True) - tile * rows
        aff = jnp.sum(jnp.where(pick, aff_ref[...], 0.0), axis=1, keepdims=True)
        c_id = lax.broadcasted_iota(I32, (slot.shape[0], rows), 1)
        onehot = jnp.where(c_id == slot, 1.0, 0.0).astype(BF16)
        acc_ref[...] += aff * _dot(onehot, y_ref[0])

    @pl.when((flags & ITEM_LAST) != 0)
    def _():
        o_ref[...] = x_ref[...] + g2_ref[0] * _rms(acc_ref[...], gp_ref[...])


def _combine(items, slot_t, aff, ye, x1, g2, g_post, tb, rt, blocks_per_batch):
    ib, ie, it, fl, steps = items
    n, d = x1.shape
    e = ye.shape[0]
    if g2.shape[0] == 1:
        g2_spec = pl.BlockSpec((1, 1, d), lambda s, b, ee, t, f: (0, 0, 0))
    else:
        g2_spec = pl.BlockSpec((1, 1, d), lambda s, b, ee, t, f: (b[s] // blocks_per_batch, 0, 0))
    grid_spec = pltpu.PrefetchScalarGridSpec(
        num_scalar_prefetch=4,
        grid=(steps,),
        in_specs=[pl.BlockSpec((tb, e), lambda s, b, ee, t, f: (b[s], 0)),
                  pl.BlockSpec((tb, e), lambda s, b, ee, t, f: (b[s], 0)),
                  pl.BlockSpec((1, rt, d), lambda s, b, ee, t, f: (ee[s], t[s], 0)),
                  pl.BlockSpec((tb, d), lambda s, b, ee, t, f: (b[s], 0)),
                  g2_spec,
                  pl.BlockSpec((1, d), lambda s, b, ee, t, f: (0, 0))],
        out_specs=pl.BlockSpec((tb, d), lambda s, b, ee, t, f: (b[s], 0)),
        scratch_shapes=[pltpu.VMEM((tb, d), F32)],
    )
    return pl.pallas_call(
        _combine_kernel,
        grid_spec=grid_spec,
        out_shape=jax.ShapeDtypeStruct((n, d), F32),
        compiler_params=_cparams("arbitrary"),
        name="moe_combine",
    )(ib, ie, it, fl, slot_t, aff, ye, x1, g2, g_post)


def _combine_items(starts, cap, rt):
    e, nb1 = starts.shape
    nb = nb1 - 1
    steps = nb * e + e * (cap // rt - 1)
    t_lo, nt = _tile_ranges(starts, cap, rt)
    nt = jnp.maximum(nt, 1).astype(I32).T.reshape(-1)
    t_lo = t_lo.astype(I32).T.reshape(-1)
    offs = jnp.cumsum(nt) - nt
    total = jnp.sum(nt)
    s = jnp.arange(steps, dtype=I32)
    sc = jnp.minimum(s, total - 1)
    idx = (jnp.sum((offs[None, :] <= sc[:, None]).astype(I32), axis=1) - 1).astype(I32)
    ib = idx // e
    ie = idx % e
    tile = t_lo[idx] + (sc - offs[idx])
    valid = s < total
    prev_b = jnp.concatenate([jnp.full((1,), -1, I32), ib[:-1]])
    next_b = jnp.concatenate([ib[1:], jnp.full((1,), -1, I32)])
    first = valid & (ib != prev_b)
    last = valid & ((ib != next_b) | (s == total - 1))
    fl = valid * ITEM_VALID + first * ITEM_FIRST + last * ITEM_LAST
    return ib, ie, tile.astype(I32), fl.astype(I32), steps


def _rope_tables(t):
    half = QK_ROPE // 2
    pos = jnp.arange(t)
    row = (pos // GRID_W).astype(F32)
    col = (pos % GRID_W).astype(F32)
    inv = ROPE_BASE ** (-jnp.arange(0, half, 2, dtype=F32) / half)
    ang = jnp.concatenate([row[:, None] * inv, col[:, None] * inv], axis=-1)
    cos, sin = jnp.cos(ang), jnp.sin(ang)

    def tabs(first_lane):
        c = jnp.ones((t, HEAD_PAD), F32)
        c = c.at[:, first_lane:first_lane + half].set(cos).at[:, first_lane + half:first_lane + 2 * half].set(cos)
        up = jnp.zeros((t, HEAD_PAD), F32).at[:, first_lane + half:first_lane + 2 * half].set(sin)
        dn = jnp.zeros((t, HEAD_PAD), F32).at[:, first_lane:first_lane + half].set(-sin)
        return c, up, dn

    return tabs(QK_NOPE + QK_ROPE) + tabs(QK_ROPE)


def _layout_weights(w_in, w_uq, w_ukv, w_branch_b, w_up, a_up):
    d = w_in.shape[0]
    deint = jnp.concatenate([jnp.arange(0, QK_ROPE, 2), jnp.arange(1, QK_ROPE, 2)])
    kpe0 = RWKV_COLS + Q_LORA + KV_LORA
    kpe_cols = w_in[:, kpe0:kpe0 + QK_ROPE]
    w_in_p = jnp.concatenate(
        [w_in[:, :kpe0 + QK_ROPE], kpe_cols[:, deint],
         jnp.zeros((d, M_COLS - MLA_COLS - QK_ROPE), F32), w_in[:, RWKV_COLS + MLA_COLS:]], axis=1)
    uq = w_uq.reshape(Q_LORA, B_HEADS, QK_NOPE + QK_ROPE)
    wuq_p = jnp.concatenate([uq, uq[:, :, QK_NOPE:][:, :, deint]], axis=2).reshape(Q_LORA, B_HEADS * HEAD_PAD)
    ukv = w_ukv.reshape(KV_LORA, B_HEADS, QK_NOPE + V_HEAD)
    zpad = jnp.zeros((KV_LORA, B_HEADS, HEAD_PAD - QK_NOPE), F32)
    wk_p = jnp.concatenate([ukv[:, :, :QK_NOPE], zpad], axis=2).reshape(KV_LORA, B_HEADS * HEAD_PAD)
    wv_p = jnp.concatenate([ukv[:, :, QK_NOPE:], zpad], axis=2).reshape(KV_LORA, B_HEADS * HEAD_PAD)
    wb = w_branch_b.reshape(B_HEADS, V_HEAD, d)
    wb_p = jnp.concatenate([wb, jnp.zeros((B_HEADS, HEAD_PAD - V_HEAD, d), F32)], axis=1).reshape(B_HEADS * HEAD_PAD, d)
    eye = jnp.eye(QK_ROPE, dtype=F32)
    z = jnp.zeros((QK_ROPE, QK_ROPE), F32)
    head_raw = jnp.concatenate([jnp.zeros((QK_ROPE, QK_NOPE), F32), eye, z], axis=1)
    head_rot = jnp.concatenate([jnp.zeros((QK_ROPE, QK_NOPE), F32), z, eye], axis=1)
    zrows = jnp.zeros((HEAD_PAD - 2 * QK_ROPE, B_HEADS * HEAD_PAD), F32)
    zr = jnp.zeros((QK_ROPE, B_HEADS * HEAD_PAD), F32)
    place_raw = jnp.concatenate([jnp.tile(head_raw, (1, B_HEADS)), zr, zrows], axis=0)
    place_rot = jnp.concatenate([zr, jnp.tile(head_rot, (1, B_HEADS)), zrows], axis=0)
    zl = jnp.zeros((DECAY_LORA, A_WIDTH), F32)
    wup_p = jnp.stack([jnp.concatenate([w_up[0], zl]), jnp.concatenate([zl, w_up[1]])])
    aup_p = jnp.stack([jnp.concatenate([a_up[0], zl]), jnp.concatenate([zl, a_up[1]])])
    bf = lambda a: a.astype(BF16)
    return (bf(w_in_p), bf(wuq_p), bf(wk_p), bf(wv_p), bf(wb_p), bf(place_raw), bf(place_rot),
            bf(wup_p), bf(aup_p))


def _group_layer(x, mods, lw, s0_f, s0_b, ctx_kv, rope_tabs):
    bsz, t, d = x.shape
    n = bsz * t
    tb = min(TOKEN_BLOCK, t)
    bpb = t // tb
    sh1, sc1, g1, sh2, sc2, g2 = mods
    x2 = x.reshape(n, d)
    u, m, gates = _inproj(x2, sc1, sh1, lw["g_pre_mix"], lw["w_in_p"], tb, bpb)
    prep = _rwkv_prep(u.reshape(bsz, t, RWKV_COLS), lw["shift_w"], lw["w0"], lw["a0"], lw["wup_p"],
                      lw["aup_p"], lw["g_up"], lw["k_k"], lw["k_a"], lw["r_k"], lw["bd"], tb)
    r, v, nkk, kd_f, kd_b, b_f, b_b, lw_f, lw_b, g, bonus = prep
    o_f, s_f = _wkv_scan(r, v, nkk, kd_f, b_f, lw_f, s0_f, reverse=False)
    o_b, s_b = _wkv_scan(r, v, nkk, kd_b, b_b, lw_b, s0_b, reverse=True)
    place = lw["place_raw"] if rope_tabs is None else lw["place_rot"]
    q, k, vv, ckv = _mla_prep(m, lw["g_qnorm"], lw["g_kvnorm"], lw["wuq_p"], lw["wk_p"], lw["wv_p"],
                              place, rope_tabs, tb, t)
    hw = B_HEADS * HEAD_PAD
    k2, v2 = ctx_kv if ctx_kv is not None else (None, None)
    att = _attention(q.reshape(bsz, t, hw), k.reshape(bsz, t, hw), vv.reshape(bsz, t, hw), k2, v2, tb)
    two = lambda a: a.reshape(n, a.shape[-1])
    x1, h2, aff_t = _post_mix(two(o_f), two(o_b), two(bonus), two(g), att.reshape(n, hw), gates, x2,
                              g1, sc2, sh2, lw["ln_x_w"], lw["ln_x_b"], lw["bd"], lw["w_branch_a"],
                              lw["wb_p"], lw["w_out"], lw["g_post_mix"], lw["g_pre_ffn"],
                              lw["w_router_t"], tb, bpb)
    cap = CAPACITY_FACTOR * n // N_EXPERTS
    rt = min(EXPERT_TILE, cap)
    pos, sel = _select(aff_t, cap)
    nb = n // tb
    starts = jnp.concatenate([pos[:, ::tb], jnp.full((N_EXPERTS, 1), cap, I32)], axis=1)
    slot = jnp.where(sel > 0, pos, -1)
    ye = _expert_ffn(_ffn_items(starts, cap, rt), slot.reshape(N_EXPERTS * nb, 1, tb), h2,
                     lw["w_exp_gate"], lw["w_exp_up"], lw["w_exp_down"], cap, tb, rt)
    out = _combine(_combine_items(starts, cap, rt), slot.T, aff_t.T, ye, x1, g2, lw["g_post_ffn"],
                   tb, rt, bpb)
    kpe = m[:, Q_LORA + KV_LORA:Q_LORA + KV_LORA + QK_ROPE]
    return out.reshape(bsz, t, d), (ckv.reshape(bsz, t, KV_LORA), kpe.reshape(bsz, t, QK_ROPE), s_f, s_b)


def kernel(x_prompt, x_sample, cache_ckv, cache_kpe, state_wkv_fwd, state_wkv_bwd, c, c_ctx,
           w_mod, b_mod, g_pre_mix, g_post_mix, g_pre_ffn, g_post_ffn, w_in, shift_w,
           w0, w_up, a0, a_up, g_up, k_k, k_a, r_k, ln_x_w, ln_x_b, w_branch_a,
           g_qnorm, w_uq, g_kvnorm, w_ukv, w_branch_b, w_out,
           w_router, w_exp_gate, w_exp_up, w_exp_down):
    depth = w_mod.shape[0]
    d = x_prompt.shape[-1]
    dec_b, dec_t = x_sample.shape[0], x_sample.shape[1]
    xp, xs = x_prompt, x_sample
    c_rows = jnp.concatenate([c, c_ctx[None, :],
                              jnp.zeros((-(dec_b + 1) % 8, d), F32)], axis=0)
    rope_tabs = _rope_tables(dec_t)
    ii = lax.broadcasted_iota(I32, (A_WIDTH, A_WIDTH), 0) // A_HEAD_DIM
    jj = lax.broadcasted_iota(I32, (A_WIDTH, A_WIDTH), 1) // A_HEAD_DIM
    bd = (ii == jj).astype(BF16)
    row = lambda a: a.reshape(1, -1)
    bf = lambda a: a.astype(BF16)
    ckv_l, kpe_l, sf_l, sb_l = [], [], [], []
    for l in range(depth):
        (w_in_p, wuq_p, wk_p, wv_p, wb_p, place_raw, place_rot, wup_p, aup_p) = _layout_weights(
            w_in[l], w_uq[l], w_ukv[l], w_branch_b[l], w_up[l], a_up[l])
        lw = {
            "g_pre_mix": row(g_pre_mix[l]), "g_post_mix": row(g_post_mix[l]),
            "g_pre_ffn": row(g_pre_ffn[l]), "g_post_ffn": row(g_post_ffn[l]),
            "w_in_p": w_in_p, "shift_w": shift_w[l], "w0": w0[l], "a0": a0[l],
            "wup_p": wup_p, "aup_p": aup_p, "g_up": bf(g_up[l]),
            "k_k": row(k_k[l]), "k_a": row(k_a[l]), "r_k": row(r_k[l]), "bd": bd,
            "ln_x_w": row(ln_x_w[l]), "ln_x_b": row(ln_x_b[l]), "w_branch_a": bf(w_branch_a[l]),
            "g_qnorm": row(g_qnorm[l]), "g_kvnorm": row(g_kvnorm[l]),
            "wuq_p": wuq_p, "wk_p": wk_p, "wv_p": wv_p, "wb_p": wb_p,
            "place_raw": place_raw, "place_rot": place_rot,
            "w_out": bf(w_out[l]), "w_router_t": w_router[l].T,
            "w_exp_gate": bf(w_exp_gate[l]), "w_exp_up": bf(w_exp_up[l]), "w_exp_down": bf(w_exp_down[l]),
        }
        mod = _modulation(c_rows, w_mod[l], b_mod[l])
        mods_lat = [mod[:dec_b, i * d:(i + 1) * d].reshape(dec_b, 1, d) for i in range(6)]
        mods_ctx = [mod[dec_b:dec_b + 1, i * d:(i + 1) * d].reshape(1, 1, d) for i in range(6)]
        zeros_state = jnp.zeros((xp.shape[0], A_HEADS, A_HEAD_DIM, A_HEAD_DIM), F32)
        xp, (ckv, kpe, s_f, s_b) = _group_layer(xp, mods_ctx, lw, zeros_state, zeros_state, None, None)
        ckv_l.append(ckv)
        kpe_l.append(kpe)
        sf_l.append(s_f)
        sb_l.append(s_b)
        past = cache_ckv.shape[2]
        kpe_pad = jnp.concatenate(
            [cache_kpe[:, l], jnp.zeros((dec_b, past, HEAD_PAD - QK_ROPE), F32)], axis=-1)
        k_ctx, v_ctx = _kv_up(cache_ckv[:, l].reshape(dec_b * past, KV_LORA),
                              kpe_pad.reshape(dec_b * past, HEAD_PAD), wk_p, wv_p, place_raw,
                              min(TOKEN_BLOCK, past))
        hw = B_HEADS * HEAD_PAD
        ctx_kv = (k_ctx.reshape(dec_b, past, hw), v_ctx.reshape(dec_b, past, hw))
        xs, _ = _group_layer(xs, mods_lat, lw, state_wkv_fwd[:, l], state_wkv_bwd[:, l], ctx_kv, rope_tabs)
    return (xp, xs, jnp.stack(ckv_l, axis=1), jnp.stack(kpe_l, axis=1),
            jnp.stack(sf_l, axis=1), jnp.stack(sb_l, axis=1))
```

```python
import functools
import math

import jax
import jax.numpy as jnp
from jax import lax
from jax.experimental import pallas as pl
from jax.experimental.pallas import tpu as pltpu

F32 = jnp.float32
BF16 = jnp.bfloat16
I32 = jnp.int32

GRID_W = 64
A_HEADS = 8
A_HEAD_DIM = 64
A_WIDTH = A_HEADS * A_HEAD_DIM
DECAY_LORA = 64
ICLR_LORA = 64
GATE_LORA = 128
DECAY_SCALE = 0.6065306597126334
GN_EPS = 64e-5
B_HEADS = 8
Q_LORA = 256
KV_LORA = 128
QK_NOPE = 64
QK_ROPE = 32
V_HEAD = 64
ROPE_BASE = 10000.0
ATTN_SCALE = 1.0 / math.sqrt(QK_NOPE + QK_ROPE)
N_EXPERTS = 16
CAPACITY_FACTOR = 2
EPS = 1e-6
RWKV_COLS = 3 * A_WIDTH + 2 * DECAY_LORA + 2 * ICLR_LORA + GATE_LORA
MLA_COLS = Q_LORA + KV_LORA + QK_ROPE

LANES = 128
HEAD_PAD = 128
M_COLS = 512
VMEM_LIMIT = 56 * 1024 * 1024

SCAN_CHUNK = 64
SCAN_GROUP = 2
TOKEN_BLOCK = 256
EXPERT_TILE = 256


def _cparams(*sem):
    return pltpu.CompilerParams(dimension_semantics=sem, vmem_limit_bytes=VMEM_LIMIT)


def _dot(a, b):
    return jnp.dot(a, b, preferred_element_type=F32)


def _dot_nt(a, b):
    return lax.dot_general(a, b, (((1,), (1,)), ((), ())), preferred_element_type=F32)


def _dot_tn(a, b):
    return lax.dot_general(a, b, (((0,), (0,)), ((), ())), preferred_element_type=F32)


def _split2(x):
    hi = x.astype(BF16)
    lo = (x - hi.astype(F32)).astype(BF16)
    return hi, lo


def _dot_x2(a, b_bf16):
    hi, lo = _split2(a)
    return _dot(hi, b_bf16) + _dot(lo, b_bf16)


def _dot_f32(a, b):
    ah, al = _split2(a)
    bh, bl = _split2(b)
    return _dot(ah, bh) + (_dot(ah, bl) + _dot(al, bh))


def _rms(x, g):
    return (x * lax.rsqrt(jnp.mean(x * x, axis=-1, keepdims=True) + EPS)) * g


def _mod_kernel(c_ref, w_ref, b_ref, o_ref):
    c = c_ref[...]
    s = c * jax.nn.sigmoid(c)
    o_ref[...] = _dot_f32(s, w_ref[...]) + b_ref[...]


def _modulation(c_rows, w_mod, b_mod):
    rows, d = c_rows.shape
    n_out = w_mod.shape[1]
    tn = n_out // 8
    return pl.pallas_call(
        _mod_kernel,
        grid=(n_out // tn,),
        in_specs=[pl.BlockSpec((rows, d), lambda j: (0, 0)),
                  pl.BlockSpec((d, tn), lambda j: (0, j)),
                  pl.BlockSpec((1, tn), lambda j: (0, j))],
        out_specs=pl.BlockSpec((rows, tn), lambda j: (0, j)),
        out_shape=jax.ShapeDtypeStruct((rows, n_out), F32),
        compiler_params=_cparams("arbitrary"),
        name="modulation",
    )(c_rows, w_mod, b_mod.reshape(1, n_out))


def _inproj_kernel(x_ref, sc_ref, sh_ref, g_ref, w_ref, u_ref, m_ref, gt_ref):
    h = _rms(x_ref[...], g_ref[...]) * (1.0 + sc_ref[0]) + sh_ref[0]
    hb = h.astype(BF16)
    u_ref[...] = _dot(hb, w_ref[:, :RWKV_COLS])
    m_ref[...] = _dot(hb, w_ref[:, RWKV_COLS:RWKV_COLS + M_COLS])
    gt_ref[...] = _dot(hb, w_ref[:, RWKV_COLS + M_COLS:])


def _mod_spec(mod, blocks_per_batch):
    d = mod.shape[-1]
    if mod.shape[0] == 1:
        return pl.BlockSpec((1, 1, d), lambda i: (0, 0, 0))
    return pl.BlockSpec((1, 1, d), lambda i: (i // blocks_per_batch, 0, 0))


def _inproj(x2, sc, sh, g, w_in_p, tb, blocks_per_batch):
    n, d = x2.shape
    cols = w_in_p.shape[1]
    gate_cols = cols - RWKV_COLS - M_COLS
    return pl.pallas_call(
        _inproj_kernel,
        grid=(n // tb,),
        in_specs=[pl.BlockSpec((tb, d), lambda i: (i, 0)),
                  _mod_spec(sc, blocks_per_batch), _mod_spec(sh, blocks_per_batch),
                  pl.BlockSpec((1, d), lambda i: (0, 0)),
                  pl.BlockSpec((d, cols), lambda i: (0, 0))],
        out_specs=[pl.BlockSpec((tb, RWKV_COLS), lambda i: (i, 0)),
                   pl.BlockSpec((tb, M_COLS), lambda i: (i, 0)),
                   pl.BlockSpec((tb, gate_cols), lambda i: (i, 0))],
        out_shape=[jax.ShapeDtypeStruct((n, RWKV_COLS), F32),
                   jax.ShapeDtypeStruct((n, M_COLS), F32),
                   jax.ShapeDtypeStruct((n, gate_cols), F32)],
        compiler_params=_cparams("arbitrary"),
        name="inproj",
    )(x2, sc, sh, g, w_in_p)


def _rwkv_prep_kernel(u_ref, hp_ref, hn_ref, sw_ref, w0_ref, a0_ref, wup_ref, aup_ref, gup_ref,
                      kk_ref, ka_ref, rk_ref, bd_ref,
                      r_o, v_o, nkk_o, kdf_o, kdb_o, bf_o, bb_o, lwf_o, lwb_o, g_o, bonus_o):
    u = u_ref[0]
    tb = u.shape[0]
    row = lax.broadcasted_iota(I32, u.shape, 0)
    prev = jnp.where(row == 0, hp_ref[0, 0], pltpu.roll(u, 1, 0))
    nxt = jnp.where(row == tb - 1, hn_ref[0, 0], pltpu.roll(u, tb - 1, 0))
    xs = sw_ref[0:1] * prev + sw_ref[1:2] * u + sw_ref[2:3] * nxt
    aw = A_WIDTH
    r = xs[:, 0:aw]
    k = xs[:, aw:2 * aw]
    v = xs[:, 2 * aw:3 * aw]
    o = 3 * aw
    dw = xs[:, o:o + 2 * DECAY_LORA]
    da = xs[:, o + 2 * DECAY_LORA:o + 2 * DECAY_LORA + 2 * ICLR_LORA]
    dg = xs[:, o + 2 * DECAY_LORA + 2 * ICLR_LORA:]
    bd = bd_ref[...]
    kkr = k * kk_ref[...]
    kk = kkr * lax.rsqrt(_dot_x2(kkr * kkr, bd) + 1e-12)
    tw = jnp.tanh(dw).astype(BF16)
    dab = da.astype(BF16)
    r_o[0] = r
    v_o[0] = v
    nkk_o[0] = -kk
    for d, (lw_o, kd_o, b_o) in enumerate(((lwf_o, kdf_o, bf_o), (lwb_o, kdb_o, bb_o))):
        lw_o[0] = -DECAY_SCALE * jax.nn.sigmoid(w0_ref[d:d + 1] + _dot(tw, wup_ref[d]))
        a = jax.nn.sigmoid(a0_ref[d:d + 1] + _dot(dab, aup_ref[d]))
        kd_o[0] = k * (1.0 + (a - 1.0) * ka_ref[...])
        b_o[0] = kk * a
    g_o[0] = _dot(jax.nn.sigmoid(dg).astype(BF16), gup_ref[...])
    bonus_o[0] = _dot_x2(r * k * rk_ref[...], bd) * v


def _rwkv_prep(u3, shift_w, w0, a0, wup_p, aup_p, gup, k_k, k_a, r_k, bd, tb):
    b, t, cols = u3.shape
    nb = t // tb
    zero = jnp.zeros((b, 1, cols), F32)
    halo_prev = jnp.concatenate([zero, u3[:, tb - 1:t - 1:tb]], axis=1).reshape(b, nb, 1, cols)
    halo_next = jnp.concatenate([u3[:, tb::tb], zero], axis=1).reshape(b, nb, 1, cols)
    aw = A_WIDTH
    full2 = lambda s: pl.BlockSpec(s, lambda i, j: (0, 0))
    full3 = lambda s: pl.BlockSpec(s, lambda i, j: (0, 0, 0))
    out_spec = pl.BlockSpec((1, tb, aw), lambda i, j: (i, j, 0))
    out_sds = jax.ShapeDtypeStruct((b, t, aw), F32)
    return pl.pallas_call(
        _rwkv_prep_kernel,
        grid=(b, nb),
        in_specs=[pl.BlockSpec((1, tb, cols), lambda i, j: (i, j, 0)),
                  pl.BlockSpec((1, 1, 1, cols), lambda i, j: (i, j, 0, 0)),
                  pl.BlockSpec((1, 1, 1, cols), lambda i, j: (i, j, 0, 0)),
                  full2((3, cols)), full2((2, aw)), full2((2, aw)),
                  full3(wup_p.shape), full3(aup_p.shape), full2(gup.shape),
                  full2((1, aw)), full2((1, aw)), full2((1, aw)), full2((aw, aw))],
        out_specs=[out_spec] * 11,
        out_shape=[out_sds] * 11,
        compiler_params=_cparams("arbitrary", "arbitrary"),
        name="rwkv_prep",
    )(u3, halo_prev, halo_next, shift_w, w0, a0, wup_p, aup_p, gup, k_k, k_a, r_k, bd)


def _bdot(a, b):
    return lax.dot_general(a, b, (((2,), (1,)), ((0,), (0,))), preferred_element_type=F32)


def _bdot_nt(a, b):
    return lax.dot_general(a, b, (((2,), (2,)), ((0,), (0,))), preferred_element_type=F32)


def _bdot_tn(a, b):
    return lax.dot_general(a, b, (((1,), (1,)), ((0,), (0,))), preferred_element_type=F32)


def _split_heads(x):
    return jnp.stack([x[g][:, h * A_HEAD_DIM:(h + 1) * A_HEAD_DIM]
                      for g in range(x.shape[0]) for h in range(A_HEADS)], axis=0)


def _scan_kernel(r_ref, v_ref, nkk_ref, kd_ref, b_ref, lw_ref, s0_ref, o_ref, sf_ref, s_scr, *, reverse):
    c = pl.program_id(1)
    grp, ch, _ = lw_ref.shape
    state_shape = s_scr.shape

    @pl.when(c == 0)
    def _():
        s_scr[...] = s0_ref[...].reshape(state_shape)

    lw = lw_ref[...]
    ri = lax.broadcasted_iota(I32, (ch, ch), 0)
    ci = lax.broadcasted_iota(I32, (ch, ch), 1)
    if reverse:
        incl, strict = ci >= ri, ci > ri
    else:
        incl, strict = ci <= ri, ci < ri
    tri = jnp.where(incl, 1.0, 0.0).astype(BF16)
    hi = lw.astype(BF16)
    rem = lw - hi.astype(F32)
    mid = rem.astype(BF16)
    lo = (rem - mid.astype(F32)).astype(BF16)
    li = jnp.stack([_dot(tri, hi[g]) + (_dot(tri, mid[g]) + _dot(tri, lo[g])) for g in range(grp)], axis=0)
    lt = li[:, 0:1] if reverse else li[:, ch - 1:ch]
    rho = 0.5 * lt
    e1 = jnp.exp(li - rho)
    e2 = jnp.exp(rho - li)
    er = jnp.exp(rho)
    a_rel = nkk_ref[...] * (e1 * jnp.exp(-lw))
    r_rel = r_ref[...] * e1
    b_rel = b_ref[...] * e2
    k_rel = kd_ref[...] * e2
    heads = lambda x: _split_heads(x.astype(BF16))
    a_abs, r_abs, b_end, k_end = heads(a_rel * er), heads(r_rel * er), heads(b_rel * er), heads(k_rel * er)
    a_rel, r_rel, b_rel, k_rel = heads(a_rel), heads(r_rel), heads(b_rel), heads(k_rel)
    v = heads(v_ref[...])
    decay = _split_heads(jnp.exp(lt))
    a_ab = jnp.where(strict[None], _bdot_nt(a_rel, b_rel), 0.0)
    a_ak = jnp.where(strict[None], _bdot_nt(a_rel, k_rel), 0.0).astype(BF16)
    a_rb = jnp.where(incl[None], _bdot_nt(r_rel, b_rel), 0.0).astype(BF16)
    a_rk = jnp.where(incl[None], _bdot_nt(r_rel, k_rel), 0.0).astype(BF16)
    inv = jnp.where(ri == ci, 1.0, 0.0)[None] + a_ab
    lp = a_ab.astype(BF16)
    for _ in range(int(math.log2(ch)) - 1):
        lp = _bdot(lp, lp).astype(BF16)
        inv = inv + _bdot(inv.astype(BF16), lp)
    s = s_scr[...]
    sb = s.astype(BF16)
    x = _bdot_nt(a_abs, sb) + _bdot(a_ak, v)
    u = _bdot(inv.astype(BF16), x.astype(BF16)).astype(BF16)
    o = _bdot_nt(r_abs, sb) + _bdot(a_rb, u) + _bdot(a_rk, v)
    s_scr[...] = s * decay + _bdot_tn(u, b_end) + _bdot_tn(v, k_end)
    for g in range(grp):
        o_ref[g] = jnp.concatenate([o[g * A_HEADS + h] for h in range(A_HEADS)], axis=1)

    @pl.when(c == pl.num_programs(1) - 1)
    def _():
        sf_ref[...] = s_scr[...].reshape(sf_ref.shape)


def _wkv_scan(r, v, nkk, kd, b, lw, s0, reverse):
    bsz, t, aw = r.shape
    ch = min(SCAN_CHUNK, t)
    nc = t // ch
    grp = SCAN_GROUP if bsz % SCAN_GROUP == 0 else 1
    tmap = (lambda i, c: (i, nc - 1 - c, 0)) if reverse else (lambda i, c: (i, c, 0))
    seq = pl.BlockSpec((grp, ch, aw), tmap)
    st = pl.BlockSpec((grp, A_HEADS, A_HEAD_DIM, A_HEAD_DIM), lambda i, c: (i, 0, 0, 0))
    return pl.pallas_call(
        functools.partial(_scan_kernel, reverse=reverse),
        grid=(bsz // grp, nc),
        in_specs=[seq] * 6 + [st],
        out_specs=[seq, st],
        out_shape=[jax.ShapeDtypeStruct((bsz, t, aw), F32),
                   jax.ShapeDtypeStruct(s0.shape, F32)],
        scratch_shapes=[pltpu.VMEM((grp * A_HEADS, A_HEAD_DIM, A_HEAD_DIM), F32)],
        compiler_params=_cparams("arbitrary", "arbitrary"),
        name="wkv_scan_bwd" if reverse else "wkv_scan_fwd",
    )(r, v, nkk, kd, b, lw, s0)


def _rope_lanes(x, cos_t, sin_up, sin_dn):
    w = x.shape[1]
    half = QK_ROPE // 2
    return x * cos_t + pltpu.roll(x, half, 1) * sin_up + pltpu.roll(x, w - half, 1) * sin_dn


def _mla_prep_kernel(*refs, rope):
    if rope:
        (m_ref, gq_ref, gkv_ref, wuq_ref, wk_ref, wv_ref, p_ref,
         qc_ref, qu_ref, qd_ref, kc_ref, ku_ref, kd_ref, q_o, k_o, v_o, ckv_o) = refs
    else:
        (m_ref, gq_ref, gkv_ref, wuq_ref, wk_ref, wv_ref, p_ref, q_o, k_o, v_o, ckv_o) = refs
    m = m_ref[...]
    qn = _rms(m[:, :Q_LORA], gq_ref[...])
    ckv = _rms(m[:, Q_LORA:Q_LORA + KV_LORA], gkv_ref[...])
    kp = m[:, Q_LORA + KV_LORA:]
    q = _dot(qn.astype(BF16), wuq_ref[...]) * ATTN_SCALE
    if rope:
        tile = lambda ref: jnp.concatenate([ref[...]] * B_HEADS, axis=1)
        q = _rope_lanes(q, tile(qc_ref), tile(qu_ref), tile(qd_ref))
        kp = _rope_lanes(kp, kc_ref[...], ku_ref[...], kd_ref[...])
    cb = ckv.astype(BF16)
    q_o[...] = q.astype(BF16)
    k_o[...] = (_dot(cb, wk_ref[...]) + _dot(kp.astype(BF16), p_ref[...])).astype(BF16)
    v_o[...] = _dot(cb, wv_ref[...]).astype(BF16)
    ckv_o[...] = ckv


def _mla_prep(m2, g_q, g_kv, wuq_p, wk_p, wv_p, place, rope_tabs, tb, t):
    n = m2.shape[0]
    hw = B_HEADS * HEAD_PAD
    full = lambda a: pl.BlockSpec(a.shape, lambda i: (0, 0))
    ins = [m2, g_q, g_kv, wuq_p, wk_p, wv_p, place]
    specs = [pl.BlockSpec((tb, M_COLS), lambda i: (i, 0))] + [full(a) for a in ins[1:]]
    if rope_tabs is not None:
        nbt = t // tb
        ins += list(rope_tabs)
        specs += [pl.BlockSpec((tb, HEAD_PAD), lambda i: (i % nbt, 0))] * 6
    big = pl.BlockSpec((tb, hw), lambda i: (i, 0))
    return pl.pallas_call(
        functools.partial(_mla_prep_kernel, rope=rope_tabs is not None),
        grid=(n // tb,),
        in_specs=specs,
        out_specs=[big, big, big, pl.BlockSpec((tb, KV_LORA), lambda i: (i, 0))],
        out_shape=[jax.ShapeDtypeStruct((n, hw), BF16)] * 3 + [jax.ShapeDtypeStruct((n, KV_LORA), F32)],
        compiler_params=_cparams("arbitrary"),
        name="mla_prep",
    )(*ins)


def _kv_up_kernel(ckv_ref, kp_ref, wk_ref, wv_ref, p_ref, k_o, v_o):
    cb = ckv_ref[...].astype(BF16)
    k_o[...] = (_dot(cb, wk_ref[...]) + _dot(kp_ref[...].astype(BF16), p_ref[...])).astype(BF16)
    v_o[...] = _dot(cb, wv_ref[...]).astype(BF16)


def _kv_up(ckv2, kpe_pad, wk_p, wv_p, place, tb):
    n = ckv2.shape[0]
    hw = B_HEADS * HEAD_PAD
    full = lambda a: pl.BlockSpec(a.shape, lambda i: (0, 0))
    big = pl.BlockSpec((tb, hw), lambda i: (i, 0))
    return pl.pallas_call(
        _kv_up_kernel,
        grid=(n // tb,),
        in_specs=[pl.BlockSpec((tb, KV_LORA), lambda i: (i, 0)),
                  pl.BlockSpec((tb, HEAD_PAD), lambda i: (i, 0)),
                  full(wk_p), full(wv_p), full(place)],
        out_specs=[big, big],
        out_shape=[jax.ShapeDtypeStruct((n, hw), BF16)] * 2,
        compiler_params=_cparams("arbitrary"),
        name="kv_up",
    )(ckv2, kpe_pad, wk_p, wv_p, place)


def _attn_kernel(*refs, two):
    if two:
        q_ref, k1_ref, v1_ref, k2_ref, v2_ref, o_ref = refs
    else:
        q_ref, k1_ref, v1_ref, o_ref = refs
    for h in range(B_HEADS):
        hs = slice(h * HEAD_PAD, (h + 1) * HEAD_PAD)
        q = q_ref[0, :, hs]
        s1 = _dot_nt(q, k1_ref[0, :, hs])
        mx = jnp.max(s1, axis=-1, keepdims=True)
        if two:
            s2 = _dot_nt(q, k2_ref[0, :, hs])
            mx = jnp.maximum(mx, jnp.max(s2, axis=-1, keepdims=True))
        p1 = jnp.exp(s1 - mx)
        den = jnp.sum(p1, axis=-1, keepdims=True)
        acc = _dot(p1.astype(BF16), v1_ref[0, :, hs])
        if two:
            p2 = jnp.exp(s2 - mx)
            den = den + jnp.sum(p2, axis=-1, keepdims=True)
            acc = acc + _dot(p2.astype(BF16), v2_ref[0, :, hs])
        o_ref[0, :, hs] = (acc / den).astype(BF16)


def _attention(q3, k1, v1, k2, v2, tq):
    b, t, hw = q3.shape
    two = k2 is not None
    qspec = pl.BlockSpec((1, tq, hw), lambda i, j: (i, j, 0))
    kv = lambda a: pl.BlockSpec((1, a.shape[1], hw), lambda i, j: (i, 0, 0))
    ins = [q3, k1, v1] + ([k2, v2] if two else [])
    return pl.pallas_call(
        functools.partial(_attn_kernel, two=two),
        grid=(b, t // tq),
        in_specs=[qspec] + [kv(a) for a in ins[1:]],
        out_specs=qspec,
        out_shape=jax.ShapeDtypeStruct((b, t, hw), BF16),
        compiler_params=_cparams("arbitrary", "arbitrary"),
        name="attention",
    )(*ins)


def _post_mix_kernel(of_ref, ob_ref, bonus_ref, g_ref, att_ref, gates_ref, x_ref,
                     g1_ref, sc2_ref, sh2_ref, lnw_ref, lnb_ref, bd_ref, wa_ref, wb_ref, wo_ref,
                     gpost_ref, gpre_ref, wr_ref, x1_o, h2_o, aff_o):
    bd = bd_ref[...]
    inv_n = 1.0 / A_HEAD_DIM
    o = of_ref[...] + ob_ref[...]
    mu = _dot_x2(o, bd) * inv_n
    oc = o - mu
    var = _dot_x2(oc * oc, bd) * inv_n
    on = (oc * lax.rsqrt(var + GN_EPS)) * lnw_ref[...] + lnb_ref[...]
    ya = _dot(((on + bonus_ref[...]) * g_ref[...]).astype(BF16), wa_ref[...])
    yb = _dot(att_ref[...], wb_ref[...])
    d = ya.shape[1]
    gates = gates_ref[...]
    mix = jax.nn.sigmoid(gates[:, :d]) * ya + jax.nn.sigmoid(gates[:, d:]) * yb
    y = _dot(mix.astype(BF16), wo_ref[...])
    x1 = x_ref[...] + g1_ref[0] * _rms(y, gpost_ref[...])
    x1_o[...] = x1
    h2 = _rms(x1, gpre_ref[...]) * (1.0 + sc2_ref[0]) + sh2_ref[0]
    h2_o[...] = h2.astype(BF16)
    hh, hl = _split2(h2)
    wh, wl = _split2(wr_ref[...])
    logits = _dot_nt(wh, hh) + (_dot_nt(wh, hl) + _dot_nt(wl, hh))
    z = jnp.exp(logits - jnp.max(logits, axis=0, keepdims=True))
    aff_o[...] = z / jnp.sum(z, axis=0, keepdims=True)


def _post_mix(o_f, o_b, bonus, g, att, gates, x2, g1, sc2, sh2, ln_w, ln_b, bd, wa, wb, wo,
              g_post, g_pre, w_router_t, tb, blocks_per_batch):
    n, d = x2.shape
    tok = lambda a: pl.BlockSpec((tb, a.shape[1]), lambda i: (i, 0))
    full = lambda a: pl.BlockSpec(a.shape, lambda i: (0, 0))
    ms = lambda a: _mod_spec(a, blocks_per_batch)
    return pl.pallas_call(
        _post_mix_kernel,
        grid=(n // tb,),
        in_specs=[tok(o_f), tok(o_b), tok(bonus), tok(g), tok(att), tok(gates), tok(x2),
                  ms(g1), ms(sc2), ms(sh2), full(ln_w), full(ln_b), full(bd), full(wa), full(wb),
                  full(wo), full(g_post), full(g_pre), full(w_router_t)],
        out_specs=[pl.BlockSpec((tb, d), lambda i: (i, 0)),
                   pl.BlockSpec((tb, d), lambda i: (i, 0)),
                   pl.BlockSpec((N_EXPERTS, tb), lambda i: (0, i))],
        out_shape=[jax.ShapeDtypeStruct((n, d), F32),
                   jax.ShapeDtypeStruct((n, d), BF16),
                   jax.ShapeDtypeStruct((N_EXPERTS, n), F32)],
        compiler_params=_cparams("arbitrary"),
        name="post_mix",
    )(o_f, o_b, bonus, g, att, gates, x2, g1, sc2, sh2, ln_w, ln_b, bd, wa, wb, wo,
      g_post, g_pre, w_router_t)


def _select_kernel(a_ref, pos_ref, sel_ref, *, cap, n_chunks):
    a = a_ref[...]
    rows = a.shape[0]
    a3 = a.reshape(N_EXPERTS, n_chunks, LANES)

    def count(mask):
        c = jnp.sum(jnp.where(mask, 1.0, 0.0), axis=2, keepdims=True)
        return jnp.sum(c, axis=1, keepdims=True)

    def body(i, thr):
        cand = thr | jnp.left_shift(jnp.int32(1), 30 - i)
        return jnp.where(count(a3 >= pltpu.bitcast(cand, F32)) >= cap, cand, thr)

    thr = pltpu.bitcast(lax.fori_loop(0, 31, body, jnp.zeros((N_EXPERTS, 1, LANES), I32)), F32)
    gt = a3 > thr
    eq = a3 == thr
    need = cap - count(gt)

    li = lax.broadcasted_iota(I32, (LANES, LANES), 0)
    lj = lax.broadcasted_iota(I32, (LANES, LANES), 1)
    lane_before = jnp.where(li < lj, 1.0, 0.0).astype(BF16)
    lane_all = jnp.ones((LANES, LANES), BF16)
    ci = lax.broadcasted_iota(I32, (n_chunks, n_chunks), 0)
    cj = lax.broadcasted_iota(I32, (n_chunks, n_chunks), 1)
    chunk_before = jnp.where(cj < ci, 1.0, 0.0).astype(BF16)

    def prefix(flags3):
        f2 = flags3.reshape(rows, LANES).astype(BF16)
        within = _dot(f2, lane_before).reshape(N_EXPERTS, n_chunks, LANES)
        tot = _dot(f2, lane_all).astype(BF16).reshape(N_EXPERTS, n_chunks, LANES)
        offs = [_dot(chunk_before, tot[e]) for e in range(N_EXPERTS)]
        return within + jnp.stack(offs, axis=0)

    eqf = jnp.where(eq, 1.0, 0.0)
    tie_ok = jnp.where(prefix(eqf) < need, eqf, 0.0)
    sel = jnp.where(gt, 1.0, tie_ok)
    pos_ref[...] = prefix(sel).reshape(rows, LANES).astype(I32)
    sel_ref[...] = sel.reshape(rows, LANES).astype(I32)


def _select(aff_t, cap):
    e, n = aff_t.shape
    n_chunks = n // LANES
    rows = e * n_chunks
    a2 = aff_t.reshape(rows, LANES)
    spec = pl.BlockSpec((rows, LANES), lambda i: (0, 0))
    pos, sel = pl.pallas_call(
        functools.partial(_select_kernel, cap=cap, n_chunks=n_chunks),
        grid=(1,),
        in_specs=[spec],
        out_specs=[spec, spec],
        out_shape=[jax.ShapeDtypeStruct((rows, LANES), I32)] * 2,
        compiler_params=_cparams("arbitrary"),
        name="expert_select",
    )(a2)
    return pos.reshape(e, n), sel.reshape(e, n)


ITEM_VALID, ITEM_FIRST, ITEM_LAST = 1, 2, 4


def _ffn_kernel(blk_ref, tile_ref, flag_ref, slot_ref, h_ref, wg_ref, wu_ref, wd_ref, y_ref, acc_ref,
                *, steps):
    e = pl.program_id(0)
    s = pl.program_id(1)
    flags = flag_ref[e * steps + s]
    tile = tile_ref[e * steps + s]
    rows = acc_ref.shape[0]

    @pl.when((flags & ITEM_FIRST) != 0)
    def _():
        acc_ref[...] = jnp.zeros_like(acc_ref)

    @pl.when((flags & ITEM_VALID) != 0)
    def _():
        slot = slot_ref[0] - tile * rows
        r_id = lax.broadcasted_iota(I32, (rows, slot.shape[1]), 0)
        onehot = jnp.where(r_id == slot, 1.0, 0.0).astype(BF16)
        acc_ref[...] += _dot(onehot, h_ref[...])

    @pl.when((flags & ITEM_LAST) != 0)
    def _():
        xe = acc_ref[...].astype(BF16)
        gate = _dot(xe, wg_ref[0])
        up = _dot(xe, wu_ref[0])
        hid = (gate * jax.nn.sigmoid(gate)) * up
        y_ref[0] = _dot(hid.astype(BF16), wd_ref[0]).astype(BF16)


def _expert_ffn(items, slot3, h2, wg, wu, wd, cap, tb, rt):
    blk, tile, flags, steps = items
    n, d = h2.shape
    nb = n // tb
    e, _, f = wg.shape
    ix = lambda ei, si: ei * steps + si
    grid_spec = pltpu.PrefetchScalarGridSpec(
        num_scalar_prefetch=3,
        grid=(e, steps),
        in_specs=[pl.BlockSpec((1, 1, tb), lambda ei, si, b, t, fl: (ei * nb + b[ix(ei, si)], 0, 0)),
                  pl.BlockSpec((tb, d), lambda ei, si, b, t, fl: (b[ix(ei, si)], 0)),
                  pl.BlockSpec((1, d, f), lambda ei, si, b, t, fl: (ei, 0, 0)),
                  pl.BlockSpec((1, d, f), lambda ei, si, b, t, fl: (ei, 0, 0)),
                  pl.BlockSpec((1, f, d), lambda ei, si, b, t, fl: (ei, 0, 0))],
        out_specs=pl.BlockSpec((1, rt, d), lambda ei, si, b, t, fl: (ei, t[ix(ei, si)], 0)),
        scratch_shapes=[pltpu.VMEM((rt, d), F32)],
    )
    return pl.pallas_call(
        functools.partial(_ffn_kernel, steps=steps),
        grid_spec=grid_spec,
        out_shape=jax.ShapeDtypeStruct((e, cap, d), BF16),
        compiler_params=_cparams("arbitrary", "arbitrary"),
        name="expert_ffn",
    )(blk, tile, flags, slot3, h2, wg, wu, wd)


def _tile_ranges(starts, cap, rt):
    lo, hi = starts[:, :-1], starts[:, 1:]
    cnt = hi - lo
    n_tiles = cap // rt
    t_lo = jnp.minimum(lo // rt, n_tiles - 1)
    t_hi = jnp.where(cnt > 0, (hi - 1) // rt, t_lo)
    return t_lo, jnp.where(cnt > 0, t_hi - t_lo + 1, 0)


def _ffn_items(starts, cap, rt):
    e, nb1 = starts.shape
    nb = nb1 - 1
    steps = nb + cap // rt
    t_lo, nt = _tile_ranges(starts, cap, rt)

    def one(t_lo_e, nt_e):
        offs = jnp.cumsum(nt_e) - nt_e
        total = jnp.sum(nt_e)
        s = jnp.arange(steps, dtype=I32)
        sc = jnp.minimum(s, total - 1)
        b = (jnp.sum((offs[None, :] <= sc[:, None]).astype(I32), axis=1) - 1).astype(I32)
        tile = t_lo_e[b] + (sc - offs[b])
        valid = s < total
        prev_t = jnp.concatenate([jnp.full((1,), -1, I32), tile[:-1]])
        next_t = jnp.concatenate([tile[1:], jnp.full((1,), -1, I32)])
        first = valid & (tile != prev_t)
        last = valid & ((tile != next_t) | (s == total - 1))
        fl = valid * ITEM_VALID + first * ITEM_FIRST + last * ITEM_LAST
        return b, tile.astype(I32), fl.astype(I32)

    b, tile, fl = jax.vmap(one)(t_lo.astype(I32), nt.astype(I32))
    return b.reshape(-1), tile.reshape(-1), fl.reshape(-1), steps


def _combine_kernel(ib_ref, ie_ref, it_ref, fl_ref, slot_ref, aff_ref, y_ref, x_ref, g2_ref, gp_ref,
                    o_ref, acc_ref):
    s = pl.program_id(0)
    flags = fl_ref[s]
    e = ie_ref[s]
    tile = it_ref[s]
    rows = y_ref.shape[1]

    @pl.when((flags & ITEM_FIRST) != 0)
    def _():
        acc_ref[...] = jnp.zeros_like(acc_ref)

    @pl.when((flags & ITEM_VALID) != 0)
    def _():
        lane = lax.broadcasted_iota(I32, slot_ref.shape, 1)
        pick = lane == e
        slot = jnp.sum(jnp.where(pick, slot_ref[...], 0), axis=1, keepdims=True) - tile * rows
        aff = jnp.sum(jnp.where(pick, aff_ref[...], 0.0), axis=1, keepdims=True)
        c_id = lax.broadcasted_iota(I32, (slot.shape[0], rows), 1)
        onehot = jnp.where(c_id == slot, 1.0, 0.0).astype(BF16)
        acc_ref[...] += aff * _dot(onehot, y_ref[0])

    @pl.when((flags & ITEM_LAST) != 0)
    def _():
        o_ref[...] = x_ref[...] + g2_ref[0] * _rms(acc_ref[...], gp_ref[...])


def _combine(items, slot_t, aff, ye, x1, g2, g_post, tb, rt, blocks_per_batch):
    ib, ie, it, fl, steps = items
    n, d = x1.shape
    e = ye.shape[0]
    if g2.shape[0] == 1:
        g2_spec = pl.BlockSpec((1, 1, d), lambda s, b, ee, t, f: (0, 0, 0))
    else:
        g2_spec = pl.BlockSpec((1, 1, d), lambda s, b, ee, t, f: (b[s] // blocks_per_batch, 0, 0))
    grid_spec = pltpu.PrefetchScalarGridSpec(
        num_scalar_prefetch=4,
        grid=(steps,),
        in_specs=[pl.BlockSpec((tb, e), lambda s, b, ee, t, f: (b[s], 0)),
                  pl.BlockSpec((tb, e), lambda s, b, ee, t, f: (b[s], 0)),
                  pl.BlockSpec((1, rt, d), lambda s, b, ee, t, f: (ee[s], t[s], 0)),
                  pl.BlockSpec((tb, d), lambda s, b, ee, t, f: (b[s], 0)),
                  g2_spec,
                  pl.BlockSpec((1, d), lambda s, b, ee, t, f: (0, 0))],
        out_specs=pl.BlockSpec((tb, d), lambda s, b, ee, t, f: (b[s], 0)),
        scratch_shapes=[pltpu.VMEM((tb, d), F32)],
    )
    return pl.pallas_call(
        _combine_kernel,
        grid_spec=grid_spec,
        out_shape=jax.ShapeDtypeStruct((n, d), F32),
        compiler_params=_cparams("arbitrary"),
        name="moe_combine",
    )(ib, ie, it, fl, slot_t, aff, ye, x1, g2, g_post)


def _combine_items(starts, cap, rt):
    e, nb1 = starts.shape
    nb = nb1 - 1
    steps = nb * e + e * (cap // rt - 1)
    t_lo, nt = _tile_ranges(starts, cap, rt)
    nt = jnp.maximum(nt, 1).astype(I32).T.reshape(-1)
    t_lo = t_lo.astype(I32).T.reshape(-1)
    offs = jnp.cumsum(nt) - nt
    total = jnp.sum(nt)
    s = jnp.arange(steps, dtype=I32)
    sc = jnp.minimum(s, total - 1)
    idx = (jnp.sum((offs[None, :] <= sc[:, None]).astype(I32), axis=1) - 1).astype(I32)
    ib = idx // e
    ie = idx % e
    tile = t_lo[idx] + (sc - offs[idx])
    valid = s < total
    prev_b = jnp.concatenate([jnp.full((1,), -1, I32), ib[:-1]])
    next_b = jnp.concatenate([ib[1:], jnp.full((1,), -1, I32)])
    first = valid & (ib != prev_b)
    last = valid & ((ib != next_b) | (s == total - 1))
    fl = valid * ITEM_VALID + first * ITEM_FIRST + last * ITEM_LAST
    return ib, ie, tile.astype(I32), fl.astype(I32), steps


def _rope_tables(t):
    half = QK_ROPE // 2
    pos = jnp.arange(t)
    row = (pos // GRID_W).astype(F32)
    col = (pos % GRID_W).astype(F32)
    inv = ROPE_BASE ** (-jnp.arange(0, half, 2, dtype=F32) / half)
    ang = jnp.concatenate([row[:, None] * inv, col[:, None] * inv], axis=-1)
    cos, sin = jnp.cos(ang), jnp.sin(ang)

    def tabs(first_lane):
        c = jnp.ones((t, HEAD_PAD), F32)
        c = c.at[:, first_lane:first_lane + half].set(cos).at[:, first_lane + half:first_lane + 2 * half].set(cos)
        up = jnp.zeros((t, HEAD_PAD), F32).at[:, first_lane + half:first_lane + 2 * half].set(sin)
        dn = jnp.zeros((t, HEAD_PAD), F32).at[:, first_lane:first_lane + half].set(-sin)
        return c, up, dn

    return tabs(QK_NOPE + QK_ROPE) + tabs(QK_ROPE)


def _layout_weights(w_in, w_uq, w_ukv, w_branch_b, w_up, a_up):
    d = w_in.shape[0]
    deint = jnp.concatenate([jnp.arange(0, QK_ROPE, 2), jnp.arange(1, QK_ROPE, 2)])
    kpe0 = RWKV_COLS + Q_LORA + KV_LORA
    kpe_cols = w_in[:, kpe0:kpe0 + QK_ROPE]
    w_in_p = jnp.concatenate(
        [w_in[:, :kpe0 + QK_ROPE], kpe_cols[:, deint],
         jnp.zeros((d, M_COLS - MLA_COLS - QK_ROPE), F32), w_in[:, RWKV_COLS + MLA_COLS:]], axis=1)
    uq = w_uq.reshape(Q_LORA, B_HEADS, QK_NOPE + QK_ROPE)
    wuq_p = jnp.concatenate([uq, uq[:, :, QK_NOPE:][:, :, deint]], axis=2).reshape(Q_LORA, B_HEADS * HEAD_PAD)
    ukv = w_ukv.reshape(KV_LORA, B_HEADS, QK_NOPE + V_HEAD)
    zpad = jnp.zeros((KV_LORA, B_HEADS, HEAD_PAD - QK_NOPE), F32)
    wk_p = jnp.concatenate([ukv[:, :, :QK_NOPE], zpad], axis=2).reshape(KV_LORA, B_HEADS * HEAD_PAD)
    wv_p = jnp.concatenate([ukv[:, :, QK_NOPE:], zpad], axis=2).reshape(KV_LORA, B_HEADS * HEAD_PAD)
    wb = w_branch_b.reshape(B_HEADS, V_HEAD, d)
    wb_p = jnp.concatenate([wb, jnp.zeros((B_HEADS, HEAD_PAD - V_HEAD, d), F32)], axis=1).reshape(B_HEADS * HEAD_PAD, d)
    eye = jnp.eye(QK_ROPE, dtype=F32)
    z = jnp.zeros((QK_ROPE, QK_ROPE), F32)
    head_raw = jnp.concatenate([jnp.zeros((QK_ROPE, QK_NOPE), F32), eye, z], axis=1)
    head_rot = jnp.concatenate([jnp.zeros((QK_ROPE, QK_NOPE), F32), z, eye], axis=1)
    zrows = jnp.zeros((HEAD_PAD - 2 * QK_ROPE, B_HEADS * HEAD_PAD), F32)
    zr = jnp.zeros((QK_ROPE, B_HEADS * HEAD_PAD), F32)
    place_raw = jnp.concatenate([jnp.tile(head_raw, (1, B_HEADS)), zr, zrows], axis=0)
    place_rot = jnp.concatenate([zr, jnp.tile(head_rot, (1, B_HEADS)), zrows], axis=0)
    zl = jnp.zeros((DECAY_LORA, A_WIDTH), F32)
    wup_p = jnp.stack([jnp.concatenate([w_up[0], zl]), jnp.concatenate([zl, w_up[1]])])
    aup_p = jnp.stack([jnp.concatenate([a_up[0], zl]), jnp.concatenate([zl, a_up[1]])])
    bf = lambda a: a.astype(BF16)
    return (bf(w_in_p), bf(wuq_p), bf(wk_p), bf(wv_p), bf(wb_p), bf(place_raw), bf(place_rot),
            bf(wup_p), bf(aup_p))


def _group_layer(x, mods, lw, s0_f, s0_b, ctx_kv, rope_tabs):
    bsz, t, d = x.shape
    n = bsz * t
    tb = min(TOKEN_BLOCK, t)
    bpb = t // tb
    sh1, sc1, g1, sh2, sc2, g2 = mods
    x2 = x.reshape(n, d)
    u, m, gates = _inproj(x2, sc1, sh1, lw["g_pre_mix"], lw["w_in_p"], tb, bpb)
    prep = _rwkv_prep(u.reshape(bsz, t, RWKV_COLS), lw["shift_w"], lw["w0"], lw["a0"], lw["wup_p"],
                      lw["aup_p"], lw["g_up"], lw["k_k"], lw["k_a"], lw["r_k"], lw["bd"], tb)
    r, v, nkk, kd_f, kd_b, b_f, b_b, lw_f, lw_b, g, bonus = prep
    o_f, s_f = _wkv_scan(r, v, nkk, kd_f, b_f, lw_f, s0_f, reverse=False)
    o_b, s_b = _wkv_scan(r, v, nkk, kd_b, b_b, lw_b, s0_b, reverse=True)
    place = lw["place_raw"] if rope_tabs is None else lw["place_rot"]
    q, k, vv, ckv = _mla_prep(m, lw["g_qnorm"], lw["g_kvnorm"], lw["wuq_p"], lw["wk_p"], lw["wv_p"],
                              place, rope_tabs, tb, t)
    hw = B_HEADS * HEAD_PAD
    k2, v2 = ctx_kv if ctx_kv is not None else (None, None)
    att = _attention(q.reshape(bsz, t, hw), k.reshape(bsz, t, hw), vv.reshape(bsz, t, hw), k2, v2, tb)
    two = lambda a: a.reshape(n, a.shape[-1])
    x1, h2, aff_t = _post_mix(two(o_f), two(o_b), two(bonus), two(g), att.reshape(n, hw), gates, x2,
                              g1, sc2, sh2, lw["ln_x_w"], lw["ln_x_b"], lw["bd"], lw["w_branch_a"],
                              lw["wb_p"], lw["w_out"], lw["g_post_mix"], lw["g_pre_ffn"],
                              lw["w_router_t"], tb, bpb)
    cap = CAPACITY_FACTOR * n // N_EXPERTS
    rt = min(EXPERT_TILE, cap)
    pos, sel = _select(aff_t, cap)
    nb = n // tb
    starts = jnp.concatenate([pos[:, ::tb], jnp.full((N_EXPERTS, 1), cap, I32)], axis=1)
    slot = jnp.where(sel > 0, pos, -1)
    ye = _expert_ffn(_ffn_items(starts, cap, rt), slot.reshape(N_EXPERTS * nb, 1, tb), h2,
                     lw["w_exp_gate"], lw["w_exp_up"], lw["w_exp_down"], cap, tb, rt)
    out = _combine(_combine_items(starts, cap, rt), slot.T, aff_t.T, ye, x1, g2, lw["g_post_ffn"],
                   tb, rt, bpb)
    kpe = m[:, Q_LORA + KV_LORA:Q_LORA + KV_LORA + QK_ROPE]
    return out.reshape(bsz, t, d), (ckv.reshape(bsz, t, KV_LORA), kpe.reshape(bsz, t, QK_ROPE), s_f, s_b)


def kernel(x_prompt, x_sample, cache_ckv, cache_kpe, state_wkv_fwd, state_wkv_bwd, c, c_ctx,
           w_mod, b_mod, g_pre_mix, g_post_mix, g_pre_ffn, g_post_ffn, w_in, shift_w,
           w0, w_up, a0, a_up, g_up, k_k, k_a, r_k, ln_x_w, ln_x_b, w_branch_a,
           g_qnorm, w_uq, g_kvnorm, w_ukv, w_branch_b, w_out,
           w_router, w_exp_gate, w_exp_up, w_exp_down):
    depth = w_mod.shape[0]
    d = x_prompt.shape[-1]
    dec_b, dec_t = x_sample.shape[0], x_sample.shape[1]
    xp, xs = x_prompt, x_sample
    c_rows = jnp.concatenate([c, c_ctx[None, :],
                              jnp.zeros((-(dec_b + 1) % 8, d), F32)], axis=0)
    rope_tabs = _rope_tables(dec_t)
    ii = lax.broadcasted_iota(I32, (A_WIDTH, A_WIDTH), 0) // A_HEAD_DIM
    jj = lax.broadcasted_iota(I32, (A_WIDTH, A_WIDTH), 1) // A_HEAD_DIM
    bd = (ii == jj).astype(BF16)
    row = lambda a: a.reshape(1, -1)
    bf = lambda a: a.astype(BF16)
    ckv_l, kpe_l, sf_l, sb_l = [], [], [], []
    for l in range(depth):
        (w_in_p, wuq_p, wk_p, wv_p, wb_p, place_raw, place_rot, wup_p, aup_p) = _layout_weights(
            w_in[l], w_uq[l], w_ukv[l], w_branch_b[l], w_up[l], a_up[l])
        lw = {
            "g_pre_mix": row(g_pre_mix[l]), "g_post_mix": row(g_post_mix[l]),
            "g_pre_ffn": row(g_pre_ffn[l]), "g_post_ffn": row(g_post_ffn[l]),
            "w_in_p": w_in_p, "shift_w": shift_w[l], "w0": w0[l], "a0": a0[l],
            "wup_p": wup_p, "aup_p": aup_p, "g_up": bf(g_up[l]),
            "k_k": row(k_k[l]), "k_a": row(k_a[l]), "r_k": row(r_k[l]), "bd": bd,
            "ln_x_w": row(ln_x_w[l]), "ln_x_b": row(ln_x_b[l]), "w_branch_a": bf(w_branch_a[l]),
            "g_qnorm": row(g_qnorm[l]), "g_kvnorm": row(g_kvnorm[l]),
            "wuq_p": wuq_p, "wk_p": wk_p, "wv_p": wv_p, "wb_p": wb_p,
            "place_raw": place_raw, "place_rot": place_rot,
            "w_out": bf(w_out[l]), "w_router_t": w_router[l].T,
            "w_exp_gate": bf(w_exp_gate[l]), "w_exp_up": bf(w_exp_up[l]), "w_exp_down": bf(w_exp_down[l]),
        }
        mod = _modulation(c_rows, w_mod[l], b_mod[l])
        mods_lat = [mod[:dec_b, i * d:(i + 1) * d].reshape(dec_b, 1, d) for i in range(6)]
        mods_ctx = [mod[dec_b:dec_b + 1, i * d:(i + 1) * d].reshape(1, 1, d) for i in range(6)]
        zeros_state = jnp.zeros((xp.shape[0], A_HEADS, A_HEAD_DIM, A_HEAD_DIM), F32)
        xp, (ckv, kpe, s_f, s_b) = _group_layer(xp, mods_ctx, lw, zeros_state, zeros_state, None, None)
        ckv_l.append(ckv)
        kpe_l.append(kpe)
        sf_l.append(s_f)
        sb_l.append(s_b)
        past = cache_ckv.shape[2]
        kpe_pad = jnp.concatenate(
            [cache_kpe[:, l], jnp.zeros((dec_b, past, HEAD_PAD - QK_ROPE), F32)], axis=-1)
        k_ctx, v_ctx = _kv_up(cache_ckv[:, l].reshape(dec_b * past, KV_LORA),
                              kpe_pad.reshape(dec_b * past, HEAD_PAD), wk_p, wv_p, place_raw,
                              min(TOKEN_BLOCK, past))
        hw = B_HEADS * HEAD_PAD
        ctx_kv = (k_ctx.reshape(dec_b, past, hw), v_ctx.reshape(dec_b, past, hw))
        xs, _ = _group_layer(xs, mods_lat, lw, state_wkv_fwd[:, l], state_wkv_bwd[:, l], ctx_kv, rope_tabs)
    return (xp, xs, jnp.stack(ckv_l, axis=1), jnp.stack(kpe_l, axis=1),
            jnp.stack(sf_l, axis=1), jnp.stack(sb_l, axis=1))
```

```python
import functools
import math

import jax
import jax.numpy as jnp
from jax import lax
from jax.experimental import pallas as pl
from jax.experimental.pallas import tpu as pltpu

F32 = jnp.float32
BF16 = jnp.bfloat16
I32 = jnp.int32

GRID_W = 64
A_HEADS = 8
A_HEAD_DIM = 64
A_WIDTH = A_HEADS * A_HEAD_DIM
DECAY_LORA = 64
ICLR_LORA = 64
GATE_LORA = 128
DECAY_SCALE = 0.6065306597126334
GN_EPS = 64e-5
B_HEADS = 8
Q_LORA = 256
KV_LORA = 128
QK_NOPE = 64
QK_ROPE = 32
V_HEAD = 64
ROPE_BASE = 10000.0
ATTN_SCALE = 1.0 / math.sqrt(QK_NOPE + QK_ROPE)
N_EXPERTS = 16
CAPACITY_FACTOR = 2
EPS = 1e-6
RWKV_COLS = 3 * A_WIDTH + 2 * DECAY_LORA + 2 * ICLR_LORA + GATE_LORA
MLA_COLS = Q_LORA + KV_LORA + QK_ROPE

LANES = 128
HEAD_PAD = 128
M_COLS = 512
VMEM_LIMIT = 56 * 1024 * 1024

SCAN_CHUNK = 64
SCAN_GROUP = 4
TOKEN_BLOCK = 256
EXPERT_TILE = 256
DISPATCH_BLOCK = 1024


def _cparams(*sem):
    return pltpu.CompilerParams(dimension_semantics=sem, vmem_limit_bytes=VMEM_LIMIT)


def _dot(a, b):
    return jnp.dot(a, b, preferred_element_type=F32)


def _dot_nt(a, b):
    return lax.dot_general(a, b, (((1,), (1,)), ((), ())), preferred_element_type=F32)


def _dot_tn(a, b):
    return lax.dot_general(a, b, (((0,), (0,)), ((), ())), preferred_element_type=F32)


def _split2(x):
    hi = x.astype(BF16)
    lo = (x - hi.astype(F32)).astype(BF16)
    return hi, lo


def _dot_x2(a, b_bf16):
    hi, lo = _split2(a)
    return _dot(hi, b_bf16) + _dot(lo, b_bf16)


def _dot_f32(a, b):
    ah, al = _split2(a)
    bh, bl = _split2(b)
    return _dot(ah, bh) + (_dot(ah, bl) + _dot(al, bh))


def _rms(x, g):
    return (x * lax.rsqrt(jnp.mean(x * x, axis=-1, keepdims=True) + EPS)) * g


def _mod_kernel(c_ref, w_ref, b_ref, o_ref):
    c = c_ref[...]
    s = c * jax.nn.sigmoid(c)
    o_ref[...] = _dot_f32(s, w_ref[...]) + b_ref[...]


def _modulation(c_rows, w_mod, b_mod):
    rows, d = c_rows.shape
    n_out = w_mod.shape[1]
    tn = n_out // 8
    return pl.pallas_call(
        _mod_kernel,
        grid=(n_out // tn,),
        in_specs=[pl.BlockSpec((rows, d), lambda j: (0, 0)),
                  pl.BlockSpec((d, tn), lambda j: (0, j)),
                  pl.BlockSpec((1, tn), lambda j: (0, j))],
        out_specs=pl.BlockSpec((rows, tn), lambda j: (0, j)),
        out_shape=jax.ShapeDtypeStruct((rows, n_out), F32),
        compiler_params=_cparams("arbitrary"),
        name="modulation",
    )(c_rows, w_mod, b_mod.reshape(1, n_out))


def _inproj_kernel(x_ref, sc_ref, sh_ref, g_ref, w_ref, u_ref, m_ref, gt_ref):
    h = _rms(x_ref[...], g_ref[...]) * (1.0 + sc_ref[0]) + sh_ref[0]
    hb = h.astype(BF16)
    u_ref[...] = _dot(hb, w_ref[:, :RWKV_COLS])
    m_ref[...] = _dot(hb, w_ref[:, RWKV_COLS:RWKV_COLS + M_COLS])
    gt_ref[...] = _dot(hb, w_ref[:, RWKV_COLS + M_COLS:])


def _mod_spec(mod, blocks_per_batch):
    d = mod.shape[-1]
    if mod.shape[0] == 1:
        return pl.BlockSpec((1, 1, d), lambda i: (0, 0, 0))
    return pl.BlockSpec((1, 1, d), lambda i: (i // blocks_per_batch, 0, 0))


def _inproj(x2, sc, sh, g, w_in_p, tb, blocks_per_batch):
    n, d = x2.shape
    cols = w_in_p.shape[1]
    gate_cols = cols - RWKV_COLS - M_COLS
    return pl.pallas_call(
        _inproj_kernel,
        grid=(n // tb,),
        in_specs=[pl.BlockSpec((tb, d), lambda i: (i, 0)),
                  _mod_spec(sc, blocks_per_batch), _mod_spec(sh, blocks_per_batch),
                  pl.BlockSpec((1, d), lambda i: (0, 0)),
                  pl.BlockSpec((d, cols), lambda i: (0, 0))],
        out_specs=[pl.BlockSpec((tb, RWKV_COLS), lambda i: (i, 0)),
                   pl.BlockSpec((tb, M_COLS), lambda i: (i, 0)),
                   pl.BlockSpec((tb, gate_cols), lambda i: (i, 0))],
        out_shape=[jax.ShapeDtypeStruct((n, RWKV_COLS), F32),
                   jax.ShapeDtypeStruct((n, M_COLS), F32),
                   jax.ShapeDtypeStruct((n, gate_cols), F32)],
        compiler_params=_cparams("arbitrary"),
        name="inproj",
    )(x2, sc, sh, g, w_in_p)


def _rwkv_prep_kernel(u_ref, hp_ref, hn_ref, sw_ref, w0_ref, a0_ref, wup_ref, aup_ref, gup_ref,
                      kk_ref, ka_ref, rk_ref, bd_ref,
                      r_o, v_o, nkk_o, kdf_o, kdb_o, bf_o, bb_o, lwf_o, lwb_o, g_o, bonus_o):
    u = u_ref[0]
    tb = u.shape[0]
    row = lax.broadcasted_iota(I32, u.shape, 0)
    prev = jnp.where(row == 0, hp_ref[0, 0], pltpu.roll(u, 1, 0))
    nxt = jnp.where(row == tb - 1, hn_ref[0, 0], pltpu.roll(u, tb - 1, 0))
    xs = sw_ref[0:1] * prev + sw_ref[1:2] * u + sw_ref[2:3] * nxt
    aw = A_WIDTH
    r = xs[:, 0:aw]
    k = xs[:, aw:2 * aw]
    v = xs[:, 2 * aw:3 * aw]
    o = 3 * aw
    dw = xs[:, o:o + 2 * DECAY_LORA]
    da = xs[:, o + 2 * DECAY_LORA:o + 2 * DECAY_LORA + 2 * ICLR_LORA]
    dg = xs[:, o + 2 * DECAY_LORA + 2 * ICLR_LORA:]
    bd = bd_ref[...]
    kkr = k * kk_ref[...]
    kk = kkr * lax.rsqrt(_dot_x2(kkr * kkr, bd) + 1e-12)
    tw = jnp.tanh(dw).astype(BF16)
    dab = da.astype(BF16)
    r_o[0] = r
    v_o[0] = v
    nkk_o[0] = -kk
    for d, (lw_o, kd_o, b_o) in enumerate(((lwf_o, kdf_o, bf_o), (lwb_o, kdb_o, bb_o))):
        lw_o[0] = -DECAY_SCALE * jax.nn.sigmoid(w0_ref[d:d + 1] + _dot(tw, wup_ref[d]))
        a = jax.nn.sigmoid(a0_ref[d:d + 1] + _dot(dab, aup_ref[d]))
        kd_o[0] = k * (1.0 + (a - 1.0) * ka_ref[...])
        b_o[0] = kk * a
    g_o[0] = _dot(jax.nn.sigmoid(dg).astype(BF16), gup_ref[...])
    bonus_o[0] = _dot_x2(r * k * rk_ref[...], bd) * v


def _rwkv_prep(u3, shift_w, w0, a0, wup_p, aup_p, gup, k_k, k_a, r_k, bd, tb):
    b, t, cols = u3.shape
    nb = t // tb
    zero = jnp.zeros((b, 1, cols), F32)
    halo_prev = jnp.concatenate([zero, u3[:, tb - 1:t - 1:tb]], axis=1).reshape(b, nb, 1, cols)
    halo_next = jnp.concatenate([u3[:, tb::tb], zero], axis=1).reshape(b, nb, 1, cols)
    aw = A_WIDTH
    full2 = lambda s: pl.BlockSpec(s, lambda i, j: (0, 0))
    full3 = lambda s: pl.BlockSpec(s, lambda i, j: (0, 0, 0))
    out_spec = pl.BlockSpec((1, tb, aw), lambda i, j: (i, j, 0))
    out_sds = jax.ShapeDtypeStruct((b, t, aw), F32)
    return pl.pallas_call(
        _rwkv_prep_kernel,
        grid=(b, nb),
        in_specs=[pl.BlockSpec((1, tb, cols), lambda i, j: (i, j, 0)),
                  pl.BlockSpec((1, 1, 1, cols), lambda i, j: (i, j, 0, 0)),
                  pl.BlockSpec((1, 1, 1, cols), lambda i, j: (i, j, 0, 0)),
                  full2((3, cols)), full2((2, aw)), full2((2, aw)),
                  full3(wup_p.shape), full3(aup_p.shape), full2(gup.shape),
                  full2((1, aw)), full2((1, aw)), full2((1, aw)), full2((aw, aw))],
        out_specs=[out_spec] * 11,
        out_shape=[out_sds] * 11,
        compiler_params=_cparams("arbitrary", "arbitrary"),
        name="rwkv_prep",
    )(u3, halo_prev, halo_next, shift_w, w0, a0, wup_p, aup_p, gup, k_k, k_a, r_k, bd)


def _bdot(a, b):
    return lax.dot_general(a, b, (((2,), (1,)), ((0,), (0,))), preferred_element_type=F32)


def _bdot_nt(a, b):
    return lax.dot_general(a, b, (((2,), (2,)), ((0,), (0,))), preferred_element_type=F32)


def _bdot_tn(a, b):
    return lax.dot_general(a, b, (((1,), (1,)), ((0,), (0,))), preferred_element_type=F32)


def _split_pairs(x):
    return jnp.stack([x[g][:, p * LANES:(p + 1) * LANES]
                      for g in range(x.shape[0]) for p in range(x.shape[2] // LANES)], axis=0)


def _scan_kernel(r_ref, v_ref, nkk_ref, kd_ref, b_ref, lw_ref, s0_ref, o_ref, sf_ref, s_scr, *, reverse):
    c = pl.program_id(1)
    grp, ch, aw = lw_ref.shape
    n_pairs = aw // LANES
    state_shape = s_scr.shape

    @pl.when(c == 0)
    def _():
        s_scr[...] = s0_ref[...].reshape(state_shape)

    lw = lw_ref[...]
    ri = lax.broadcasted_iota(I32, (ch, ch), 0)
    ci = lax.broadcasted_iota(I32, (ch, ch), 1)
    tri = jnp.where((ci >= ri) if reverse else (ci <= ri), 1.0, 0.0).astype(BF16)
    hi = lw.astype(BF16)
    rem = lw - hi.astype(F32)
    mid = rem.astype(BF16)
    lo = (rem - mid.astype(F32)).astype(BF16)
    li = jnp.stack([_dot(tri, hi[g]) + (_dot(tri, mid[g]) + _dot(tri, lo[g])) for g in range(grp)], axis=0)
    lt = li[:, 0:1] if reverse else li[:, ch - 1:ch]
    rho = 0.5 * lt
    e1 = jnp.exp(li - rho)
    e2 = jnp.exp(rho - li)
    er = jnp.exp(rho)
    a_rel = nkk_ref[...] * (e1 * jnp.exp(-lw))
    r_rel = r_ref[...] * e1
    b_rel = b_ref[...] * e2
    k_rel = kd_ref[...] * e2
    pairs = lambda x: _split_pairs(x.astype(BF16))
    a_abs, r_abs, b_end, k_end = pairs(a_rel * er), pairs(r_rel * er), pairs(b_rel * er), pairs(k_rel * er)
    a_rel, r_rel, b_rel, k_rel = pairs(a_rel), pairs(r_rel), pairs(b_rel), pairs(k_rel)
    v = pairs(v_ref[...])
    decay = _split_pairs(jnp.exp(lt))
    lane = lax.broadcasted_iota(I32, (1, 1, LANES), 2)
    first = lane < A_HEAD_DIM

    def blockdiag(x):
        zero = jnp.zeros_like(x)
        return jnp.concatenate([jnp.where(first, x, zero), jnp.where(first, zero, x)], axis=1)

    ri2 = lax.broadcasted_iota(I32, (1, ch, 2 * ch), 1)
    ci2 = lax.broadcasted_iota(I32, (1, ch, 2 * ch), 2)
    ci2 = jnp.where(ci2 >= ch, ci2 - ch, ci2)
    if reverse:
        incl2, strict2 = ci2 >= ri2, ci2 > ri2
    else:
        incl2, strict2 = ci2 <= ri2, ci2 < ri2
    eye2 = jnp.where(ri2 == ci2, 1.0, 0.0)
    lhs = jnp.concatenate([a_rel, r_rel], axis=1)
    with_b = _bdot_nt(lhs, blockdiag(b_rel))
    with_k = _bdot_nt(lhs, blockdiag(k_rel))
    a_ab = jnp.where(strict2, with_b[:, :ch], 0.0)
    a_ak = jnp.where(strict2, with_k[:, :ch], 0.0).astype(BF16)
    a_r = jnp.concatenate([jnp.where(incl2, with_b[:, ch:], 0.0), jnp.where(incl2, with_k[:, ch:], 0.0)],
                          axis=2).astype(BF16)
    inv = eye2
    for k in range(int(math.log2(ch))):
        p, q = (ci2 >> k, ri2 >> k) if reverse else (ri2 >> k, ci2 >> k)
        joins = ((p ^ q) * 4 + (p - q)) == 5
        lk = jnp.where(joins, a_ab, 0.0)
        if k == 0:
            inv = inv + lk
        else:
            invb = inv.astype(BF16)
            inv = inv + _bdot(_bdot(invb, blockdiag(lk.astype(BF16))).astype(BF16), blockdiag(invb))
    s = s_scr[...]
    from_state = _bdot_nt(jnp.concatenate([a_abs, r_abs], axis=1), s.astype(BF16))
    v_bd = blockdiag(v)
    x = from_state[:, :ch] + _bdot(a_ak, v_bd)
    u = _bdot(inv.astype(BF16), blockdiag(x.astype(BF16))).astype(BF16)
    o = from_state[:, ch:] + _bdot(a_r, jnp.concatenate([blockdiag(u), v_bd], axis=1))
    si = lax.broadcasted_iota(I32, (1, LANES, LANES), 1) < A_HEAD_DIM
    same_head = si == (lax.broadcasted_iota(I32, (1, LANES, LANES), 2) < A_HEAD_DIM)
    upd = _bdot_tn(jnp.concatenate([u, v], axis=1), jnp.concatenate([b_end, k_end], axis=1))
    s_scr[...] = s * decay + jnp.where(same_head, upd, 0.0)
    for g in range(grp):
        o_ref[g] = jnp.concatenate([o[g * n_pairs + p] for p in range(n_pairs)], axis=1)

    @pl.when(c == pl.num_programs(1) - 1)
    def _():
        sf_ref[...] = s_scr[...].reshape(sf_ref.shape)


def _wkv_scan(r, v, nkk, kd, b, lw, s0, reverse):
    bsz, t, aw = r.shape
    ch = SCAN_CHUNK
    assert t % ch == 0 and 2 * ch == LANES and 2 * A_HEAD_DIM == LANES
    nc = t // ch
    n_pairs = aw // LANES
    grp = SCAN_GROUP if bsz % SCAN_GROUP == 0 else 1
    tmap = (lambda i, c: (i, nc - 1 - c, 0)) if reverse else (lambda i, c: (i, c, 0))
    seq = pl.BlockSpec((grp, ch, aw), tmap)
    st = pl.BlockSpec((grp, n_pairs, LANES, LANES), lambda i, c: (i, 0, 0, 0))
    o, s_fin = pl.pallas_call(
        functools.partial(_scan_kernel, reverse=reverse),
        grid=(bsz // grp, nc),
        in_specs=[seq] * 6 + [st],
        out_specs=[seq, st],
        out_shape=[jax.ShapeDtypeStruct((bsz, t, aw), F32),
                   jax.ShapeDtypeStruct((bsz, n_pairs, LANES, LANES), F32)],
        scratch_shapes=[pltpu.VMEM((grp * n_pairs, LANES, LANES), F32)],
        compiler_params=_cparams("arbitrary", "arbitrary"),
        name="wkv_scan_bwd" if reverse else "wkv_scan_fwd",
    )(r, v, nkk, kd, b, lw, _pair_states(s0))
    return o, _unpair_states(s_fin)


def _pair_states(s):
    b, h, d, _ = s.shape
    eye = jnp.eye(2, dtype=s.dtype)[None, None, :, None, :, None]
    return (s.reshape(b, h // 2, 2, d, 1, d) * eye).reshape(b, h // 2, 2 * d, 2 * d)


def _unpair_states(s):
    b, p, dd, _ = s.shape
    d = dd // 2
    s6 = s.reshape(b, p, 2, d, 2, d)
    return jnp.stack([s6[:, :, 0, :, 0, :], s6[:, :, 1, :, 1, :]], axis=2).reshape(b, 2 * p, d, d)


def _rope_lanes(x, cos_t, sin_up, sin_dn):
    w = x.shape[1]
    half = QK_ROPE // 2
    return x * cos_t + pltpu.roll(x, half, 1) * sin_up + pltpu.roll(x, w - half, 1) * sin_dn


def _mla_prep_kernel(*refs, rope):
    if rope:
        (m_ref, gq_ref, gkv_ref, wuq_ref, wk_ref, wv_ref, p_ref,
         qc_ref, qu_ref, qd_ref, kc_ref, ku_ref, kd_ref, q_o, k_o, v_o, ckv_o) = refs
    else:
        (m_ref, gq_ref, gkv_ref, wuq_ref, wk_ref, wv_ref, p_ref, q_o, k_o, v_o, ckv_o) = refs
    m = m_ref[...]
    qn = _rms(m[:, :Q_LORA], gq_ref[...])
    ckv = _rms(m[:, Q_LORA:Q_LORA + KV_LORA], gkv_ref[...])
    kp = m[:, Q_LORA + KV_LORA:]
    q = _dot(qn.astype(BF16), wuq_ref[...]) * ATTN_SCALE
    if rope:
        tile = lambda ref: jnp.concatenate([ref[...]] * B_HEADS, axis=1)
        q = _rope_lanes(q, tile(qc_ref), tile(qu_ref), tile(qd_ref))
        kp = _rope_lanes(kp, kc_ref[...], ku_ref[...], kd_ref[...])
    cb = ckv.astype(BF16)
    q_o[...] = q.astype(BF16)
    k_o[...] = (_dot(cb, wk_ref[...]) + _dot(kp.astype(BF16), p_ref[...])).astype(BF16)
    v_o[...] = _dot(cb, wv_ref[...]).astype(BF16)
    ckv_o[...] = ckv


def _mla_prep(m2, g_q, g_kv, wuq_p, wk_p, wv_p, place, rope_tabs, tb, t):
    n = m2.shape[0]
    hw = B_HEADS * HEAD_PAD
    full = lambda a: pl.BlockSpec(a.shape, lambda i: (0, 0))
    ins = [m2, g_q, g_kv, wuq_p, wk_p, wv_p, place]
    specs = [pl.BlockSpec((tb, M_COLS), lambda i: (i, 0))] + [full(a) for a in ins[1:]]
    if rope_tabs is not None:
        nbt = t // tb
        ins += list(rope_tabs)
        specs += [pl.BlockSpec((tb, HEAD_PAD), lambda i: (i % nbt, 0))] * 6
    big = pl.BlockSpec((tb, hw), lambda i: (i, 0))
    return pl.pallas_call(
        functools.partial(_mla_prep_kernel, rope=rope_tabs is not None),
        grid=(n // tb,),
        in_specs=specs,
        out_specs=[big, big, big, pl.BlockSpec((tb, KV_LORA), lambda i: (i, 0))],
        out_shape=[jax.ShapeDtypeStruct((n, hw), BF16)] * 3 + [jax.ShapeDtypeStruct((n, KV_LORA), F32)],
        compiler_params=_cparams("arbitrary"),
        name="mla_prep",
    )(*ins)


def _kv_up_kernel(ckv_ref, kp_ref, wk_ref, wv_ref, p_ref, k_o, v_o):
    cb = ckv_ref[...].astype(BF16)
    k_o[...] = (_dot(cb, wk_ref[...]) + _dot(kp_ref[...].astype(BF16), p_ref[...])).astype(BF16)
    v_o[...] = _dot(cb, wv_ref[...]).astype(BF16)


def _kv_up(ckv2, kpe_pad, wk_p, wv_p, place, tb):
    n = ckv2.shape[0]
    hw = B_HEADS * HEAD_PAD
    full = lambda a: pl.BlockSpec(a.shape, lambda i: (0, 0))
    big = pl.BlockSpec((tb, hw), lambda i: (i, 0))
    return pl.pallas_call(
        _kv_up_kernel,
        grid=(n // tb,),
        in_specs=[pl.BlockSpec((tb, KV_LORA), lambda i: (i, 0)),
                  pl.BlockSpec((tb, HEAD_PAD), lambda i: (i, 0)),
                  full(wk_p), full(wv_p), full(place)],
        out_specs=[big, big],
        out_shape=[jax.ShapeDtypeStruct((n, hw), BF16)] * 2,
        compiler_params=_cparams("arbitrary"),
        name="kv_up",
    )(ckv2, kpe_pad, wk_p, wv_p, place)


def _attn_kernel(*refs, two):
    if two:
        q_ref, k1_ref, v1_ref, k2_ref, v2_ref, o_ref = refs
    else:
        q_ref, k1_ref, v1_ref, o_ref = refs
    for h in range(B_HEADS):
        hs = slice(h * HEAD_PAD, (h + 1) * HEAD_PAD)
        q = q_ref[0, :, hs]
        s1 = _dot_nt(q, k1_ref[0, :, hs])
        mx = jnp.max(s1, axis=-1, keepdims=True)
        if two:
            s2 = _dot_nt(q, k2_ref[0, :, hs])
            mx = jnp.maximum(mx, jnp.max(s2, axis=-1, keepdims=True))
        p1 = jnp.exp(s1 - mx)
        den = jnp.sum(p1, axis=-1, keepdims=True)
        acc = _dot(p1.astype(BF16), v1_ref[0, :, hs])
        if two:
            p2 = jnp.exp(s2 - mx)
            den = den + jnp.sum(p2, axis=-1, keepdims=True)
            acc = acc + _dot(p2.astype(BF16), v2_ref[0, :, hs])
        o_ref[0, :, hs] = (acc / den).astype(BF16)


def _attention(q3, k1, v1, k2, v2, tq):
    b, t, hw = q3.shape
    two = k2 is not None
    qspec = pl.BlockSpec((1, tq, hw), lambda i, j: (i, j, 0))
    kv = lambda a: pl.BlockSpec((1, a.shape[1], hw), lambda i, j: (i, 0, 0))
    ins = [q3, k1, v1] + ([k2, v2] if two else [])
    return pl.pallas_call(
        functools.partial(_attn_kernel, two=two),
        grid=(b, t // tq),
        in_specs=[qspec] + [kv(a) for a in ins[1:]],
        out_specs=qspec,
        out_shape=jax.ShapeDtypeStruct((b, t, hw), BF16),
        compiler_params=_cparams("arbitrary", "arbitrary"),
        name="attention",
    )(*ins)


def _post_mix_kernel(of_ref, ob_ref, bonus_ref, g_ref, att_ref, gates_ref, x_ref,
                     g1_ref, sc2_ref, sh2_ref, lnw_ref, lnb_ref, bd_ref, wa_ref, wb_ref, wo_ref,
                     gpost_ref, gpre_ref, wr_ref, x1_o, h2_o, aff_o):
    bd = bd_ref[...]
    inv_n = 1.0 / A_HEAD_DIM
    o = of_ref[...] + ob_ref[...]
    mu = _dot_x2(o, bd) * inv_n
    oc = o - mu
    var = _dot_x2(oc * oc, bd) * inv_n
    on = (oc * lax.rsqrt(var + GN_EPS)) * lnw_ref[...] + lnb_ref[...]
    ya = _dot(((on + bonus_ref[...]) * g_ref[...]).astype(BF16), wa_ref[...])
    yb = _dot(att_ref[...], wb_ref[...])
    d = ya.shape[1]
    gates = gates_ref[...]
    mix = jax.nn.sigmoid(gates[:, :d]) * ya + jax.nn.sigmoid(gates[:, d:]) * yb
    y = _dot(mix.astype(BF16), wo_ref[...])
    x1 = x_ref[...] + g1_ref[0] * _rms(y, gpost_ref[...])
    x1_o[...] = x1
    h2 = _rms(x1, gpre_ref[...]) * (1.0 + sc2_ref[0]) + sh2_ref[0]
    h2_o[...] = h2.astype(BF16)
    hh, hl = _split2(h2)
    wh, wl = _split2(wr_ref[...])
    logits = _dot_nt(wh, hh) + (_dot_nt(wh, hl) + _dot_nt(wl, hh))
    z = jnp.exp(logits - jnp.max(logits, axis=0, keepdims=True))
    aff_o[...] = z / jnp.sum(z, axis=0, keepdims=True)


def _post_mix(o_f, o_b, bonus, g, att, gates, x2, g1, sc2, sh2, ln_w, ln_b, bd, wa, wb, wo,
              g_post, g_pre, w_router_t, tb, blocks_per_batch):
    n, d = x2.shape
    tok = lambda a: pl.BlockSpec((tb, a.shape[1]), lambda i: (i, 0))
    full = lambda a: pl.BlockSpec(a.shape, lambda i: (0, 0))
    ms = lambda a: _mod_spec(a, blocks_per_batch)
    return pl.pallas_call(
        _post_mix_kernel,
        grid=(n // tb,),
        in_specs=[tok(o_f), tok(o_b), tok(bonus), tok(g), tok(att), tok(gates), tok(x2),
                  ms(g1), ms(sc2), ms(sh2), full(ln_w), full(ln_b), full(bd), full(wa), full(wb),
                  full(wo), full(g_post), full(g_pre), full(w_router_t)],
        out_specs=[pl.BlockSpec((tb, d), lambda i: (i, 0)),
                   pl.BlockSpec((tb, d), lambda i: (i, 0)),
                   pl.BlockSpec((N_EXPERTS, tb), lambda i: (0, i))],
        out_shape=[jax.ShapeDtypeStruct((n, d), F32),
                   jax.ShapeDtypeStruct((n, d), BF16),
                   jax.ShapeDtypeStruct((N_EXPERTS, n), F32)],
        compiler_params=_cparams("arbitrary"),
        name="post_mix",
    )(o_f, o_b, bonus, g, att, gates, x2, g1, sc2, sh2, ln_w, ln_b, bd, wa, wb, wo,
      g_post, g_pre, w_router_t)


def _select_kernel(a_ref, pos_ref, sel_ref, *, cap, n_chunks):
    a = a_ref[...]
    rows = a.shape[0]
    a3 = a.reshape(N_EXPERTS, n_chunks, LANES)

    def count(mask):
        c = jnp.sum(jnp.where(mask, 1.0, 0.0), axis=2, keepdims=True)
        return jnp.sum(c, axis=1, keepdims=True)

    def body(i, thr):
        cand = thr | jnp.left_shift(jnp.int32(1), 30 - i)
        return jnp.where(count(a3 >= pltpu.bitcast(cand, F32)) >= cap, cand, thr)

    thr = pltpu.bitcast(lax.fori_loop(0, 31, body, jnp.zeros((N_EXPERTS, 1, LANES), I32)), F32)
    gt = a3 > thr
    eq = a3 == thr
    need = cap - count(gt)

    li = lax.broadcasted_iota(I32, (LANES, LANES), 0)
    lj = lax.broadcasted_iota(I32, (LANES, LANES), 1)
    lane_before = jnp.where(li < lj, 1.0, 0.0).astype(BF16)
    lane_all = jnp.ones((LANES, LANES), BF16)
    ci = lax.broadcasted_iota(I32, (n_chunks, n_chunks), 0)
    cj = lax.broadcasted_iota(I32, (n_chunks, n_chunks), 1)
    chunk_before = jnp.where(cj < ci, 1.0, 0.0).astype(BF16)

    def prefix(flags3):
        f2 = flags3.reshape(rows, LANES).astype(BF16)
        within = _dot(f2, lane_before).reshape(N_EXPERTS, n_chunks, LANES)
        tot = _dot(f2, lane_all).astype(BF16).reshape(N_EXPERTS, n_chunks, LANES)
        offs = [_dot(chunk_before, tot[e]) for e in range(N_EXPERTS)]
        return within + jnp.stack(offs, axis=0)

    eqf = jnp.where(eq, 1.0, 0.0)
    tie_ok = jnp.where(prefix(eqf) < need, eqf, 0.0)
    sel = jnp.where(gt, 1.0, tie_ok)
    pos_ref[...] = prefix(sel).reshape(rows, LANES).astype(I32)
    sel_ref[...] = sel.reshape(rows, LANES).astype(I32)


def _select(aff_t, cap):
    e, n = aff_t.shape
    n_chunks = n // LANES
    rows = e * n_chunks
    a2 = aff_t.reshape(rows, LANES)
    spec = pl.BlockSpec((rows, LANES), lambda i: (0, 0))
    pos, sel = pl.pallas_call(
        functools.partial(_select_kernel, cap=cap, n_chunks=n_chunks),
        grid=(1,),
        in_specs=[spec],
        out_specs=[spec, spec],
        out_shape=[jax.ShapeDtypeStruct((rows, LANES), I32)] * 2,
        compiler_params=_cparams("arbitrary"),
        name="expert_select",
    )(a2)
    return pos.reshape(e, n), sel.reshape(e, n)


ITEM_VALID, ITEM_FIRST, ITEM_LAST = 1, 2, 4


def _ffn_kernel(blk_ref, tile_ref, flag_ref, slot_ref, aff_ref, h_ref, wg_ref, wu_ref, wd_ref, y_ref,
                acc_ref, wacc_ref, *, steps):
    e = pl.program_id(0)
    s = pl.program_id(1)
    flags = flag_ref[e * steps + s]
    tile = tile_ref[e * steps + s]
    rows = acc_ref.shape[0]

    @pl.when((flags & ITEM_FIRST) != 0)
    def _():
        acc_ref[...] = jnp.zeros_like(acc_ref)
        wacc_ref[...] = jnp.zeros_like(wacc_ref)

    @pl.when((flags & ITEM_VALID) != 0)
    def _():
        slot = slot_ref[0] - tile * rows
        hit = lax.broadcasted_iota(I32, (rows, slot.shape[1]), 0) == slot
        acc_ref[...] += _dot(jnp.where(hit, 1.0, 0.0).astype(BF16), h_ref[...])
        wacc_ref[...] += jnp.sum(jnp.where(hit, aff_ref[pl.ds(e, 1), :], 0.0), axis=1, keepdims=True)

    @pl.when((flags & ITEM_LAST) != 0)
    def _():
        xe = acc_ref[...].astype(BF16)
        gate = _dot(xe, wg_ref[0])
        up = _dot(xe, wu_ref[0])
        hid = (gate * jax.nn.sigmoid(gate)) * up
        y_ref[0] = (_dot(hid.astype(BF16), wd_ref[0]) * wacc_ref[...]).astype(BF16)


def _expert_ffn(items, slot3, aff_t, h2, wg, wu, wd, cap, tb, rt):
    blk, tile, flags, steps = items
    n, d = h2.shape
    nb = n // tb
    e, _, f = wg.shape
    ix = lambda ei, si: ei * steps + si
    grid_spec = pltpu.PrefetchScalarGridSpec(
        num_scalar_prefetch=3,
        grid=(e, steps),
        in_specs=[pl.BlockSpec((1, 1, tb), lambda ei, si, b, t, fl: (ei * nb + b[ix(ei, si)], 0, 0)),
                  pl.BlockSpec((e, tb), lambda ei, si, b, t, fl: (0, b[ix(ei, si)])),
                  pl.BlockSpec((tb, d), lambda ei, si, b, t, fl: (b[ix(ei, si)], 0)),
                  pl.BlockSpec((1, d, f), lambda ei, si, b, t, fl: (ei, 0, 0)),
                  pl.BlockSpec((1, d, f), lambda ei, si, b, t, fl: (ei, 0, 0)),
                  pl.BlockSpec((1, f, d), lambda ei, si, b, t, fl: (ei, 0, 0))],
        out_specs=pl.BlockSpec((1, rt, d), lambda ei, si, b, t, fl: (ei, t[ix(ei, si)], 0)),
        scratch_shapes=[pltpu.VMEM((rt, d), F32), pltpu.VMEM((rt, 1), F32)],
    )
    return pl.pallas_call(
        functools.partial(_ffn_kernel, steps=steps),
        grid_spec=grid_spec,
        out_shape=jax.ShapeDtypeStruct((e, cap, d), BF16),
        compiler_params=_cparams("arbitrary", "arbitrary"),
        name="expert_ffn",
    )(blk, tile, flags, slot3, aff_t, h2, wg, wu, wd)


def _tile_ranges(starts, cap, rt):
    lo, hi = starts[:, :-1], starts[:, 1:]
    cnt = hi - lo
    n_tiles = cap // rt
    t_lo = jnp.minimum(lo // rt, n_tiles - 1)
    t_hi = jnp.where(cnt > 0, (hi - 1) // rt, t_lo)
    return t_lo, jnp.where(cnt > 0, t_hi - t_lo + 1, 0)


def _ffn_items(starts, cap, rt):
    e, nb1 = starts.shape
    nb = nb1 - 1
    steps = nb + cap // rt
    t_lo, nt = _tile_ranges(starts, cap, rt)

    def one(t_lo_e, nt_e):
        offs = jnp.cumsum(nt_e) - nt_e
        total = jnp.sum(nt_e)
        s = jnp.arange(steps, dtype=I32)
        sc = jnp.minimum(s, total - 1)
        b = (jnp.sum((offs[None, :] <= sc[:, None]).astype(I32), axis=1) - 1).astype(I32)
        tile = t_lo_e[b] + (sc - offs[b])
        valid = s < total
        prev_t = jnp.concatenate([jnp.full((1,), -1, I32), tile[:-1]])
        next_t = jnp.concatenate([tile[1:], jnp.full((1,), -1, I32)])
        first = valid & (tile != prev_t)
        last = valid & ((tile != next_t) | (s == total - 1))
        fl = valid * ITEM_VALID + first * ITEM_FIRST + last * ITEM_LAST
        return b, tile.astype(I32), fl.astype(I32)

    b, tile, fl = jax.vmap(one)(t_lo.astype(I32), nt.astype(I32))
    return b.reshape(-1), tile.reshape(-1), fl.reshape(-1), steps


def _combine_kernel(t0_ref, fetch_ref, slot_ref, x_ref, g2_ref, gp_ref, *rest, n_win):
    y_refs, o_ref = rest[:-1], rest[-1]
    blk = pl.program_id(0)
    tb = slot_ref.shape[0]
    rows = y_refs[0].shape[1]
    lane = lax.broadcasted_iota(I32, (tb, n_win * rows), 1)
    acc = None
    for e in range(N_EXPERTS):
        slot = slot_ref[:, e:e + 1] - t0_ref[blk * N_EXPERTS + e] * rows
        onehot = jnp.where(lane == slot, 1.0, 0.0).astype(BF16)
        for k in range(n_win):
            part = _dot(onehot[:, k * rows:(k + 1) * rows], y_refs[e * n_win + k][0])
            acc = part if acc is None else acc + part
    o_ref[...] = x_ref[...] + g2_ref[0] * _rms(acc, gp_ref[...])


def _combine_windows(pos, tb, rt, cap):
    n_tiles = cap // rt
    lo = pos[:, ::tb]
    hi = jnp.concatenate([lo[:, 1:], jnp.full((lo.shape[0], 1), cap, I32)], axis=1)
    t0 = jnp.minimum(lo // rt, n_tiles - 1)
    t_last = jnp.where(hi > lo, (hi - 1) // rt, t0)
    fetch = [t0]
    for k in range(1, tb // rt + 1):
        fetch.append(lax.cummax(jnp.where(t_last >= t0 + k, t0 + k, 0), axis=1))
    flat = lambda a: a.T.reshape(-1).astype(I32)
    return flat(t0), jnp.concatenate([flat(f) for f in fetch])


def _combine(windows, slot_t, ye, x1, g2, g_post, tb, rt, blocks_per_batch):
    t0, fetch = windows
    n, d = x1.shape
    e, cap, _ = ye.shape
    nb = n // tb
    n_win = tb // rt + 1
    if g2.shape[0] == 1:
        g2_spec = pl.BlockSpec((1, 1, d), lambda b, t, f: (0, 0, 0))
    else:
        g2_spec = pl.BlockSpec((1, 1, d), lambda b, t, f: (b // blocks_per_batch, 0, 0))

    def window(ei, k):
        return pl.BlockSpec((1, rt, d), lambda b, t, f: (ei, f[k * nb * e + b * e + ei], 0))

    grid_spec = pltpu.PrefetchScalarGridSpec(
        num_scalar_prefetch=2,
        grid=(nb,),
        in_specs=[pl.BlockSpec((tb, e), lambda b, t, f: (b, 0)),
                  pl.BlockSpec((tb, d), lambda b, t, f: (b, 0)),
                  g2_spec,
                  pl.BlockSpec((1, d), lambda b, t, f: (0, 0))]
                 + [window(ei, k) for ei in range(e) for k in range(n_win)],
        out_specs=pl.BlockSpec((tb, d), lambda b, t, f: (b, 0)),
    )
    return pl.pallas_call(
        functools.partial(_combine_kernel, n_win=n_win),
        grid_spec=grid_spec,
        out_shape=jax.ShapeDtypeStruct((n, d), F32),
        compiler_params=_cparams("arbitrary"),
        name="moe_combine",
    )(t0, fetch, slot_t, x1, g2, g_post, *([ye] * (e * n_win)))


def _rope_tables(t):
    half = QK_ROPE // 2
    pos = jnp.arange(t)
    row = (pos // GRID_W).astype(F32)
    col = (pos % GRID_W).astype(F32)
    inv = ROPE_BASE ** (-jnp.arange(0, half, 2, dtype=F32) / half)
    ang = jnp.concatenate([row[:, None] * inv, col[:, None] * inv], axis=-1)
    cos, sin = jnp.cos(ang), jnp.sin(ang)

    def tabs(first_lane):
        c = jnp.ones((t, HEAD_PAD), F32)
        c = c.at[:, first_lane:first_lane + half].set(cos).at[:, first_lane + half:first_lane + 2 * half].set(cos)
        up = jnp.zeros((t, HEAD_PAD), F32).at[:, first_lane + half:first_lane + 2 * half].set(sin)
        dn = jnp.zeros((t, HEAD_PAD), F32).at[:, first_lane:first_lane + half].set(-sin)
        return c, up, dn

    return tabs(QK_NOPE + QK_ROPE) + tabs(QK_ROPE)


def _layout_weights(w_in, w_uq, w_ukv, w_branch_b, w_up, a_up):
    d = w_in.shape[0]
    deint = jnp.concatenate([jnp.arange(0, QK_ROPE, 2), jnp.arange(1, QK_ROPE, 2)])
    kpe0 = RWKV_COLS + Q_LORA + KV_LORA
    kpe_cols = w_in[:, kpe0:kpe0 + QK_ROPE]
    w_in_p = jnp.concatenate(
        [w_in[:, :kpe0 + QK_ROPE], kpe_cols[:, deint],
         jnp.zeros((d, M_COLS - MLA_COLS - QK_ROPE), F32), w_in[:, RWKV_COLS + MLA_COLS:]], axis=1)
    uq = w_uq.reshape(Q_LORA, B_HEADS, QK_NOPE + QK_ROPE)
    wuq_p = jnp.concatenate([uq, uq[:, :, QK_NOPE:][:, :, deint]], axis=2).reshape(Q_LORA, B_HEADS * HEAD_PAD)
    ukv = w_ukv.reshape(KV_LORA, B_HEADS, QK_NOPE + V_HEAD)
    zpad = jnp.zeros((KV_LORA, B_HEADS, HEAD_PAD - QK_NOPE), F32)
    wk_p = jnp.concatenate([ukv[:, :, :QK_NOPE], zpad], axis=2).reshape(KV_LORA, B_HEADS * HEAD_PAD)
    wv_p = jnp.concatenate([ukv[:, :, QK_NOPE:], zpad], axis=2).reshape(KV_LORA, B_HEADS * HEAD_PAD)
    wb = w_branch_b.reshape(B_HEADS, V_HEAD, d)
    wb_p = jnp.concatenate([wb, jnp.zeros((B_HEADS, HEAD_PAD - V_HEAD, d), F32)], axis=1).reshape(B_HEADS * HEAD_PAD, d)
    eye = jnp.eye(QK_ROPE, dtype=F32)
    z = jnp.zeros((QK_ROPE, QK_ROPE), F32)
    head_raw = jnp.concatenate([jnp.zeros((QK_ROPE, QK_NOPE), F32), eye, z], axis=1)
    head_rot = jnp.concatenate([jnp.zeros((QK_ROPE, QK_NOPE), F32), z, eye], axis=1)
    zrows = jnp.zeros((HEAD_PAD - 2 * QK_ROPE, B_HEADS * HEAD_PAD), F32)
    zr = jnp.zeros((QK_ROPE, B_HEADS * HEAD_PAD), F32)
    place_raw = jnp.concatenate([jnp.tile(head_raw, (1, B_HEADS)), zr, zrows], axis=0)
    place_rot = jnp.concatenate([zr, jnp.tile(head_rot, (1, B_HEADS)), zrows], axis=0)
    zl = jnp.zeros((DECAY_LORA, A_WIDTH), F32)
    wup_p = jnp.stack([jnp.concatenate([w_up[0], zl]), jnp.concatenate([zl, w_up[1]])])
    aup_p = jnp.stack([jnp.concatenate([a_up[0], zl]), jnp.concatenate([zl, a_up[1]])])
    bf = lambda a: a.astype(BF16)
    return (bf(w_in_p), bf(wuq_p), bf(wk_p), bf(wv_p), bf(wb_p), bf(place_raw), bf(place_rot),
            bf(wup_p), bf(aup_p))


def _group_layer(x, mods, lw, s0_f, s0_b, ctx_kv, rope_tabs):
    bsz, t, d = x.shape
    n = bsz * t
    tb = min(TOKEN_BLOCK, t)
    bpb = t // tb
    sh1, sc1, g1, sh2, sc2, g2 = mods
    x2 = x.reshape(n, d)
    u, m, gates = _inproj(x2, sc1, sh1, lw["g_pre_mix"], lw["w_in_p"], tb, bpb)
    prep = _rwkv_prep(u.reshape(bsz, t, RWKV_COLS), lw["shift_w"], lw["w0"], lw["a0"], lw["wup_p"],
                      lw["aup_p"], lw["g_up"], lw["k_k"], lw["k_a"], lw["r_k"], lw["bd"], tb)
    r, v, nkk, kd_f, kd_b, b_f, b_b, lw_f, lw_b, g, bonus = prep
    o_f, s_f = _wkv_scan(r, v, nkk, kd_f, b_f, lw_f, s0_f, reverse=False)
    o_b, s_b = _wkv_scan(r, v, nkk, kd_b, b_b, lw_b, s0_b, reverse=True)
    place = lw["place_raw"] if rope_tabs is None else lw["place_rot"]
    q, k, vv, ckv = _mla_prep(m, lw["g_qnorm"], lw["g_kvnorm"], lw["wuq_p"], lw["wk_p"], lw["wv_p"],
                              place, rope_tabs, tb, t)
    hw = B_HEADS * HEAD_PAD
    k2, v2 = ctx_kv if ctx_kv is not None else (None, None)
    att = _attention(q.reshape(bsz, t, hw), k.reshape(bsz, t, hw), vv.reshape(bsz, t, hw), k2, v2, tb)
    two = lambda a: a.reshape(n, a.shape[-1])
    x1, h2, aff_t = _post_mix(two(o_f), two(o_b), two(bonus), two(g), att.reshape(n, hw), gates, x2,
                              g1, sc2, sh2, lw["ln_x_w"], lw["ln_x_b"], lw["bd"], lw["w_branch_a"],
                              lw["wb_p"], lw["w_out"], lw["g_post_mix"], lw["g_pre_ffn"],
                              lw["w_router_t"], tb, bpb)
    cap = CAPACITY_FACTOR * n // N_EXPERTS
    rt = min(EXPERT_TILE, cap)
    pos, sel = _select(aff_t, cap)
    slot = jnp.where(sel > 0, pos, -1)
    td = min(DISPATCH_BLOCK, n)
    starts = jnp.concatenate([pos[:, ::td], jnp.full((N_EXPERTS, 1), cap, I32)], axis=1)
    ye = _expert_ffn(_ffn_items(starts, cap, rt), slot.reshape(N_EXPERTS * (n // td), 1, td), aff_t, h2,
                     lw["w_exp_gate"], lw["w_exp_up"], lw["w_exp_down"], cap, td, rt)
    out = _combine(_combine_windows(pos, tb, rt, cap), slot.T, ye, x1, g2, lw["g_post_ffn"], tb, rt, bpb)
    kpe = m[:, Q_LORA + KV_LORA:Q_LORA + KV_LORA + QK_ROPE]
    return out.reshape(bsz, t, d), (ckv.reshape(bsz, t, KV_LORA), kpe.reshape(bsz, t, QK_ROPE), s_f, s_b)


def kernel(x_prompt, x_sample, cache_ckv, cache_kpe, state_wkv_fwd, state_wkv_bwd, c, c_ctx,
           w_mod, b_mod, g_pre_mix, g_post_mix, g_pre_ffn, g_post_ffn, w_in, shift_w,
           w0, w_up, a0, a_up, g_up, k_k, k_a, r_k, ln_x_w, ln_x_b, w_branch_a,
           g_qnorm, w_uq, g_kvnorm, w_ukv, w_branch_b, w_out,
           w_router, w_exp_gate, w_exp_up, w_exp_down):
    depth = w_mod.shape[0]
    d = x_prompt.shape[-1]
    dec_b, dec_t = x_sample.shape[0], x_sample.shape[1]
    xp, xs = x_prompt, x_sample
    c_rows = jnp.concatenate([c, c_ctx[None, :],
                              jnp.zeros((-(dec_b + 1) % 8, d), F32)], axis=0)
    rope_tabs = _rope_tables(dec_t)
    ii = lax.broadcasted_iota(I32, (A_WIDTH, A_WIDTH), 0) // A_HEAD_DIM
    jj = lax.broadcasted_iota(I32, (A_WIDTH, A_WIDTH), 1) // A_HEAD_DIM
    bd = (ii == jj).astype(BF16)
    row = lambda a: a.reshape(1, -1)
    bf = lambda a: a.astype(BF16)
    ckv_l, kpe_l, sf_l, sb_l = [], [], [], []
    for l in range(depth):
        (w_in_p, wuq_p, wk_p, wv_p, wb_p, place_raw, place_rot, wup_p, aup_p) = _layout_weights(
            w_in[l], w_uq[l], w_ukv[l], w_branch_b[l], w_up[l], a_up[l])
        lw = {
            "g_pre_mix": row(g_pre_mix[l]), "g_post_mix": row(g_post_mix[l]),
            "g_pre_ffn": row(g_pre_ffn[l]), "g_post_ffn": row(g_post_ffn[l]),
            "w_in_p": w_in_p, "shift_w": shift_w[l], "w0": w0[l], "a0": a0[l],
            "wup_p": wup_p, "aup_p": aup_p, "g_up": bf(g_up[l]),
            "k_k": row(k_k[l]), "k_a": row(k_a[l]), "r_k": row(r_k[l]), "bd": bd,
            "ln_x_w": row(ln_x_w[l]), "ln_x_b": row(ln_x_b[l]), "w_branch_a": bf(w_branch_a[l]),
            "g_qnorm": row(g_qnorm[l]), "g_kvnorm": row(g_kvnorm[l]),
            "wuq_p": wuq_p, "wk_p": wk_p, "wv_p": wv_p, "wb_p": wb_p,
            "place_raw": place_raw, "place_rot": place_rot,
            "w_out": bf(w_out[l]), "w_router_t": w_router[l].T,
            "w_exp_gate": bf(w_exp_gate[l]), "w_exp_up": bf(w_exp_up[l]), "w_exp_down": bf(w_exp_down[l]),
        }
        mod = _modulation(c_rows, w_mod[l], b_mod[l])
        mods_lat = [mod[:dec_b, i * d:(i + 1) * d].reshape(dec_b, 1, d) for i in range(6)]
        mods_ctx = [mod[dec_b:dec_b + 1, i * d:(i + 1) * d].reshape(1, 1, d) for i in range(6)]
        zeros_state = jnp.zeros((xp.shape[0], A_HEADS, A_HEAD_DIM, A_HEAD_DIM), F32)
        xp, (ckv, kpe, s_f, s_b) = _group_layer(xp, mods_ctx, lw, zeros_state, zeros_state, None, None)
        ckv_l.append(ckv)
        kpe_l.append(kpe)
        sf_l.append(s_f)
        sb_l.append(s_b)
        past = cache_ckv.shape[2]
        kpe_pad = jnp.concatenate(
            [cache_kpe[:, l], jnp.zeros((dec_b, past, HEAD_PAD - QK_ROPE), F32)], axis=-1)
        k_ctx, v_ctx = _kv_up(cache_ckv[:, l].reshape(dec_b * past, KV_LORA),
                              kpe_pad.reshape(dec_b * past, HEAD_PAD), wk_p, wv_p, place_raw,
                              min(TOKEN_BLOCK, past))
        hw = B_HEADS * HEAD_PAD
        ctx_kv = (k_ctx.reshape(dec_b, past, hw), v_ctx.reshape(dec_b, past, hw))
        xs, _ = _group_layer(xs, mods_lat, lw, state_wkv_fwd[:, l], state_wkv_bwd[:, l], ctx_kv, rope_tabs)
    return (xp, xs, jnp.stack(ckv_l, axis=1), jnp.stack(kpe_l, axis=1),
            jnp.stack(sf_l, axis=1), jnp.stack(sb_l, axis=1))
```

```python
import functools
import math

import jax
import jax.numpy as jnp
from jax import lax
from jax.experimental import pallas as pl
from jax.experimental.pallas import tpu as pltpu

F32 = jnp.float32
BF16 = jnp.bfloat16
I32 = jnp.int32

GRID_W = 64
A_HEADS = 8
A_HEAD_DIM = 64
A_WIDTH = A_HEADS * A_HEAD_DIM
DECAY_LORA = 64
ICLR_LORA = 64
GATE_LORA = 128
DECAY_SCALE = 0.6065306597126334
GN_EPS = 64e-5
B_HEADS = 8
Q_LORA = 256
KV_LORA = 128
QK_NOPE = 64
QK_ROPE = 32
V_HEAD = 64
ROPE_BASE = 10000.0
ATTN_SCALE = 1.0 / math.sqrt(QK_NOPE + QK_ROPE)
N_EXPERTS = 16
CAPACITY_FACTOR = 2
EPS = 1e-6
RWKV_COLS = 3 * A_WIDTH + 2 * DECAY_LORA + 2 * ICLR_LORA + GATE_LORA
MLA_COLS = Q_LORA + KV_LORA + QK_ROPE

LANES = 128
HEAD_PAD = 128
M_COLS = 512
VMEM_LIMIT = 56 * 1024 * 1024

SCAN_CHUNK = 64
SCAN_GROUP = 8
TOKEN_BLOCK = 256
EXPERT_TILE = 256
DISPATCH_BLOCK = 1024


def _cparams(*sem):
    return pltpu.CompilerParams(dimension_semantics=sem, vmem_limit_bytes=VMEM_LIMIT)


def _dot(a, b):
    return jnp.dot(a, b, preferred_element_type=F32)


def _dot_nt(a, b):
    return lax.dot_general(a, b, (((1,), (1,)), ((), ())), preferred_element_type=F32)


def _dot_tn(a, b):
    return lax.dot_general(a, b, (((0,), (0,)), ((), ())), preferred_element_type=F32)


def _split2(x):
    hi = x.astype(BF16)
    lo = (x - hi.astype(F32)).astype(BF16)
    return hi, lo


def _dot_x2(a, b_bf16):
    hi, lo = _split2(a)
    return _dot(hi, b_bf16) + _dot(lo, b_bf16)


def _dot_f32(a, b):
    ah, al = _split2(a)
    bh, bl = _split2(b)
    return _dot(ah, bh) + (_dot(ah, bl) + _dot(al, bh))


def _rms(x, g):
    return (x * lax.rsqrt(jnp.mean(x * x, axis=-1, keepdims=True) + EPS)) * g


def _mod_kernel(c_ref, w_ref, b_ref, o_ref):
    c = c_ref[...]
    s = c * jax.nn.sigmoid(c)
    o_ref[...] = _dot_f32(s, w_ref[...]) + b_ref[...]


def _modulation(c_rows, w_mod, b_mod):
    rows, d = c_rows.shape
    n_out = w_mod.shape[1]
    tn = n_out // 8
    return pl.pallas_call(
        _mod_kernel,
        grid=(n_out // tn,),
        in_specs=[pl.BlockSpec((rows, d), lambda j: (0, 0)),
                  pl.BlockSpec((d, tn), lambda j: (0, j)),
                  pl.BlockSpec((1, tn), lambda j: (0, j))],
        out_specs=pl.BlockSpec((rows, tn), lambda j: (0, j)),
        out_shape=jax.ShapeDtypeStruct((rows, n_out), F32),
        compiler_params=_cparams("arbitrary"),
        name="modulation",
    )(c_rows, w_mod, b_mod.reshape(1, n_out))


def _inproj_kernel(x_ref, sc_ref, sh_ref, g_ref, w_ref, u_ref, m_ref, gt_ref):
    h = _rms(x_ref[...], g_ref[...]) * (1.0 + sc_ref[0]) + sh_ref[0]
    hb = h.astype(BF16)
    u_ref[...] = _dot(hb, w_ref[:, :RWKV_COLS])
    m_ref[...] = _dot(hb, w_ref[:, RWKV_COLS:RWKV_COLS + M_COLS])
    gt_ref[...] = _dot(hb, w_ref[:, RWKV_COLS + M_COLS:]).astype(gt_ref.dtype)


def _mod_spec(mod, blocks_per_batch):
    d = mod.shape[-1]
    if mod.shape[0] == 1:
        return pl.BlockSpec((1, 1, d), lambda i: (0, 0, 0))
    return pl.BlockSpec((1, 1, d), lambda i: (i // blocks_per_batch, 0, 0))


def _inproj(x2, sc, sh, g, w_in_p, tb, blocks_per_batch):
    n, d = x2.shape
    cols = w_in_p.shape[1]
    gate_cols = cols - RWKV_COLS - M_COLS
    return pl.pallas_call(
        _inproj_kernel,
        grid=(n // tb,),
        in_specs=[pl.BlockSpec((tb, d), lambda i: (i, 0)),
                  _mod_spec(sc, blocks_per_batch), _mod_spec(sh, blocks_per_batch),
                  pl.BlockSpec((1, d), lambda i: (0, 0)),
                  pl.BlockSpec((d, cols), lambda i: (0, 0))],
        out_specs=[pl.BlockSpec((tb, RWKV_COLS), lambda i: (i, 0)),
                   pl.BlockSpec((tb, M_COLS), lambda i: (i, 0)),
                   pl.BlockSpec((tb, gate_cols), lambda i: (i, 0))],
        out_shape=[jax.ShapeDtypeStruct((n, RWKV_COLS), F32),
                   jax.ShapeDtypeStruct((n, M_COLS), F32),
                   jax.ShapeDtypeStruct((n, gate_cols), BF16)],
        compiler_params=_cparams("arbitrary"),
        name="inproj",
    )(x2, sc, sh, g, w_in_p)


def _rwkv_prep_kernel(u_ref, hp_ref, hn_ref, sw_ref, w0_ref, a0_ref, wup_ref, aup_ref, gup_ref,
                      kk_ref, ka_ref, rk_ref, bd_ref,
                      r_o, v_o, nkk_o, kdf_o, kdb_o, bf_o, bb_o, lwf_o, lwb_o, g_o, bonus_o):
    u = u_ref[0]
    tb = u.shape[0]
    row = lax.broadcasted_iota(I32, u.shape, 0)
    prev = jnp.where(row == 0, hp_ref[0, 0], pltpu.roll(u, 1, 0))
    nxt = jnp.where(row == tb - 1, hn_ref[0, 0], pltpu.roll(u, tb - 1, 0))
    xs = sw_ref[0:1] * prev + sw_ref[1:2] * u + sw_ref[2:3] * nxt
    aw = A_WIDTH
    r = xs[:, 0:aw]
    k = xs[:, aw:2 * aw]
    v = xs[:, 2 * aw:3 * aw]
    o = 3 * aw
    dw = xs[:, o:o + 2 * DECAY_LORA]
    da = xs[:, o + 2 * DECAY_LORA:o + 2 * DECAY_LORA + 2 * ICLR_LORA]
    dg = xs[:, o + 2 * DECAY_LORA + 2 * ICLR_LORA:]
    bd = bd_ref[...]
    kkr = k * kk_ref[...]
    kk = kkr * lax.rsqrt(_dot((kkr * kkr).astype(BF16), bd) + 1e-12)
    tw = jnp.tanh(dw).astype(BF16)
    dab = da.astype(BF16)
    r_o[0] = r.astype(BF16)
    v_o[0] = v.astype(BF16)
    nkk_o[0] = (-kk).astype(BF16)
    for d, (lw_o, kd_o, b_o) in enumerate(((lwf_o, kdf_o, bf_o), (lwb_o, kdb_o, bb_o))):
        lw_o[0] = -DECAY_SCALE * jax.nn.sigmoid(w0_ref[d:d + 1] + _dot(tw, wup_ref[d]))
        a = jax.nn.sigmoid(a0_ref[d:d + 1] + _dot(dab, aup_ref[d]))
        kd_o[0] = (k * (1.0 + (a - 1.0) * ka_ref[...])).astype(BF16)
        b_o[0] = (kk * a).astype(BF16)
    g_o[0] = _dot(jax.nn.sigmoid(dg).astype(BF16), gup_ref[...]).astype(BF16)
    bonus_o[0] = (_dot_x2(r * k * rk_ref[...], bd) * v).astype(BF16)


def _rwkv_prep(u3, shift_w, w0, a0, wup_p, aup_p, gup, k_k, k_a, r_k, bd, tb):
    b, t, cols = u3.shape
    nb = t // tb
    zero = jnp.zeros((b, 1, cols), F32)
    halo_prev = jnp.concatenate([zero, u3[:, tb - 1:t - 1:tb]], axis=1).reshape(b, nb, 1, cols)
    halo_next = jnp.concatenate([u3[:, tb::tb], zero], axis=1).reshape(b, nb, 1, cols)
    aw = A_WIDTH
    full2 = lambda s: pl.BlockSpec(s, lambda i, j: (0, 0))
    full3 = lambda s: pl.BlockSpec(s, lambda i, j: (0, 0, 0))
    out_spec = pl.BlockSpec((1, tb, aw), lambda i, j: (i, j, 0))
    sds = lambda dt: jax.ShapeDtypeStruct((b, t, aw), dt)
    out_dtypes = [BF16] * 7 + [F32] * 2 + [BF16] * 2
    return pl.pallas_call(
        _rwkv_prep_kernel,
        grid=(b, nb),
        in_specs=[pl.BlockSpec((1, tb, cols), lambda i, j: (i, j, 0)),
                  pl.BlockSpec((1, 1, 1, cols), lambda i, j: (i, j, 0, 0)),
                  pl.BlockSpec((1, 1, 1, cols), lambda i, j: (i, j, 0, 0)),
                  full2((3, cols)), full2((2, aw)), full2((2, aw)),
                  full3(wup_p.shape), full3(aup_p.shape), full2(gup.shape),
                  full2((1, aw)), full2((1, aw)), full2((1, aw)), full2((aw, aw))],
        out_specs=[out_spec] * 11,
        out_shape=[sds(dt) for dt in out_dtypes],
        compiler_params=_cparams("arbitrary", "arbitrary"),
        name="rwkv_prep",
    )(u3, halo_prev, halo_next, shift_w, w0, a0, wup_p, aup_p, gup, k_k, k_a, r_k, bd)


def _bdot(a, b):
    return lax.dot_general(a, b, (((2,), (1,)), ((0,), (0,))), preferred_element_type=F32)


def _bdot_nt(a, b):
    return lax.dot_general(a, b, (((2,), (2,)), ((0,), (0,))), preferred_element_type=F32)


def _bdot_tn(a, b):
    return lax.dot_general(a, b, (((1,), (1,)), ((0,), (0,))), preferred_element_type=F32)


def _split_pairs(x):
    return jnp.stack([x[g][:, p * LANES:(p + 1) * LANES]
                      for g in range(x.shape[0]) for p in range(x.shape[2] // LANES)], axis=0)


def _scan_kernel(r_ref, v_ref, nkk_ref, kd_ref, b_ref, lw_ref, s0_ref, o_ref, sf_ref, s_scr, *, reverse):
    c = pl.program_id(1)
    grp, ch, aw = lw_ref.shape
    n_pairs = aw // LANES
    state_shape = s_scr.shape

    @pl.when(c == 0)
    def _():
        s_scr[...] = s0_ref[...].reshape(state_shape)

    lw = lw_ref[...]
    ri = lax.broadcasted_iota(I32, (ch, ch), 0)
    ci = lax.broadcasted_iota(I32, (ch, ch), 1)
    tri = jnp.where((ci >= ri) if reverse else (ci <= ri), 1.0, 0.0).astype(BF16)
    hi = lw.astype(BF16)
    rem = lw - hi.astype(F32)
    mid = rem.astype(BF16)
    lo = (rem - mid.astype(F32)).astype(BF16)
    li = jnp.stack([_dot(tri, hi[g]) + (_dot(tri, mid[g]) + _dot(tri, lo[g])) for g in range(grp)], axis=0)
    lt = li[:, 0:1] if reverse else li[:, ch - 1:ch]
    rho = 0.5 * lt
    e1 = jnp.exp(li - rho)
    e2 = jnp.exp(rho - li)
    er = jnp.exp(rho)
    a_rel = nkk_ref[...] * (e1 * jnp.exp(-lw))
    r_rel = r_ref[...] * e1
    b_rel = b_ref[...] * e2
    k_rel = kd_ref[...] * e2
    pairs = lambda x: _split_pairs(x.astype(BF16))
    a_abs, r_abs, b_end, k_end = pairs(a_rel * er), pairs(r_rel * er), pairs(b_rel * er), pairs(k_rel * er)
    a_rel, r_rel, b_rel, k_rel = pairs(a_rel), pairs(r_rel), pairs(b_rel), pairs(k_rel)
    v = pairs(v_ref[...])
    decay = _split_pairs(jnp.exp(lt))
    lane = lax.broadcasted_iota(I32, (1, 1, LANES), 2)
    first = lane < A_HEAD_DIM

    def blockdiag(x):
        zero = jnp.zeros_like(x)
        return jnp.concatenate([jnp.where(first, x, zero), jnp.where(first, zero, x)], axis=1)

    ri2 = lax.broadcasted_iota(I32, (1, ch, 2 * ch), 1)
    ci2 = lax.broadcasted_iota(I32, (1, ch, 2 * ch), 2)
    ci2 = jnp.where(ci2 >= ch, ci2 - ch, ci2)
    if reverse:
        incl2, strict2 = ci2 >= ri2, ci2 > ri2
    else:
        incl2, strict2 = ci2 <= ri2, ci2 < ri2
    eye2 = jnp.where(ri2 == ci2, 1.0, 0.0)
    lhs = jnp.concatenate([a_rel, r_rel], axis=1)
    with_b = _bdot_nt(lhs, blockdiag(b_rel))
    with_k = _bdot_nt(lhs, blockdiag(k_rel))
    a_ab = jnp.where(strict2, with_b[:, :ch], 0.0)
    a_ak = jnp.where(strict2, with_k[:, :ch], 0.0).astype(BF16)
    a_r = jnp.concatenate([jnp.where(incl2, with_b[:, ch:], 0.0), jnp.where(incl2, with_k[:, ch:], 0.0)],
                          axis=2).astype(BF16)
    inv = eye2
    for k in range(int(math.log2(ch))):
        p, q = (ci2 >> k, ri2 >> k) if reverse else (ri2 >> k, ci2 >> k)
        joins = ((p ^ q) * 4 + (p - q)) == 5
        lk = jnp.where(joins, a_ab, 0.0)
        if k == 0:
            inv = inv + lk
        else:
            invb = inv.astype(BF16)
            inv = inv + _bdot(_bdot(invb, blockdiag(lk.astype(BF16))).astype(BF16), blockdiag(invb))
    s = s_scr[...]
    from_state = _bdot_nt(jnp.concatenate([a_abs, r_abs], axis=1), s.astype(BF16))
    v_bd = blockdiag(v)
    x = from_state[:, :ch] + _bdot(a_ak, v_bd)
    u = _bdot(inv.astype(BF16), blockdiag(x.astype(BF16))).astype(BF16)
    o = from_state[:, ch:] + _bdot(a_r, jnp.concatenate([blockdiag(u), v_bd], axis=1))
    si = lax.broadcasted_iota(I32, (1, LANES, LANES), 1) < A_HEAD_DIM
    same_head = si == (lax.broadcasted_iota(I32, (1, LANES, LANES), 2) < A_HEAD_DIM)
    upd = _bdot_tn(jnp.concatenate([u, v], axis=1), jnp.concatenate([b_end, k_end], axis=1))
    s_scr[...] = s * decay + jnp.where(same_head, upd, 0.0)
    for g in range(grp):
        o_ref[g] = jnp.concatenate([o[g * n_pairs + p] for p in range(n_pairs)], axis=1)

    @pl.when(c == pl.num_programs(1) - 1)
    def _():
        sf_ref[...] = s_scr[...].reshape(sf_ref.shape)


def _wkv_scan(r, v, nkk, kd, b, lw, s0, reverse):
    bsz, t, aw = r.shape
    ch = SCAN_CHUNK
    assert t % ch == 0 and 2 * ch == LANES and 2 * A_HEAD_DIM == LANES
    nc = t // ch
    n_pairs = aw // LANES
    grp = SCAN_GROUP if bsz % SCAN_GROUP == 0 else 1
    tmap = (lambda i, c: (i, nc - 1 - c, 0)) if reverse else (lambda i, c: (i, c, 0))
    seq = pl.BlockSpec((grp, ch, aw), tmap)
    st = pl.BlockSpec((grp, n_pairs, LANES, LANES), lambda i, c: (i, 0, 0, 0))
    o, s_fin = pl.pallas_call(
        functools.partial(_scan_kernel, reverse=reverse),
        grid=(bsz // grp, nc),
        in_specs=[seq] * 6 + [st],
        out_specs=[seq, st],
        out_shape=[jax.ShapeDtypeStruct((bsz, t, aw), F32),
                   jax.ShapeDtypeStruct((bsz, n_pairs, LANES, LANES), F32)],
        scratch_shapes=[pltpu.VMEM((grp * n_pairs, LANES, LANES), F32)],
        compiler_params=_cparams("arbitrary", "arbitrary"),
        name="wkv_scan_bwd" if reverse else "wkv_scan_fwd",
    )(r, v, nkk, kd, b, lw, _pair_states(s0))
    return o, _unpair_states(s_fin)


def _pair_states(s):
    b, h, d, _ = s.shape
    eye = jnp.eye(2, dtype=s.dtype)[None, None, :, None, :, None]
    return (s.reshape(b, h // 2, 2, d, 1, d) * eye).reshape(b, h // 2, 2 * d, 2 * d)


def _unpair_states(s):
    b, p, dd, _ = s.shape
    d = dd // 2
    s6 = s.reshape(b, p, 2, d, 2, d)
    return jnp.stack([s6[:, :, 0, :, 0, :], s6[:, :, 1, :, 1, :]], axis=2).reshape(b, 2 * p, d, d)


def _denominator_lane(width):
    lane = lax.broadcasted_iota(I32, (1, width), 1)
    return jnp.where((lane & (HEAD_PAD - 1)) == V_HEAD, 1.0, 0.0)


def _rope_lanes(x, cos_t, sin_up, sin_dn):
    w = x.shape[1]
    half = QK_ROPE // 2
    return x * cos_t + pltpu.roll(x, half, 1) * sin_up + pltpu.roll(x, w - half, 1) * sin_dn


def _mla_prep_kernel(*refs, rope):
    if rope:
        (m_ref, gq_ref, gkv_ref, wuq_ref, wk_ref, wv_ref, p_ref,
         qc_ref, qu_ref, qd_ref, kc_ref, ku_ref, kd_ref, q_o, k_o, v_o, ckv_o) = refs
    else:
        (m_ref, gq_ref, gkv_ref, wuq_ref, wk_ref, wv_ref, p_ref, q_o, k_o, v_o, ckv_o) = refs
    m = m_ref[...]
    qn = _rms(m[:, :Q_LORA], gq_ref[...])
    ckv = _rms(m[:, Q_LORA:Q_LORA + KV_LORA], gkv_ref[...])
    kp = m[:, Q_LORA + KV_LORA:]
    q = _dot(qn.astype(BF16), wuq_ref[...]) * ATTN_SCALE
    if rope:
        tile = lambda ref: jnp.concatenate([ref[...]] * B_HEADS, axis=1)
        q = _rope_lanes(q, tile(qc_ref), tile(qu_ref), tile(qd_ref))
        kp = _rope_lanes(kp, kc_ref[...], ku_ref[...], kd_ref[...])
    cb = ckv.astype(BF16)
    q_o[...] = q.astype(BF16)
    k_o[...] = (_dot(cb, wk_ref[...]) + _dot(kp.astype(BF16), p_ref[...])).astype(BF16)
    v_o[...] = (_dot(cb, wv_ref[...]) + _denominator_lane(v_o.shape[1])).astype(BF16)
    ckv_o[...] = ckv


def _mla_prep(m2, g_q, g_kv, wuq_p, wk_p, wv_p, place, rope_tabs, tb, t):
    n = m2.shape[0]
    hw = B_HEADS * HEAD_PAD
    full = lambda a: pl.BlockSpec(a.shape, lambda i: (0, 0))
    ins = [m2, g_q, g_kv, wuq_p, wk_p, wv_p, place]
    specs = [pl.BlockSpec((tb, M_COLS), lambda i: (i, 0))] + [full(a) for a in ins[1:]]
    if rope_tabs is not None:
        nbt = t // tb
        ins += list(rope_tabs)
        specs += [pl.BlockSpec((tb, HEAD_PAD), lambda i: (i % nbt, 0))] * 6
    big = pl.BlockSpec((tb, hw), lambda i: (i, 0))
    return pl.pallas_call(
        functools.partial(_mla_prep_kernel, rope=rope_tabs is not None),
        grid=(n // tb,),
        in_specs=specs,
        out_specs=[big, big, big, pl.BlockSpec((tb, KV_LORA), lambda i: (i, 0))],
        out_shape=[jax.ShapeDtypeStruct((n, hw), BF16)] * 3 + [jax.ShapeDtypeStruct((n, KV_LORA), F32)],
        compiler_params=_cparams("arbitrary"),
        name="mla_prep",
    )(*ins)


def _kv_up_kernel(ckv_ref, kp_ref, wk_ref, wv_ref, p_ref, k_o, v_o):
    cb = ckv_ref[...].astype(BF16)
    k_o[...] = (_dot(cb, wk_ref[...]) + _dot(kp_ref[...].astype(BF16), p_ref[...])).astype(BF16)
    v_o[...] = (_dot(cb, wv_ref[...]) + _denominator_lane(v_o.shape[1])).astype(BF16)


def _kv_up(ckv2, kpe_pad, wk_p, wv_p, place, tb):
    n = ckv2.shape[0]
    hw = B_HEADS * HEAD_PAD
    full = lambda a: pl.BlockSpec(a.shape, lambda i: (0, 0))
    big = pl.BlockSpec((tb, hw), lambda i: (i, 0))
    return pl.pallas_call(
        _kv_up_kernel,
        grid=(n // tb,),
        in_specs=[pl.BlockSpec((tb, KV_LORA), lambda i: (i, 0)),
                  pl.BlockSpec((tb, HEAD_PAD), lambda i: (i, 0)),
                  full(wk_p), full(wv_p), full(place)],
        out_specs=[big, big],
        out_shape=[jax.ShapeDtypeStruct((n, hw), BF16)] * 2,
        compiler_params=_cparams("arbitrary"),
        name="kv_up",
    )(ckv2, kpe_pad, wk_p, wv_p, place)


def _attn_kernel(*refs, two):
    if two:
        q_ref, k1_ref, v1_ref, k2_ref, v2_ref, o_ref = refs
    else:
        q_ref, k1_ref, v1_ref, o_ref = refs
    for h in range(B_HEADS):
        hs = slice(h * HEAD_PAD, (h + 1) * HEAD_PAD)
        q = q_ref[0, :, hs]
        s1 = _dot_nt(q, k1_ref[0, :, hs])
        mx = jnp.max(s1, axis=-1, keepdims=True)
        if two:
            s2 = _dot_nt(q, k2_ref[0, :, hs])
            mx = jnp.maximum(mx, jnp.max(s2, axis=-1, keepdims=True))
        acc = _dot(jnp.exp(s1 - mx).astype(BF16), v1_ref[0, :, hs])
        if two:
            acc = acc + _dot(jnp.exp(s2 - mx).astype(BF16), v2_ref[0, :, hs])
        o_ref[0, :, hs] = (acc / acc[:, V_HEAD:V_HEAD + 1]).astype(BF16)


def _attention(q3, k1, v1, k2, v2, tq):
    b, t, hw = q3.shape
    two = k2 is not None
    qspec = pl.BlockSpec((1, tq, hw), lambda i, j: (i, j, 0))
    kv = lambda a: pl.BlockSpec((1, a.shape[1], hw), lambda i, j: (i, 0, 0))
    ins = [q3, k1, v1] + ([k2, v2] if two else [])
    return pl.pallas_call(
        functools.partial(_attn_kernel, two=two),
        grid=(b, t // tq),
        in_specs=[qspec] + [kv(a) for a in ins[1:]],
        out_specs=qspec,
        out_shape=jax.ShapeDtypeStruct((b, t, hw), BF16),
        compiler_params=_cparams("arbitrary", "arbitrary"),
        name="attention",
    )(*ins)


def _post_mix_kernel(of_ref, ob_ref, bonus_ref, g_ref, att_ref, gates_ref, x_ref,
                     g1_ref, sc2_ref, sh2_ref, lnw_ref, lnb_ref, bd_ref, wa_ref, wb_ref, wo_ref,
                     gpost_ref, gpre_ref, wr_ref, x1_o, h2_o, aff_o):
    bd = bd_ref[...]
    inv_n = 1.0 / A_HEAD_DIM
    o = of_ref[...] + ob_ref[...]
    mu = _dot_x2(o, bd) * inv_n
    oc = o - mu
    var = _dot((oc * oc).astype(BF16), bd) * inv_n
    on = (oc * lax.rsqrt(var + GN_EPS)) * lnw_ref[...] + lnb_ref[...]
    ya = _dot(((on + bonus_ref[...]) * g_ref[...]).astype(BF16), wa_ref[...])
    yb = _dot(att_ref[...], wb_ref[...])
    d = ya.shape[1]
    gates = gates_ref[...].astype(F32)
    mix = jax.nn.sigmoid(gates[:, :d]) * ya + jax.nn.sigmoid(gates[:, d:]) * yb
    y = _dot(mix.astype(BF16), wo_ref[...])
    x1 = x_ref[...] + g1_ref[0] * _rms(y, gpost_ref[...])
    x1_o[...] = x1
    h2 = _rms(x1, gpre_ref[...]) * (1.0 + sc2_ref[0]) + sh2_ref[0]
    h2_o[...] = h2.astype(BF16)
    hh, hl = _split2(h2)
    wh, wl = _split2(wr_ref[...])
    logits = _dot_nt(wh, hh) + (_dot_nt(wh, hl) + _dot_nt(wl, hh))
    z = jnp.exp(logits - jnp.max(logits, axis=0, keepdims=True))
    aff_o[...] = z / jnp.sum(z, axis=0, keepdims=True)


def _post_mix(o_f, o_b, bonus, g, att, gates, x2, g1, sc2, sh2, ln_w, ln_b, bd, wa, wb, wo,
              g_post, g_pre, w_router_t, tb, blocks_per_batch):
    n, d = x2.shape
    tok = lambda a: pl.BlockSpec((tb, a.shape[1]), lambda i: (i, 0))
    full = lambda a: pl.BlockSpec(a.shape, lambda i: (0, 0))
    ms = lambda a: _mod_spec(a, blocks_per_batch)
    return pl.pallas_call(
        _post_mix_kernel,
        grid=(n // tb,),
        in_specs=[tok(o_f), tok(o_b), tok(bonus), tok(g), tok(att), tok(gates), tok(x2),
                  ms(g1), ms(sc2), ms(sh2), full(ln_w), full(ln_b), full(bd), full(wa), full(wb),
                  full(wo), full(g_post), full(g_pre), full(w_router_t)],
        out_specs=[pl.BlockSpec((tb, d), lambda i: (i, 0)),
                   pl.BlockSpec((tb, d), lambda i: (i, 0)),
                   pl.BlockSpec((N_EXPERTS, tb), lambda i: (0, i))],
        out_shape=[jax.ShapeDtypeStruct((n, d), F32),
                   jax.ShapeDtypeStruct((n, d), BF16),
                   jax.ShapeDtypeStruct((N_EXPERTS, n), F32)],
        compiler_params=_cparams("arbitrary"),
        name="post_mix",
    )(o_f, o_b, bonus, g, att, gates, x2, g1, sc2, sh2, ln_w, ln_b, bd, wa, wb, wo,
      g_post, g_pre, w_router_t)


def _select_kernel(a_ref, pos_ref, sel_ref, *, cap, n_chunks):
    a = a_ref[...]
    rows = a.shape[0]
    a3 = a.reshape(N_EXPERTS, n_chunks, LANES)

    def count(mask):
        c = jnp.sum(jnp.where(mask, 1.0, 0.0), axis=2, keepdims=True)
        return jnp.sum(c, axis=1, keepdims=True)

    def body(i, thr):
        cand = thr | jnp.left_shift(jnp.int32(1), 30 - i)
        return jnp.where(count(a3 >= pltpu.bitcast(cand, F32)) >= cap, cand, thr)

    thr = pltpu.bitcast(lax.fori_loop(0, 31, body, jnp.zeros((N_EXPERTS, 1, LANES), I32)), F32)
    gt = a3 > thr
    eq = a3 == thr
    need = cap - count(gt)

    li = lax.broadcasted_iota(I32, (LANES, LANES), 0)
    lj = lax.broadcasted_iota(I32, (LANES, LANES), 1)
    lane_before = jnp.where(li < lj, 1.0, 0.0).astype(BF16)
    lane_all = jnp.ones((LANES, LANES), BF16)
    ci = lax.broadcasted_iota(I32, (n_chunks, n_chunks), 0)
    cj = lax.broadcasted_iota(I32, (n_chunks, n_chunks), 1)
    chunk_before = jnp.where(cj < ci, 1.0, 0.0).astype(BF16)

    def prefix(flags3):
        f2 = flags3.reshape(rows, LANES).astype(BF16)
        within = _dot(f2, lane_before).reshape(N_EXPERTS, n_chunks, LANES)
        tot = _dot(f2, lane_all).astype(BF16).reshape(N_EXPERTS, n_chunks, LANES)
        offs = [_dot(chunk_before, tot[e]) for e in range(N_EXPERTS)]
        return within + jnp.stack(offs, axis=0)

    eqf = jnp.where(eq, 1.0, 0.0)
    tie_ok = jnp.where(prefix(eqf) < need, eqf, 0.0)
    sel = jnp.where(gt, 1.0, tie_ok)
    pos_ref[...] = prefix(sel).reshape(rows, LANES).astype(I32)
    sel_ref[...] = sel.reshape(rows, LANES).astype(I32)


def _select(aff_t, cap):
    e, n = aff_t.shape
    n_chunks = n // LANES
    rows = e * n_chunks
    a2 = aff_t.reshape(rows, LANES)
    spec = pl.BlockSpec((rows, LANES), lambda i: (0, 0))
    pos, sel = pl.pallas_call(
        functools.partial(_select_kernel, cap=cap, n_chunks=n_chunks),
        grid=(1,),
        in_specs=[spec],
        out_specs=[spec, spec],
        out_shape=[jax.ShapeDtypeStruct((rows, LANES), I32)] * 2,
        compiler_params=_cparams("arbitrary"),
        name="expert_select",
    )(a2)
    return pos.reshape(e, n), sel.reshape(e, n)


ITEM_VALID, ITEM_FIRST, ITEM_LAST = 1, 2, 4


def _ffn_kernel(blk_ref, tile_ref, flag_ref, slot_ref, aff_ref, h_ref, wg_ref, wu_ref, wd_ref, y_ref,
                acc_ref, wacc_ref, wg_bf, wu_bf, wd_bf, *, steps):
    e = pl.program_id(0)
    s = pl.program_id(1)
    flags = flag_ref[e * steps + s]
    tile = tile_ref[e * steps + s]
    rows = acc_ref.shape[0]

    @pl.when(s == 0)
    def _():
        wg_bf[...] = wg_ref[0, 0].astype(BF16)
        wu_bf[...] = wu_ref[0, 0].astype(BF16)
        wd_bf[...] = wd_ref[0, 0].astype(BF16)

    @pl.when((flags & ITEM_FIRST) != 0)
    def _():
        acc_ref[...] = jnp.zeros_like(acc_ref)
        wacc_ref[...] = jnp.zeros_like(wacc_ref)

    @pl.when((flags & ITEM_VALID) != 0)
    def _():
        slot = slot_ref[0] - tile * rows
        hit = lax.broadcasted_iota(I32, (rows, slot.shape[1]), 0) == slot
        acc_ref[...] += _dot(jnp.where(hit, 1.0, 0.0).astype(BF16), h_ref[...])
        wacc_ref[...] += jnp.sum(jnp.where(hit, aff_ref[pl.ds(e, 1), :], 0.0), axis=1, keepdims=True)

    @pl.when((flags & ITEM_LAST) != 0)
    def _():
        xe = acc_ref[...].astype(BF16)
        gate = _dot(xe, wg_bf[...])
        up = _dot(xe, wu_bf[...])
        hid = (gate * jax.nn.sigmoid(gate)) * up
        y_ref[0] = (_dot(hid.astype(BF16), wd_bf[...]) * wacc_ref[...]).astype(BF16)


def _expert_ffn(items, slot3, aff_t, h2, wg, wu, wd, layer, cap, tb, rt):
    blk, tile, flags, steps = items
    n, d = h2.shape
    nb = n // tb
    _, e, _, f = wg.shape
    ix = lambda ei, si: ei * steps + si
    grid_spec = pltpu.PrefetchScalarGridSpec(
        num_scalar_prefetch=3,
        grid=(e, steps),
        in_specs=[pl.BlockSpec((1, 1, tb), lambda ei, si, b, t, fl: (ei * nb + b[ix(ei, si)], 0, 0)),
                  pl.BlockSpec((e, tb), lambda ei, si, b, t, fl: (0, b[ix(ei, si)])),
                  pl.BlockSpec((tb, d), lambda ei, si, b, t, fl: (b[ix(ei, si)], 0)),
                  pl.BlockSpec((1, 1, d, f), lambda ei, si, b, t, fl: (layer, ei, 0, 0)),
                  pl.BlockSpec((1, 1, d, f), lambda ei, si, b, t, fl: (layer, ei, 0, 0)),
                  pl.BlockSpec((1, 1, f, d), lambda ei, si, b, t, fl: (layer, ei, 0, 0))],
        out_specs=pl.BlockSpec((1, rt, d), lambda ei, si, b, t, fl: (ei, t[ix(ei, si)], 0)),
        scratch_shapes=[pltpu.VMEM((rt, d), F32), pltpu.VMEM((rt, 1), F32),
                        pltpu.VMEM((d, f), BF16), pltpu.VMEM((d, f), BF16), pltpu.VMEM((f, d), BF16)],
    )
    return pl.pallas_call(
        functools.partial(_ffn_kernel, steps=steps),
        grid_spec=grid_spec,
        out_shape=jax.ShapeDtypeStruct((e, cap, d), BF16),
        compiler_params=_cparams("arbitrary", "arbitrary"),
        name="expert_ffn",
    )(blk, tile, flags, slot3, aff_t, h2, wg, wu, wd)


def _tile_ranges(starts, cap, rt):
    lo, hi = starts[:, :-1], starts[:, 1:]
    cnt = hi - lo
    n_tiles = cap // rt
    t_lo = jnp.minimum(lo // rt, n_tiles - 1)
    t_hi = jnp.where(cnt > 0, (hi - 1) // rt, t_lo)
    return t_lo, jnp.where(cnt > 0, t_hi - t_lo + 1, 0)


def _ffn_items(starts, cap, rt):
    e, nb1 = starts.shape
    nb = nb1 - 1
    steps = nb + cap // rt
    t_lo, nt = _tile_ranges(starts, cap, rt)

    def one(t_lo_e, nt_e):
        offs = jnp.cumsum(nt_e) - nt_e
        total = jnp.sum(nt_e)
        s = jnp.arange(steps, dtype=I32)
        sc = jnp.minimum(s, total - 1)
        b = (jnp.sum((offs[None, :] <= sc[:, None]).astype(I32), axis=1) - 1).astype(I32)
        tile = t_lo_e[b] + (sc - offs[b])
        valid = s < total
        prev_t = jnp.concatenate([jnp.full((1,), -1, I32), tile[:-1]])
        next_t = jnp.concatenate([tile[1:], jnp.full((1,), -1, I32)])
        first = valid & (tile != prev_t)
        last = valid & ((tile != next_t) | (s == total - 1))
        fl = valid * ITEM_VALID + first * ITEM_FIRST + last * ITEM_LAST
        return b, tile.astype(I32), fl.astype(I32)

    b, tile, fl = jax.vmap(one)(t_lo.astype(I32), nt.astype(I32))
    return b.reshape(-1), tile.reshape(-1), fl.reshape(-1), steps


def _combine_kernel(t0_ref, fetch_ref, need_ref, slot_ref, x_ref, g2_ref, gp_ref, *rest, n_win):
    y_refs, o_ref, acc_ref = rest[:-2], rest[-2], rest[-1]
    blk = pl.program_id(0)
    n_blk = pl.num_programs(0)
    tb = slot_ref.shape[0]
    rows = y_refs[0].shape[1]
    lane = lax.broadcasted_iota(I32, (tb, rows), 1)

    def picked(e, k):
        slot = slot_ref[:, e:e + 1] - (t0_ref[blk * N_EXPERTS + e] + k) * rows
        return _dot(jnp.where(lane == slot, 1.0, 0.0).astype(BF16), y_refs[e * n_win + k][0])

    acc = picked(0, 0)
    for e in range(1, N_EXPERTS):
        acc = acc + picked(e, 0)
    acc_ref[...] = acc
    for k in range(1, n_win):
        for e in range(N_EXPERTS):
            @pl.when(need_ref[(k * n_blk + blk) * N_EXPERTS + e] != 0)
            def _():
                acc_ref[...] += picked(e, k)
    o_ref[...] = x_ref[...] + g2_ref[0] * _rms(acc_ref[...], gp_ref[...])


def _combine_windows(pos, tb, rt, cap):
    n_tiles = cap // rt
    lo = pos[:, ::tb]
    hi = jnp.concatenate([lo[:, 1:], jnp.full((lo.shape[0], 1), cap, I32)], axis=1)
    t0 = jnp.minimum(lo // rt, n_tiles - 1)
    t_last = jnp.where(hi > lo, (hi - 1) // rt, t0)
    fetch, need = [t0], [jnp.ones_like(t0)]
    for k in range(1, tb // rt + 1):
        need.append(t_last >= t0 + k)
        fetch.append(lax.cummax(jnp.where(need[k], t0 + k, 0), axis=1))
    flat = lambda a: a.T.reshape(-1).astype(I32)
    return flat(t0), jnp.concatenate([flat(f) for f in fetch]), jnp.concatenate([flat(f) for f in need])


def _combine(windows, slot_t, ye, x1, g2, g_post, tb, rt, blocks_per_batch):
    t0, fetch, need = windows
    n, d = x1.shape
    e, cap, _ = ye.shape
    nb = n // tb
    n_win = tb // rt + 1
    if g2.shape[0] == 1:
        g2_spec = pl.BlockSpec((1, 1, d), lambda b, t, f, nd: (0, 0, 0))
    else:
        g2_spec = pl.BlockSpec((1, 1, d), lambda b, t, f, nd: (b // blocks_per_batch, 0, 0))

    def window(ei, k):
        return pl.BlockSpec((1, rt, d), lambda b, t, f, nd: (ei, f[(k * nb + b) * e + ei], 0))

    grid_spec = pltpu.PrefetchScalarGridSpec(
        num_scalar_prefetch=3,
        grid=(nb,),
        in_specs=[pl.BlockSpec((tb, e), lambda b, t, f, nd: (b, 0)),
                  pl.BlockSpec((tb, d), lambda b, t, f, nd: (b, 0)),
                  g2_spec,
                  pl.BlockSpec((1, d), lambda b, t, f, nd: (0, 0))]
                 + [window(ei, k) for ei in range(e) for k in range(n_win)],
        out_specs=pl.BlockSpec((tb, d), lambda b, t, f, nd: (b, 0)),
        scratch_shapes=[pltpu.VMEM((tb, d), F32)],
    )
    return pl.pallas_call(
        functools.partial(_combine_kernel, n_win=n_win),
        grid_spec=grid_spec,
        out_shape=jax.ShapeDtypeStruct((n, d), F32),
        compiler_params=_cparams("arbitrary"),
        name="moe_combine",
    )(t0, fetch, need, slot_t, x1, g2, g_post, *([ye] * (e * n_win)))


def _rope_tables(t):
    half = QK_ROPE // 2
    pos = jnp.arange(t)
    row = (pos // GRID_W).astype(F32)
    col = (pos % GRID_W).astype(F32)
    inv = ROPE_BASE ** (-jnp.arange(0, half, 2, dtype=F32) / half)
    ang = jnp.concatenate([row[:, None] * inv, col[:, None] * inv], axis=-1)
    cos, sin = jnp.cos(ang), jnp.sin(ang)

    def tabs(first_lane):
        c = jnp.ones((t, HEAD_PAD), F32)
        c = c.at[:, first_lane:first_lane + half].set(cos).at[:, first_lane + half:first_lane + 2 * half].set(cos)
        up = jnp.zeros((t, HEAD_PAD), F32).at[:, first_lane + half:first_lane + 2 * half].set(sin)
        dn = jnp.zeros((t, HEAD_PAD), F32).at[:, first_lane:first_lane + half].set(-sin)
        return c, up, dn

    return tabs(QK_NOPE + QK_ROPE) + tabs(QK_ROPE)


def _layout_weights(w_in, w_uq, w_ukv, w_branch_b, w_up, a_up):
    d = w_in.shape[0]
    deint = jnp.concatenate([jnp.arange(0, QK_ROPE, 2), jnp.arange(1, QK_ROPE, 2)])
    kpe0 = RWKV_COLS + Q_LORA + KV_LORA
    kpe_cols = w_in[:, kpe0:kpe0 + QK_ROPE]
    w_in_p = jnp.concatenate(
        [w_in[:, :kpe0 + QK_ROPE], kpe_cols[:, deint],
         jnp.zeros((d, M_COLS - MLA_COLS - QK_ROPE), F32), w_in[:, RWKV_COLS + MLA_COLS:]], axis=1)
    uq = w_uq.reshape(Q_LORA, B_HEADS, QK_NOPE + QK_ROPE)
    wuq_p = jnp.concatenate([uq, uq[:, :, QK_NOPE:][:, :, deint]], axis=2).reshape(Q_LORA, B_HEADS * HEAD_PAD)
    ukv = w_ukv.reshape(KV_LORA, B_HEADS, QK_NOPE + V_HEAD)
    zpad = jnp.zeros((KV_LORA, B_HEADS, HEAD_PAD - QK_NOPE), F32)
    wk_p = jnp.concatenate([ukv[:, :, :QK_NOPE], zpad], axis=2).reshape(KV_LORA, B_HEADS * HEAD_PAD)
    wv_p = jnp.concatenate([ukv[:, :, QK_NOPE:], zpad], axis=2).reshape(KV_LORA, B_HEADS * HEAD_PAD)
    wb = w_branch_b.reshape(B_HEADS, V_HEAD, d)
    wb_p = jnp.concatenate([wb, jnp.zeros((B_HEADS, HEAD_PAD - V_HEAD, d), F32)], axis=1).reshape(B_HEADS * HEAD_PAD, d)
    eye = jnp.eye(QK_ROPE, dtype=F32)
    z = jnp.zeros((QK_ROPE, QK_ROPE), F32)
    head_raw = jnp.concatenate([jnp.zeros((QK_ROPE, QK_NOPE), F32), eye, z], axis=1)
    head_rot = jnp.concatenate([jnp.zeros((QK_ROPE, QK_NOPE), F32), z, eye], axis=1)
    zrows = jnp.zeros((HEAD_PAD - 2 * QK_ROPE, B_HEADS * HEAD_PAD), F32)
    zr = jnp.zeros((QK_ROPE, B_HEADS * HEAD_PAD), F32)
    place_raw = jnp.concatenate([jnp.tile(head_raw, (1, B_HEADS)), zr, zrows], axis=0)
    place_rot = jnp.concatenate([zr, jnp.tile(head_rot, (1, B_HEADS)), zrows], axis=0)
    zl = jnp.zeros((DECAY_LORA, A_WIDTH), F32)
    wup_p = jnp.stack([jnp.concatenate([w_up[0], zl]), jnp.concatenate([zl, w_up[1]])])
    aup_p = jnp.stack([jnp.concatenate([a_up[0], zl]), jnp.concatenate([zl, a_up[1]])])
    bf = lambda a: a.astype(BF16)
    return (bf(w_in_p), bf(wuq_p), bf(wk_p), bf(wv_p), bf(wb_p), bf(place_raw), bf(place_rot),
            bf(wup_p), bf(aup_p))


def _group_layer(x, mods, lw, s0_f, s0_b, ctx_kv, rope_tabs):
    bsz, t, d = x.shape
    n = bsz * t
    tb = min(TOKEN_BLOCK, t)
    bpb = t // tb
    sh1, sc1, g1, sh2, sc2, g2 = mods
    x2 = x.reshape(n, d)
    u, m, gates = _inproj(x2, sc1, sh1, lw["g_pre_mix"], lw["w_in_p"], tb, bpb)
    prep = _rwkv_prep(u.reshape(bsz, t, RWKV_COLS), lw["shift_w"], lw["w0"], lw["a0"], lw["wup_p"],
                      lw["aup_p"], lw["g_up"], lw["k_k"], lw["k_a"], lw["r_k"], lw["bd"], tb)
    r, v, nkk, kd_f, kd_b, b_f, b_b, lw_f, lw_b, g, bonus = prep
    o_f, s_f = _wkv_scan(r, v, nkk, kd_f, b_f, lw_f, s0_f, reverse=False)
    o_b, s_b = _wkv_scan(r, v, nkk, kd_b, b_b, lw_b, s0_b, reverse=True)
    place = lw["place_raw"] if rope_tabs is None else lw["place_rot"]
    q, k, vv, ckv = _mla_prep(m, lw["g_qnorm"], lw["g_kvnorm"], lw["wuq_p"], lw["wk_p"], lw["wv_p"],
                              place, rope_tabs, tb, t)
    hw = B_HEADS * HEAD_PAD
    k2, v2 = ctx_kv if ctx_kv is not None else (None, None)
    att = _attention(q.reshape(bsz, t, hw), k.reshape(bsz, t, hw), vv.reshape(bsz, t, hw), k2, v2, tb)
    two = lambda a: a.reshape(n, a.shape[-1])
    x1, h2, aff_t = _post_mix(two(o_f), two(o_b), two(bonus), two(g), att.reshape(n, hw), gates, x2,
                              g1, sc2, sh2, lw["ln_x_w"], lw["ln_x_b"], lw["bd"], lw["w_branch_a"],
                              lw["wb_p"], lw["w_out"], lw["g_post_mix"], lw["g_pre_ffn"],
                              lw["w_router_t"], tb, bpb)
    cap = CAPACITY_FACTOR * n // N_EXPERTS
    rt = min(EXPERT_TILE, cap)
    pos, sel = _select(aff_t, cap)
    slot = jnp.where(sel > 0, pos, -1)
    td = min(DISPATCH_BLOCK, n)
    starts = jnp.concatenate([pos[:, ::td], jnp.full((N_EXPERTS, 1), cap, I32)], axis=1)
    ye = _expert_ffn(_ffn_items(starts, cap, rt), slot.reshape(N_EXPERTS * (n // td), 1, td), aff_t, h2,
                     lw["w_exp_gate"], lw["w_exp_up"], lw["w_exp_down"], lw["layer"], cap, td, rt)
    out = _combine(_combine_windows(pos, tb, rt, cap), slot.T, ye, x1, g2, lw["g_post_ffn"], tb, rt, bpb)
    kpe = m[:, Q_LORA + KV_LORA:Q_LORA + KV_LORA + QK_ROPE]
    return out.reshape(bsz, t, d), (ckv.reshape(bsz, t, KV_LORA), kpe.reshape(bsz, t, QK_ROPE), s_f, s_b)


def kernel(x_prompt, x_sample, cache_ckv, cache_kpe, state_wkv_fwd, state_wkv_bwd, c, c_ctx,
           w_mod, b_mod, g_pre_mix, g_post_mix, g_pre_ffn, g_post_ffn, w_in, shift_w,
           w0, w_up, a0, a_up, g_up, k_k, k_a, r_k, ln_x_w, ln_x_b, w_branch_a,
           g_qnorm, w_uq, g_kvnorm, w_ukv, w_branch_b, w_out,
           w_router, w_exp_gate, w_exp_up, w_exp_down):
    depth = w_mod.shape[0]
    d = x_prompt.shape[-1]
    dec_b, dec_t = x_sample.shape[0], x_sample.shape[1]
    xp, xs = x_prompt, x_sample
    c_rows = jnp.concatenate([c, c_ctx[None, :],
                              jnp.zeros((-(dec_b + 1) % 8, d), F32)], axis=0)
    rope_tabs = _rope_tables(dec_t)
    ii = lax.broadcasted_iota(I32, (A_WIDTH, A_WIDTH), 0) // A_HEAD_DIM
    jj = lax.broadcasted_iota(I32, (A_WIDTH, A_WIDTH), 1) // A_HEAD_DIM
    bd = (ii == jj).astype(BF16)
    row = lambda a: a.reshape(1, -1)
    bf = lambda a: a.astype(BF16)
    ckv_l, kpe_l, sf_l, sb_l = [], [], [], []
    for l in range(depth):
        (w_in_p, wuq_p, wk_p, wv_p, wb_p, place_raw, place_rot, wup_p, aup_p) = _layout_weights(
            w_in[l], w_uq[l], w_ukv[l], w_branch_b[l], w_up[l], a_up[l])
        lw = {
            "g_pre_mix": row(g_pre_mix[l]), "g_post_mix": row(g_post_mix[l]),
            "g_pre_ffn": row(g_pre_ffn[l]), "g_post_ffn": row(g_post_ffn[l]),
            "w_in_p": w_in_p, "shift_w": shift_w[l], "w0": w0[l], "a0": a0[l],
            "wup_p": wup_p, "aup_p": aup_p, "g_up": bf(g_up[l]),
            "k_k": row(k_k[l]), "k_a": row(k_a[l]), "r_k": row(r_k[l]), "bd": bd,
            "ln_x_w": row(ln_x_w[l]), "ln_x_b": row(ln_x_b[l]), "w_branch_a": bf(w_branch_a[l]),
            "g_qnorm": row(g_qnorm[l]), "g_kvnorm": row(g_kvnorm[l]),
            "wuq_p": wuq_p, "wk_p": wk_p, "wv_p": wv_p, "wb_p": wb_p,
            "place_raw": place_raw, "place_rot": place_rot,
            "w_out": bf(w_out[l]), "w_router_t": w_router[l].T,
            "w_exp_gate": w_exp_gate, "w_exp_up": w_exp_up, "w_exp_down": w_exp_down, "layer": l,
        }
        mod = _modulation(c_rows, w_mod[l], b_mod[l])
        mods_lat = [mod[:dec_b, i * d:(i + 1) * d].reshape(dec_b, 1, d) for i in range(6)]
        mods_ctx = [mod[dec_b:dec_b + 1, i * d:(i + 1) * d].reshape(1, 1, d) for i in range(6)]
        zeros_state = jnp.zeros((xp.shape[0], A_HEADS, A_HEAD_DIM, A_HEAD_DIM), F32)
        xp, (ckv, kpe, s_f, s_b) = _group_layer(xp, mods_ctx, lw, zeros_state, zeros_state, None, None)
        ckv_l.append(ckv)
        kpe_l.append(kpe)
        sf_l.append(s_f)
        sb_l.append(s_b)
        past = cache_ckv.shape[2]
        kpe_pad = jnp.concatenate(
            [cache_kpe[:, l], jnp.zeros((dec_b, past, HEAD_PAD - QK_ROPE), F32)], axis=-1)
        k_ctx, v_ctx = _kv_up(cache_ckv[:, l].reshape(dec_b * past, KV_LORA),
                              kpe_pad.reshape(dec_b * past, HEAD_PAD), wk_p, wv_p, place_raw,
                              min(TOKEN_BLOCK, past))
        hw = B_HEADS * HEAD_PAD
        ctx_kv = (k_ctx.reshape(dec_b, past, hw), v_ctx.reshape(dec_b, past, hw))
        xs, _ = _group_layer(xs, mods_lat, lw, state_wkv_fwd[:, l], state_wkv_bwd[:, l], ctx_kv, rope_tabs)
    return (xp, xs, jnp.stack(ckv_l, axis=1), jnp.stack(kpe_l, axis=1),
            jnp.stack(sf_l, axis=1), jnp.stack(sb_l, axis=1))
```

```python
import functools
import math

import jax
import jax.numpy as jnp
from jax import lax
from jax.experimental import pallas as pl
from jax.experimental.pallas import tpu as pltpu

F32 = jnp.float32
BF16 = jnp.bfloat16
I32 = jnp.int32

GRID_W = 64
A_HEADS = 8
A_HEAD_DIM = 64
A_WIDTH = A_HEADS * A_HEAD_DIM
DECAY_LORA = 64
ICLR_LORA = 64
GATE_LORA = 128
DECAY_SCALE = 0.6065306597126334
GN_EPS = 64e-5
B_HEADS = 8
Q_LORA = 256
KV_LORA = 128
QK_NOPE = 64
QK_ROPE = 32
V_HEAD = 64
ROPE_BASE = 10000.0
ATTN_SCALE = 1.0 / math.sqrt(QK_NOPE + QK_ROPE)
N_EXPERTS = 16
CAPACITY_FACTOR = 2
EPS = 1e-6
RWKV_COLS = 3 * A_WIDTH + 2 * DECAY_LORA + 2 * ICLR_LORA + GATE_LORA
MLA_COLS = Q_LORA + KV_LORA + QK_ROPE

LANES = 128
HEAD_PAD = 128
M_COLS = 512
VMEM_LIMIT = 56 * 1024 * 1024

SCAN_CHUNK = 64
SCAN_GROUP = 8
TOKEN_BLOCK = 256
EXPERT_TILE = 256
GATHER_BLOCK = 512


def _cparams(*sem):
    return pltpu.CompilerParams(dimension_semantics=sem, vmem_limit_bytes=VMEM_LIMIT)


def _dot(a, b):
    return jnp.dot(a, b, preferred_element_type=F32)


def _dot_nt(a, b):
    return lax.dot_general(a, b, (((1,), (1,)), ((), ())), preferred_element_type=F32)


def _dot_tn(a, b):
    return lax.dot_general(a, b, (((0,), (0,)), ((), ())), preferred_element_type=F32)


def _split2(x):
    hi = x.astype(BF16)
    lo = (x - hi.astype(F32)).astype(BF16)
    return hi, lo


def _dot_x2(a, b_bf16):
    hi, lo = _split2(a)
    return _dot(hi, b_bf16) + _dot(lo, b_bf16)


def _dot_f32(a, b):
    ah, al = _split2(a)
    bh, bl = _split2(b)
    return _dot(ah, bh) + (_dot(ah, bl) + _dot(al, bh))


def _rms(x, g):
    return (x * lax.rsqrt(jnp.mean(x * x, axis=-1, keepdims=True) + EPS)) * g


def _mod_kernel(c_ref, w_ref, b_ref, o_ref):
    c = c_ref[...]
    s = c * jax.nn.sigmoid(c)
    o_ref[...] = _dot_f32(s, w_ref[...]) + b_ref[...]


def _modulation(c_rows, w_mod, b_mod):
    rows, d = c_rows.shape
    n_out = w_mod.shape[1]
    tn = n_out // 8
    return pl.pallas_call(
        _mod_kernel,
        grid=(n_out // tn,),
        in_specs=[pl.BlockSpec((rows, d), lambda j: (0, 0)),
                  pl.BlockSpec((d, tn), lambda j: (0, j)),
                  pl.BlockSpec((1, tn), lambda j: (0, j))],
        out_specs=pl.BlockSpec((rows, tn), lambda j: (0, j)),
        out_shape=jax.ShapeDtypeStruct((rows, n_out), F32),
        compiler_params=_cparams("arbitrary"),
        name="modulation",
    )(c_rows, w_mod, b_mod.reshape(1, n_out))


def _inproj_kernel(x_ref, sc_ref, sh_ref, g_ref, w_ref, u_ref, m_ref, gt_ref, edge_ref):
    h = _rms(x_ref[...], g_ref[...]) * (1.0 + sc_ref[0]) + sh_ref[0]
    hb = h.astype(BF16)
    u = _dot(hb, w_ref[:, :RWKV_COLS])
    u_ref[...] = u
    edge_ref[0, 0:1, :] = u[0:1]
    edge_ref[0, 1:2, :] = u[u.shape[0] - 1:]
    m_ref[...] = _dot(hb, w_ref[:, RWKV_COLS:RWKV_COLS + M_COLS])
    gt_ref[...] = _dot(hb, w_ref[:, RWKV_COLS + M_COLS:]).astype(gt_ref.dtype)


def _mod_spec(mod, blocks_per_batch):
    d = mod.shape[-1]
    if mod.shape[0] == 1:
        return pl.BlockSpec((1, 1, d), lambda i: (0, 0, 0))
    return pl.BlockSpec((1, 1, d), lambda i: (i // blocks_per_batch, 0, 0))


def _inproj(x2, sc, sh, g, w_in_p, tb, blocks_per_batch):
    n, d = x2.shape
    cols = w_in_p.shape[1]
    gate_cols = cols - RWKV_COLS - M_COLS
    return pl.pallas_call(
        _inproj_kernel,
        grid=(n // tb,),
        in_specs=[pl.BlockSpec((tb, d), lambda i: (i, 0)),
                  _mod_spec(sc, blocks_per_batch), _mod_spec(sh, blocks_per_batch),
                  pl.BlockSpec((1, d), lambda i: (0, 0)),
                  pl.BlockSpec((d, cols), lambda i: (0, 0))],
        out_specs=[pl.BlockSpec((tb, RWKV_COLS), lambda i: (i, 0)),
                   pl.BlockSpec((tb, M_COLS), lambda i: (i, 0)),
                   pl.BlockSpec((tb, gate_cols), lambda i: (i, 0)),
                   pl.BlockSpec((1, 2, RWKV_COLS), lambda i: (i, 0, 0))],
        out_shape=[jax.ShapeDtypeStruct((n, RWKV_COLS), F32),
                   jax.ShapeDtypeStruct((n, M_COLS), F32),
                   jax.ShapeDtypeStruct((n, gate_cols), BF16),
                   jax.ShapeDtypeStruct((n // tb, 2, RWKV_COLS), F32)],
        compiler_params=_cparams("arbitrary"),
        name="inproj",
    )(x2, sc, sh, g, w_in_p)


def _rwkv_prep_kernel(u_ref, hp_ref, hn_ref, sw_ref, w0_ref, a0_ref, wup_ref, aup_ref, gup_ref,
                      kk_ref, ka_ref, rk_ref, bd_ref,
                      r_o, v_o, nkk_o, kdf_o, kdb_o, bf_o, bb_o, lwf_o, lwb_o, g_o, bonus_o):
    u = u_ref[0]
    tb = u.shape[0]
    row = lax.broadcasted_iota(I32, u.shape, 0)
    prev = jnp.where(row == 0, hp_ref[0, 0], pltpu.roll(u, 1, 0))
    nxt = jnp.where(row == tb - 1, hn_ref[0, 0], pltpu.roll(u, tb - 1, 0))
    xs = sw_ref[0:1] * prev + sw_ref[1:2] * u + sw_ref[2:3] * nxt
    aw = A_WIDTH
    r = xs[:, 0:aw]
    k = xs[:, aw:2 * aw]
    v = xs[:, 2 * aw:3 * aw]
    o = 3 * aw
    dw = xs[:, o:o + 2 * DECAY_LORA]
    da = xs[:, o + 2 * DECAY_LORA:o + 2 * DECAY_LORA + 2 * ICLR_LORA]
    dg = xs[:, o + 2 * DECAY_LORA + 2 * ICLR_LORA:]
    bd = bd_ref[...]
    kkr = k * kk_ref[...]
    kk = kkr * lax.rsqrt(_dot((kkr * kkr).astype(BF16), bd) + 1e-12)
    tw = jnp.tanh(dw).astype(BF16)
    dab = da.astype(BF16)
    r_o[0] = r.astype(BF16)
    v_o[0] = v.astype(BF16)
    nkk_o[0] = (-kk).astype(BF16)
    for d, (lw_o, kd_o, b_o) in enumerate(((lwf_o, kdf_o, bf_o), (lwb_o, kdb_o, bb_o))):
        lw_o[0] = -DECAY_SCALE * jax.nn.sigmoid(w0_ref[d:d + 1] + _dot(tw, wup_ref[d]))
        a = jax.nn.sigmoid(a0_ref[d:d + 1] + _dot(dab, aup_ref[d]))
        kd_o[0] = (k * (1.0 + (a - 1.0) * ka_ref[...])).astype(BF16)
        b_o[0] = (kk * a).astype(BF16)
    g_o[0] = _dot(jax.nn.sigmoid(dg).astype(BF16), gup_ref[...]).astype(BF16)
    bonus_o[0] = (_dot_x2(r * k * rk_ref[...], bd) * v).astype(BF16)


def _rwkv_prep(u3, edges, shift_w, w0, a0, wup_p, aup_p, gup, k_k, k_a, r_k, bd, tb):
    b, t, cols = u3.shape
    nb = t // tb
    edges = edges.reshape(b, nb, 2, cols)
    zero = jnp.zeros((b, 1, cols), F32)
    halo_prev = jnp.concatenate([zero, edges[:, :nb - 1, 1]], axis=1).reshape(b, nb, 1, cols)
    halo_next = jnp.concatenate([edges[:, 1:, 0], zero], axis=1).reshape(b, nb, 1, cols)
    aw = A_WIDTH
    full2 = lambda s: pl.BlockSpec(s, lambda i, j: (0, 0))
    full3 = lambda s: pl.BlockSpec(s, lambda i, j: (0, 0, 0))
    out_spec = pl.BlockSpec((1, tb, aw), lambda i, j: (i, j, 0))
    sds = lambda dt: jax.ShapeDtypeStruct((b, t, aw), dt)
    out_dtypes = [BF16] * 7 + [F32] * 2 + [BF16] * 2
    return pl.pallas_call(
        _rwkv_prep_kernel,
        grid=(b, nb),
        in_specs=[pl.BlockSpec((1, tb, cols), lambda i, j: (i, j, 0)),
                  pl.BlockSpec((1, 1, 1, cols), lambda i, j: (i, j, 0, 0)),
                  pl.BlockSpec((1, 1, 1, cols), lambda i, j: (i, j, 0, 0)),
                  full2((3, cols)), full2((2, aw)), full2((2, aw)),
                  full3(wup_p.shape), full3(aup_p.shape), full2(gup.shape),
                  full2((1, aw)), full2((1, aw)), full2((1, aw)), full2((aw, aw))],
        out_specs=[out_spec] * 11,
        out_shape=[sds(dt) for dt in out_dtypes],
        compiler_params=_cparams("arbitrary", "arbitrary"),
        name="rwkv_prep",
    )(u3, halo_prev, halo_next, shift_w, w0, a0, wup_p, aup_p, gup, k_k, k_a, r_k, bd)


def _bdot(a, b):
    return lax.dot_general(a, b, (((2,), (1,)), ((0,), (0,))), preferred_element_type=F32)


def _bdot_nt(a, b):
    return lax.dot_general(a, b, (((2,), (2,)), ((0,), (0,))), preferred_element_type=F32)


def _bdot_tn(a, b):
    return lax.dot_general(a, b, (((1,), (1,)), ((0,), (0,))), preferred_element_type=F32)


def _split_pairs(x):
    return jnp.stack([x[g][:, p * LANES:(p + 1) * LANES]
                      for g in range(x.shape[0]) for p in range(x.shape[2] // LANES)], axis=0)


def _scan_kernel(r_ref, v_ref, nkk_ref, kd_ref, b_ref, lw_ref, s0_ref, o_ref, sf_ref, s_scr, *, reverse):
    c = pl.program_id(1)
    grp, ch, aw = lw_ref.shape
    n_pairs = aw // LANES
    state_shape = s_scr.shape

    @pl.when(c == 0)
    def _():
        s_scr[...] = s0_ref[...].reshape(state_shape)

    lw = lw_ref[...]
    ri = lax.broadcasted_iota(I32, (ch, ch), 0)
    ci = lax.broadcasted_iota(I32, (ch, ch), 1)
    tri = jnp.where((ci >= ri) if reverse else (ci <= ri), 1.0, 0.0).astype(BF16)
    hi = lw.astype(BF16)
    rem = lw - hi.astype(F32)
    mid = rem.astype(BF16)
    lo = (rem - mid.astype(F32)).astype(BF16)
    li = jnp.stack([_dot(tri, hi[g]) + (_dot(tri, mid[g]) + _dot(tri, lo[g])) for g in range(grp)], axis=0)
    lt = li[:, 0:1] if reverse else li[:, ch - 1:ch]
    rho = 0.5 * lt
    e1 = jnp.exp(li - rho)
    e2 = jnp.exp(rho - li)
    er = jnp.exp(rho)
    a_rel = nkk_ref[...] * (e1 * jnp.exp(-lw))
    r_rel = r_ref[...] * e1
    b_rel = b_ref[...] * e2
    k_rel = kd_ref[...] * e2
    pairs = lambda x: _split_pairs(x.astype(BF16))
    a_abs, r_abs, b_end, k_end = pairs(a_rel * er), pairs(r_rel * er), pairs(b_rel * er), pairs(k_rel * er)
    a_rel, r_rel, b_rel, k_rel = pairs(a_rel), pairs(r_rel), pairs(b_rel), pairs(k_rel)
    v = pairs(v_ref[...])
    decay = _split_pairs(jnp.exp(lt))
    lane = lax.broadcasted_iota(I32, (1, 1, LANES), 2)
    first = lane < A_HEAD_DIM

    def blockdiag(x):
        zero = jnp.zeros_like(x)
        return jnp.concatenate([jnp.where(first, x, zero), jnp.where(first, zero, x)], axis=1)

    ri2 = lax.broadcasted_iota(I32, (1, ch, 2 * ch), 1)
    ci2 = lax.broadcasted_iota(I32, (1, ch, 2 * ch), 2)
    ci2 = jnp.where(ci2 >= ch, ci2 - ch, ci2)
    if reverse:
        incl2, strict2 = ci2 >= ri2, ci2 > ri2
    else:
        incl2, strict2 = ci2 <= ri2, ci2 < ri2
    eye2 = jnp.where(ri2 == ci2, 1.0, 0.0)
    lhs = jnp.concatenate([a_rel, r_rel], axis=1)
    with_b = _bdot_nt(lhs, blockdiag(b_rel))
    with_k = _bdot_nt(lhs, blockdiag(k_rel))
    a_ab = jnp.where(strict2, with_b[:, :ch], 0.0)
    a_ak = jnp.where(strict2, with_k[:, :ch], 0.0).astype(BF16)
    a_r = jnp.concatenate([jnp.where(incl2, with_b[:, ch:], 0.0), jnp.where(incl2, with_k[:, ch:], 0.0)],
                          axis=2).astype(BF16)
    inv = eye2
    for k in range(int(math.log2(ch))):
        p, q = (ci2 >> k, ri2 >> k) if reverse else (ri2 >> k, ci2 >> k)
        joins = ((p ^ q) * 4 + (p - q)) == 5
        lk = jnp.where(joins, a_ab, 0.0)
        if k == 0:
            inv = inv + lk
        else:
            invb = inv.astype(BF16)
            inv = inv + _bdot(_bdot(invb, blockdiag(lk.astype(BF16))).astype(BF16), blockdiag(invb))
    s = s_scr[...]
    from_state = _bdot_nt(jnp.concatenate([a_abs, r_abs], axis=1), s.astype(BF16))
    v_bd = blockdiag(v)
    x = from_state[:, :ch] + _bdot(a_ak, v_bd)
    u = _bdot(inv.astype(BF16), blockdiag(x.astype(BF16))).astype(BF16)
    o = from_state[:, ch:] + _bdot(a_r, jnp.concatenate([blockdiag(u), v_bd], axis=1))
    si = lax.broadcasted_iota(I32, (1, LANES, LANES), 1) < A_HEAD_DIM
    same_head = si == (lax.broadcasted_iota(I32, (1, LANES, LANES), 2) < A_HEAD_DIM)
    upd = _bdot_tn(jnp.concatenate([u, v], axis=1), jnp.concatenate([b_end, k_end], axis=1))
    s_scr[...] = s * decay + jnp.where(same_head, upd, 0.0)
    for g in range(grp):
        o_ref[g] = jnp.concatenate([o[g * n_pairs + p] for p in range(n_pairs)], axis=1)

    @pl.when(c == pl.num_programs(1) - 1)
    def _():
        s_fin = s_scr[...]
        for g in range(grp):
            for p in range(n_pairs):
                pair = s_fin[g * n_pairs + p]
                sf_ref[g, 2 * p] = pair[:A_HEAD_DIM, :A_HEAD_DIM]
                sf_ref[g, 2 * p + 1] = pair[A_HEAD_DIM:, A_HEAD_DIM:]


def _wkv_scan(r, v, nkk, kd, b, lw, s0, reverse):
    bsz, t, aw = r.shape
    ch = SCAN_CHUNK
    assert t % ch == 0 and 2 * ch == LANES and 2 * A_HEAD_DIM == LANES
    nc = t // ch
    n_pairs = aw // LANES
    grp = SCAN_GROUP if bsz % SCAN_GROUP == 0 else 1
    tmap = (lambda i, c: (i, nc - 1 - c, 0)) if reverse else (lambda i, c: (i, c, 0))
    seq = pl.BlockSpec((grp, ch, aw), tmap)
    st_in = pl.BlockSpec((grp, n_pairs, LANES, LANES), lambda i, c: (i, 0, 0, 0))
    st_out = pl.BlockSpec((grp,) + s0.shape[1:], lambda i, c: (i, 0, 0, 0))
    return pl.pallas_call(
        functools.partial(_scan_kernel, reverse=reverse),
        grid=(bsz // grp, nc),
        in_specs=[seq] * 6 + [st_in],
        out_specs=[seq, st_out],
        out_shape=[jax.ShapeDtypeStruct((bsz, t, aw), F32), jax.ShapeDtypeStruct(s0.shape, F32)],
        scratch_shapes=[pltpu.VMEM((grp * n_pairs, LANES, LANES), F32)],
        compiler_params=_cparams("arbitrary", "arbitrary"),
        name="wkv_scan_bwd" if reverse else "wkv_scan_fwd",
    )(r, v, nkk, kd, b, lw, _pair_states(s0))


def _pair_states(s):
    b, h, d, _ = s.shape
    eye = jnp.eye(2, dtype=s.dtype)[None, None, :, None, :, None]
    return (s.reshape(b, h // 2, 2, d, 1, d) * eye).reshape(b, h // 2, 2 * d, 2 * d)


def _denominator_lane(width):
    lane = lax.broadcasted_iota(I32, (1, width), 1)
    return jnp.where((lane & (HEAD_PAD - 1)) == V_HEAD, 1.0, 0.0)


def _rope_lanes(x, cos_t, sin_up, sin_dn):
    w = x.shape[1]
    half = QK_ROPE // 2
    return x * cos_t + pltpu.roll(x, half, 1) * sin_up + pltpu.roll(x, w - half, 1) * sin_dn


def _mla_prep_kernel(*refs, rope):
    if rope:
        (m_ref, gq_ref, gkv_ref, wuq_ref, wk_ref, wv_ref, p_ref,
         qc_ref, qu_ref, qd_ref, kc_ref, ku_ref, kd_ref, q_o, k_o, v_o, ckv_o) = refs
    else:
        (m_ref, gq_ref, gkv_ref, wuq_ref, wk_ref, wv_ref, p_ref, q_o, k_o, v_o, ckv_o) = refs
    m = m_ref[...]
    qn = _rms(m[:, :Q_LORA], gq_ref[...])
    ckv = _rms(m[:, Q_LORA:Q_LORA + KV_LORA], gkv_ref[...])
    kp = m[:, Q_LORA + KV_LORA:]
    q = _dot(qn.astype(BF16), wuq_ref[...]) * ATTN_SCALE
    if rope:
        tile = lambda ref: jnp.concatenate([ref[...]] * B_HEADS, axis=1)
        q = _rope_lanes(q, tile(qc_ref), tile(qu_ref), tile(qd_ref))
        kp = _rope_lanes(kp, kc_ref[...], ku_ref[...], kd_ref[...])
    cb = ckv.astype(BF16)
    q_o[...] = q.astype(BF16)
    k_o[...] = (_dot(cb, wk_ref[...]) + _dot(kp.astype(BF16), p_ref[...])).astype(BF16)
    v_o[...] = (_dot(cb, wv_ref[...]) + _denominator_lane(v_o.shape[1])).astype(BF16)
    ckv_o[...] = ckv


def _mla_prep(m2, g_q, g_kv, wuq_p, wk_p, wv_p, place, rope_tabs, tb, t):
    n = m2.shape[0]
    hw = B_HEADS * HEAD_PAD
    full = lambda a: pl.BlockSpec(a.shape, lambda i: (0, 0))
    ins = [m2, g_q, g_kv, wuq_p, wk_p, wv_p, place]
    specs = [pl.BlockSpec((tb, M_COLS), lambda i: (i, 0))] + [full(a) for a in ins[1:]]
    if rope_tabs is not None:
        nbt = t // tb
        ins += list(rope_tabs)
        specs += [pl.BlockSpec((tb, HEAD_PAD), lambda i: (i % nbt, 0))] * 6
    big = pl.BlockSpec((tb, hw), lambda i: (i, 0))
    return pl.pallas_call(
        functools.partial(_mla_prep_kernel, rope=rope_tabs is not None),
        grid=(n // tb,),
        in_specs=specs,
        out_specs=[big, big, big, pl.BlockSpec((tb, KV_LORA), lambda i: (i, 0))],
        out_shape=[jax.ShapeDtypeStruct((n, hw), BF16)] * 3 + [jax.ShapeDtypeStruct((n, KV_LORA), F32)],
        compiler_params=_cparams("arbitrary"),
        name="mla_prep",
    )(*ins)


def _kv_up_kernel(ckv_ref, kp_ref, wk_ref, wv_ref, p_ref, k_o, v_o):
    cb = ckv_ref[...].astype(BF16)
    k_o[...] = (_dot(cb, wk_ref[...]) + _dot(kp_ref[...].astype(BF16), p_ref[...])).astype(BF16)
    v_o[...] = (_dot(cb, wv_ref[...]) + _denominator_lane(v_o.shape[1])).astype(BF16)


def _kv_up(ckv2, kpe_pad, wk_p, wv_p, place, tb):
    n = ckv2.shape[0]
    hw = B_HEADS * HEAD_PAD
    full = lambda a: pl.BlockSpec(a.shape, lambda i: (0, 0))
    big = pl.BlockSpec((tb, hw), lambda i: (i, 0))
    return pl.pallas_call(
        _kv_up_kernel,
        grid=(n // tb,),
        in_specs=[pl.BlockSpec((tb, KV_LORA), lambda i: (i, 0)),
                  pl.BlockSpec((tb, HEAD_PAD), lambda i: (i, 0)),
                  full(wk_p), full(wv_p), full(place)],
        out_specs=[big, big],
        out_shape=[jax.ShapeDtypeStruct((n, hw), BF16)] * 2,
        compiler_params=_cparams("arbitrary"),
        name="kv_up",
    )(ckv2, kpe_pad, wk_p, wv_p, place)


def _attn_kernel(*refs, two):
    if two:
        q_ref, k1_ref, v1_ref, k2_ref, v2_ref, o_ref = refs
    else:
        q_ref, k1_ref, v1_ref, o_ref = refs
    for h in range(B_HEADS):
        hs = slice(h * HEAD_PAD, (h + 1) * HEAD_PAD)
        q = q_ref[0, :, hs]
        s1 = _dot_nt(q, k1_ref[0, :, hs])
        mx = jnp.max(s1, axis=-1, keepdims=True)
        if two:
            s2 = _dot_nt(q, k2_ref[0, :, hs])
            mx = jnp.maximum(mx, jnp.max(s2, axis=-1, keepdims=True))
        acc = _dot(jnp.exp(s1 - mx).astype(BF16), v1_ref[0, :, hs])
        if two:
            acc = acc + _dot(jnp.exp(s2 - mx).astype(BF16), v2_ref[0, :, hs])
        o_ref[0, :, hs] = (acc / acc[:, V_HEAD:V_HEAD + 1]).astype(BF16)


def _attention(q3, k1, v1, k2, v2, tq):
    b, t, hw = q3.shape
    two = k2 is not None
    qspec = pl.BlockSpec((1, tq, hw), lambda i, j: (i, j, 0))
    kv = lambda a: pl.BlockSpec((1, a.shape[1], hw), lambda i, j: (i, 0, 0))
    ins = [q3, k1, v1] + ([k2, v2] if two else [])
    return pl.pallas_call(
        functools.partial(_attn_kernel, two=two),
        grid=(b, t // tq),
        in_specs=[qspec] + [kv(a) for a in ins[1:]],
        out_specs=qspec,
        out_shape=jax.ShapeDtypeStruct((b, t, hw), BF16),
        compiler_params=_cparams("arbitrary", "arbitrary"),
        name="attention",
    )(*ins)


def _post_mix_kernel(of_ref, ob_ref, bonus_ref, g_ref, att_ref, gates_ref, x_ref,
                     g1_ref, sc2_ref, sh2_ref, lnw_ref, lnb_ref, bd_ref, wa_ref, wb_ref, wo_ref,
                     gpost_ref, gpre_ref, wr_ref, x1_o, h2_o, aff_o):
    bd = bd_ref[...]
    inv_n = 1.0 / A_HEAD_DIM
    o = of_ref[...] + ob_ref[...]
    mu = _dot_x2(o, bd) * inv_n
    oc = o - mu
    var = _dot((oc * oc).astype(BF16), bd) * inv_n
    on = (oc * lax.rsqrt(var + GN_EPS)) * lnw_ref[...] + lnb_ref[...]
    ya = _dot(((on + bonus_ref[...]) * g_ref[...]).astype(BF16), wa_ref[...])
    yb = _dot(att_ref[...], wb_ref[...])
    d = ya.shape[1]
    gates = gates_ref[...].astype(F32)
    mix = jax.nn.sigmoid(gates[:, :d]) * ya + jax.nn.sigmoid(gates[:, d:]) * yb
    y = _dot(mix.astype(BF16), wo_ref[...])
    x1 = x_ref[...] + g1_ref[0] * _rms(y, gpost_ref[...])
    x1_o[...] = x1
    h2 = _rms(x1, gpre_ref[...]) * (1.0 + sc2_ref[0]) + sh2_ref[0]
    h2_o[...] = h2.astype(BF16)
    hh, hl = _split2(h2)
    wh, wl = _split2(wr_ref[...])
    logits = _dot_nt(wh, hh) + (_dot_nt(wh, hl) + _dot_nt(wl, hh))
    z = jnp.exp(logits - jnp.max(logits, axis=0, keepdims=True))
    aff_o[...] = z / jnp.sum(z, axis=0, keepdims=True)


def _post_mix(o_f, o_b, bonus, g, att, gates, x2, g1, sc2, sh2, ln_w, ln_b, bd, wa, wb, wo,
              g_post, g_pre, w_router_t, tb, blocks_per_batch):
    n, d = x2.shape
    tok = lambda a: pl.BlockSpec((tb, a.shape[1]), lambda i: (i, 0))
    full = lambda a: pl.BlockSpec(a.shape, lambda i: (0, 0))
    ms = lambda a: _mod_spec(a, blocks_per_batch)
    return pl.pallas_call(
        _post_mix_kernel,
        grid=(n // tb,),
        in_specs=[tok(o_f), tok(o_b), tok(bonus), tok(g), tok(att), tok(gates), tok(x2),
                  ms(g1), ms(sc2), ms(sh2), full(ln_w), full(ln_b), full(bd), full(wa), full(wb),
                  full(wo), full(g_post), full(g_pre), full(w_router_t)],
        out_specs=[pl.BlockSpec((tb, d), lambda i: (i, 0)),
                   pl.BlockSpec((tb, d), lambda i: (i, 0)),
                   pl.BlockSpec((N_EXPERTS, tb), lambda i: (0, i))],
        out_shape=[jax.ShapeDtypeStruct((n, d), F32),
                   jax.ShapeDtypeStruct((n, d), BF16),
                   jax.ShapeDtypeStruct((N_EXPERTS, n), F32)],
        compiler_params=_cparams("arbitrary"),
        name="post_mix",
    )(o_f, o_b, bonus, g, att, gates, x2, g1, sc2, sh2, ln_w, ln_b, bd, wa, wb, wo,
      g_post, g_pre, w_router_t)


def _select_kernel(a_ref, pos_ref, sel_ref, *, cap, n_chunks):
    a = a_ref[...]
    rows = a.shape[0]
    a3 = a.reshape(N_EXPERTS, n_chunks, LANES)

    def count(mask):
        c = jnp.sum(jnp.where(mask, 1.0, 0.0), axis=2, keepdims=True)
        return jnp.sum(c, axis=1, keepdims=True)

    def body(i, thr):
        cand = thr | jnp.left_shift(jnp.int32(1), 30 - i)
        return jnp.where(count(a3 >= pltpu.bitcast(cand, F32)) >= cap, cand, thr)

    thr = pltpu.bitcast(lax.fori_loop(0, 31, body, jnp.zeros((N_EXPERTS, 1, LANES), I32)), F32)
    gt = a3 > thr
    eq = a3 == thr
    need = cap - count(gt)

    li = lax.broadcasted_iota(I32, (LANES, LANES), 0)
    lj = lax.broadcasted_iota(I32, (LANES, LANES), 1)
    lane_before = jnp.where(li < lj, 1.0, 0.0).astype(BF16)
    lane_all = jnp.ones((LANES, LANES), BF16)
    ci = lax.broadcasted_iota(I32, (n_chunks, n_chunks), 0)
    cj = lax.broadcasted_iota(I32, (n_chunks, n_chunks), 1)
    chunk_before = jnp.where(cj < ci, 1.0, 0.0).astype(BF16)

    def prefix(flags3):
        f2 = flags3.reshape(rows, LANES).astype(BF16)
        within = _dot(f2, lane_before).reshape(N_EXPERTS, n_chunks, LANES)
        tot = _dot(f2, lane_all).astype(BF16).reshape(N_EXPERTS, n_chunks, LANES)
        offs = [_dot(chunk_before, tot[e]) for e in range(N_EXPERTS)]
        return within + jnp.stack(offs, axis=0)

    eqf = jnp.where(eq, 1.0, 0.0)
    tie_ok = jnp.where(prefix(eqf) < need, eqf, 0.0)
    sel = jnp.where(gt, 1.0, tie_ok)
    pos_ref[...] = prefix(sel).reshape(rows, LANES).astype(I32)
    sel_ref[...] = sel.reshape(rows, LANES).astype(I32)


def _select(aff_t, cap):
    e, n = aff_t.shape
    n_chunks = n // LANES
    rows = e * n_chunks
    a2 = aff_t.reshape(rows, LANES)
    spec = pl.BlockSpec((rows, LANES), lambda i: (0, 0))
    pos, sel = pl.pallas_call(
        functools.partial(_select_kernel, cap=cap, n_chunks=n_chunks),
        grid=(1,),
        in_specs=[spec],
        out_specs=[spec, spec],
        out_shape=[jax.ShapeDtypeStruct((rows, LANES), I32)] * 2,
        compiler_params=_cparams("arbitrary"),
        name="expert_select",
    )(a2)
    return pos.reshape(e, n), sel.reshape(e, n)


def _ffn_kernel(lo_ref, hi_ref, slot_ref, aff_ref, h_ref, wg_ref, wu_ref, wd_ref, y_ref, acc_ref, wacc_ref,
                *, n_tiles):
    e = pl.program_id(0)
    j = pl.program_id(1)
    rows = acc_ref.shape[0]
    tb = slot_ref.shape[2]
    acc_ref[...] = jnp.zeros_like(acc_ref)
    wacc_ref[...] = jnp.zeros_like(wacc_ref)
    r_id = lax.broadcasted_iota(I32, (rows, tb), 0) + j * rows

    def gather_block(b, carry):
        hit = r_id == slot_ref[0, pl.ds(b, 1), :]
        tokens = h_ref[pl.ds(pl.multiple_of(b * tb, tb), tb), :]
        acc_ref[...] += _dot(jnp.where(hit, 1.0, 0.0).astype(BF16), tokens)
        wacc_ref[...] += jnp.sum(jnp.where(hit, aff_ref[0, pl.ds(b, 1), :], 0.0), axis=1, keepdims=True)
        return carry

    lax.fori_loop(lo_ref[e * n_tiles + j], hi_ref[e * n_tiles + j], gather_block, 0)
    xe = acc_ref[...].astype(BF16)
    gate = _dot(xe, wg_ref[0])
    up = _dot(xe, wu_ref[0])
    hid = (gate * jax.nn.sigmoid(gate)) * up
    y_ref[0] = (_dot(hid.astype(BF16), wd_ref[0]) * wacc_ref[...]).astype(BF16)


def _expert_ffn(pos, slot, aff_t, h2, wg, wu, wd, cap, tb, rt):
    n, d = h2.shape
    nb = n // tb
    e, _, f = wg.shape
    n_tiles = cap // rt
    starts = jnp.concatenate([pos[:, ::tb], jnp.full((e, 1), cap, I32)], axis=1)
    edges = jnp.arange(n_tiles, dtype=I32) * rt
    lo = jnp.sum((starts[:, None, 1:] <= edges[None, :, None]).astype(I32), axis=2)
    hi = jnp.sum((starts[:, None, :-1] < (edges + rt)[None, :, None]).astype(I32), axis=2)
    grid_spec = pltpu.PrefetchScalarGridSpec(
        num_scalar_prefetch=2,
        grid=(e, n_tiles),
        in_specs=[pl.BlockSpec((1, nb, tb), lambda ei, j, lo, hi: (ei, 0, 0)),
                  pl.BlockSpec((1, nb, tb), lambda ei, j, lo, hi: (ei, 0, 0)),
                  pl.BlockSpec((n, d), lambda ei, j, lo, hi: (0, 0), pipeline_mode=pl.Buffered(1)),
                  pl.BlockSpec((1, d, f), lambda ei, j, lo, hi: (ei, 0, 0)),
                  pl.BlockSpec((1, d, f), lambda ei, j, lo, hi: (ei, 0, 0)),
                  pl.BlockSpec((1, f, d), lambda ei, j, lo, hi: (ei, 0, 0))],
        out_specs=pl.BlockSpec((1, rt, d), lambda ei, j, lo, hi: (ei, j, 0)),
        scratch_shapes=[pltpu.VMEM((rt, d), F32), pltpu.VMEM((rt, 1), F32)],
    )
    return pl.pallas_call(
        functools.partial(_ffn_kernel, n_tiles=n_tiles),
        grid_spec=grid_spec,
        out_shape=jax.ShapeDtypeStruct((e, cap, d), BF16),
        compiler_params=_cparams("arbitrary", "arbitrary"),
        name="expert_ffn",
    )(lo.reshape(-1), hi.reshape(-1), slot.reshape(e, nb, tb), aff_t.reshape(e, nb, tb), h2, wg, wu, wd)


def _combine_kernel(t0_ref, fetch_ref, need_ref, slot_ref, x_ref, g2_ref, gp_ref, *rest, n_win):
    y_refs, o_ref, acc_ref = rest[:-2], rest[-2], rest[-1]
    blk = pl.program_id(0)
    n_blk = pl.num_programs(0)
    tb = slot_ref.shape[0]
    rows = y_refs[0].shape[1]
    lane = lax.broadcasted_iota(I32, (tb, rows), 1)

    def picked(e, k):
        slot = slot_ref[:, e:e + 1] - (t0_ref[blk * N_EXPERTS + e] + k) * rows
        return _dot(jnp.where(lane == slot, 1.0, 0.0).astype(BF16), y_refs[e * n_win + k][0])

    acc = picked(0, 0)
    for e in range(1, N_EXPERTS):
        acc = acc + picked(e, 0)
    acc_ref[...] = acc
    for k in range(1, n_win):
        for e in range(N_EXPERTS):
            @pl.when(need_ref[(k * n_blk + blk) * N_EXPERTS + e] != 0)
            def _():
                acc_ref[...] += picked(e, k)
    o_ref[...] = x_ref[...] + g2_ref[0] * _rms(acc_ref[...], gp_ref[...])


def _combine_windows(pos, tb, rt, cap):
    n_tiles = cap // rt
    lo = pos[:, ::tb]
    hi = jnp.concatenate([lo[:, 1:], jnp.full((lo.shape[0], 1), cap, I32)], axis=1)
    t0 = jnp.minimum(lo // rt, n_tiles - 1)
    t_last = jnp.where(hi > lo, (hi - 1) // rt, t0)
    fetch, need = [t0], [jnp.ones_like(t0)]
    for k in range(1, tb // rt + 1):
        need.append(t_last >= t0 + k)
        fetch.append(lax.cummax(jnp.where(need[k], t0 + k, 0), axis=1))
    flat = lambda a: a.T.reshape(-1).astype(I32)
    return flat(t0), jnp.concatenate([flat(f) for f in fetch]), jnp.concatenate([flat(f) for f in need])


def _combine(windows, slot_t, ye, x1, g2, g_post, tb, rt, blocks_per_batch):
    t0, fetch, need = windows
    n, d = x1.shape
    e, cap, _ = ye.shape
    nb = n // tb
    n_win = tb // rt + 1
    if g2.shape[0] == 1:
        g2_spec = pl.BlockSpec((1, 1, d), lambda b, t, f, nd: (0, 0, 0))
    else:
        g2_spec = pl.BlockSpec((1, 1, d), lambda b, t, f, nd: (b // blocks_per_batch, 0, 0))

    def window(ei, k):
        return pl.BlockSpec((1, rt, d), lambda b, t, f, nd: (ei, f[(k * nb + b) * e + ei], 0))

    grid_spec = pltpu.PrefetchScalarGridSpec(
        num_scalar_prefetch=3,
        grid=(nb,),
        in_specs=[pl.BlockSpec((tb, e), lambda b, t, f, nd: (b, 0)),
                  pl.BlockSpec((tb, d), lambda b, t, f, nd: (b, 0)),
                  g2_spec,
                  pl.BlockSpec((1, d), lambda b, t, f, nd: (0, 0))]
                 + [window(ei, k) for ei in range(e) for k in range(n_win)],
        out_specs=pl.BlockSpec((tb, d), lambda b, t, f, nd: (b, 0)),
        scratch_shapes=[pltpu.VMEM((tb, d), F32)],
    )
    return pl.pallas_call(
        functools.partial(_combine_kernel, n_win=n_win),
        grid_spec=grid_spec,
        out_shape=jax.ShapeDtypeStruct((n, d), F32),
        compiler_params=_cparams("arbitrary"),
        name="moe_combine",
    )(t0, fetch, need, slot_t, x1, g2, g_post, *([ye] * (e * n_win)))


def _rope_tables(t):
    half = QK_ROPE // 2
    pos = jnp.arange(t)
    row = (pos // GRID_W).astype(F32)
    col = (pos % GRID_W).astype(F32)
    inv = ROPE_BASE ** (-jnp.arange(0, half, 2, dtype=F32) / half)
    ang = jnp.concatenate([row[:, None] * inv, col[:, None] * inv], axis=-1)
    cos, sin = jnp.cos(ang), jnp.sin(ang)

    def tabs(first_lane):
        c = jnp.ones((t, HEAD_PAD), F32)
        c = c.at[:, first_lane:first_lane + half].set(cos).at[:, first_lane + half:first_lane + 2 * half].set(cos)
        up = jnp.zeros((t, HEAD_PAD), F32).at[:, first_lane + half:first_lane + 2 * half].set(sin)
        dn = jnp.zeros((t, HEAD_PAD), F32).at[:, first_lane:first_lane + half].set(-sin)
        return c, up, dn

    return tabs(QK_NOPE + QK_ROPE) + tabs(QK_ROPE)


def _layout_weights(w_in, w_uq, w_ukv, w_branch_b, w_up, a_up):
    d = w_in.shape[0]
    deint = jnp.concatenate([jnp.arange(0, QK_ROPE, 2), jnp.arange(1, QK_ROPE, 2)])
    kpe0 = RWKV_COLS + Q_LORA + KV_LORA
    kpe_cols = w_in[:, kpe0:kpe0 + QK_ROPE]
    w_in_p = jnp.concatenate(
        [w_in[:, :kpe0 + QK_ROPE], kpe_cols[:, deint],
         jnp.zeros((d, M_COLS - MLA_COLS - QK_ROPE), F32), w_in[:, RWKV_COLS + MLA_COLS:]], axis=1)
    uq = w_uq.reshape(Q_LORA, B_HEADS, QK_NOPE + QK_ROPE)
    wuq_p = jnp.concatenate([uq, uq[:, :, QK_NOPE:][:, :, deint]], axis=2).reshape(Q_LORA, B_HEADS * HEAD_PAD)
    ukv = w_ukv.reshape(KV_LORA, B_HEADS, QK_NOPE + V_HEAD)
    zpad = jnp.zeros((KV_LORA, B_HEADS, HEAD_PAD - QK_NOPE), F32)
    wk_p = jnp.concatenate([ukv[:, :, :QK_NOPE], zpad], axis=2).reshape(KV_LORA, B_HEADS * HEAD_PAD)
    wv_p = jnp.concatenate([ukv[:, :, QK_NOPE:], zpad], axis=2).reshape(KV_LORA, B_HEADS * HEAD_PAD)
    wb = w_branch_b.reshape(B_HEADS, V_HEAD, d)
    wb_p = jnp.concatenate([wb, jnp.zeros((B_HEADS, HEAD_PAD - V_HEAD, d), F32)], axis=1).reshape(B_HEADS * HEAD_PAD, d)
    eye = jnp.eye(QK_ROPE, dtype=F32)
    z = jnp.zeros((QK_ROPE, QK_ROPE), F32)
    head_raw = jnp.concatenate([jnp.zeros((QK_ROPE, QK_NOPE), F32), eye, z], axis=1)
    head_rot = jnp.concatenate([jnp.zeros((QK_ROPE, QK_NOPE), F32), z, eye], axis=1)
    zrows = jnp.zeros((HEAD_PAD - 2 * QK_ROPE, B_HEADS * HEAD_PAD), F32)
    zr = jnp.zeros((QK_ROPE, B_HEADS * HEAD_PAD), F32)
    place_raw = jnp.concatenate([jnp.tile(head_raw, (1, B_HEADS)), zr, zrows], axis=0)
    place_rot = jnp.concatenate([zr, jnp.tile(head_rot, (1, B_HEADS)), zrows], axis=0)
    zl = jnp.zeros((DECAY_LORA, A_WIDTH), F32)
    wup_p = jnp.stack([jnp.concatenate([w_up[0], zl]), jnp.concatenate([zl, w_up[1]])])
    aup_p = jnp.stack([jnp.concatenate([a_up[0], zl]), jnp.concatenate([zl, a_up[1]])])
    bf = lambda a: a.astype(BF16)
    return (bf(w_in_p), bf(wuq_p), bf(wk_p), bf(wv_p), bf(wb_p), bf(place_raw), bf(place_rot),
            bf(wup_p), bf(aup_p))


def _group_layer(x, mods, lw, s0_f, s0_b, ctx_kv, rope_tabs):
    bsz, t, d = x.shape
    n = bsz * t
    tb = min(TOKEN_BLOCK, t)
    bpb = t // tb
    sh1, sc1, g1, sh2, sc2, g2 = mods
    x2 = x.reshape(n, d)
    u, m, gates, edges = _inproj(x2, sc1, sh1, lw["g_pre_mix"], lw["w_in_p"], tb, bpb)
    prep = _rwkv_prep(u.reshape(bsz, t, RWKV_COLS), edges, lw["shift_w"], lw["w0"], lw["a0"], lw["wup_p"],
                      lw["aup_p"], lw["g_up"], lw["k_k"], lw["k_a"], lw["r_k"], lw["bd"], tb)
    r, v, nkk, kd_f, kd_b, b_f, b_b, lw_f, lw_b, g, bonus = prep
    o_f, s_f = _wkv_scan(r, v, nkk, kd_f, b_f, lw_f, s0_f, reverse=False)
    o_b, s_b = _wkv_scan(r, v, nkk, kd_b, b_b, lw_b, s0_b, reverse=True)
    place = lw["place_raw"] if rope_tabs is None else lw["place_rot"]
    q, k, vv, ckv = _mla_prep(m, lw["g_qnorm"], lw["g_kvnorm"], lw["wuq_p"], lw["wk_p"], lw["wv_p"],
                              place, rope_tabs, tb, t)
    hw = B_HEADS * HEAD_PAD
    k2, v2 = ctx_kv if ctx_kv is not None else (None, None)
    att = _attention(q.reshape(bsz, t, hw), k.reshape(bsz, t, hw), vv.reshape(bsz, t, hw), k2, v2, tb)
    two = lambda a: a.reshape(n, a.shape[-1])
    x1, h2, aff_t = _post_mix(two(o_f), two(o_b), two(bonus), two(g), att.reshape(n, hw), gates, x2,
                              g1, sc2, sh2, lw["ln_x_w"], lw["ln_x_b"], lw["bd"], lw["w_branch_a"],
                              lw["wb_p"], lw["w_out"], lw["g_post_mix"], lw["g_pre_ffn"],
                              lw["w_router_t"], tb, bpb)
    cap = CAPACITY_FACTOR * n // N_EXPERTS
    rt = min(EXPERT_TILE, cap)
    pos, sel = _select(aff_t, cap)
    slot = jnp.where(sel > 0, pos, -1)
    ye = _expert_ffn(pos, slot, aff_t, h2, lw["w_exp_gate"], lw["w_exp_up"], lw["w_exp_down"], cap,
                     min(GATHER_BLOCK, n), rt)
    out = _combine(_combine_windows(pos, tb, rt, cap), slot.T, ye, x1, g2, lw["g_post_ffn"], tb, rt, bpb)
    kpe = m[:, Q_LORA + KV_LORA:Q_LORA + KV_LORA + QK_ROPE]
    return out.reshape(bsz, t, d), (ckv.reshape(bsz, t, KV_LORA), kpe.reshape(bsz, t, QK_ROPE), s_f, s_b)


def kernel(x_prompt, x_sample, cache_ckv, cache_kpe, state_wkv_fwd, state_wkv_bwd, c, c_ctx,
           w_mod, b_mod, g_pre_mix, g_post_mix, g_pre_ffn, g_post_ffn, w_in, shift_w,
           w0, w_up, a0, a_up, g_up, k_k, k_a, r_k, ln_x_w, ln_x_b, w_branch_a,
           g_qnorm, w_uq, g_kvnorm, w_ukv, w_branch_b, w_out,
           w_router, w_exp_gate, w_exp_up, w_exp_down):
    depth = w_mod.shape[0]
    d = x_prompt.shape[-1]
    dec_b, dec_t = x_sample.shape[0], x_sample.shape[1]
    xp, xs = x_prompt, x_sample
    c_rows = jnp.concatenate([c, c_ctx[None, :],
                              jnp.zeros((-(dec_b + 1) % 8, d), F32)], axis=0)
    rope_tabs = _rope_tables(dec_t)
    ii = lax.broadcasted_iota(I32, (A_WIDTH, A_WIDTH), 0) // A_HEAD_DIM
    jj = lax.broadcasted_iota(I32, (A_WIDTH, A_WIDTH), 1) // A_HEAD_DIM
    bd = (ii == jj).astype(BF16)
    row = lambda a: a.reshape(1, -1)
    bf = lambda a: a.astype(BF16)
    ckv_l, kpe_l, sf_l, sb_l = [], [], [], []
    for l in range(depth):
        (w_in_p, wuq_p, wk_p, wv_p, wb_p, place_raw, place_rot, wup_p, aup_p) = _layout_weights(
            w_in[l], w_uq[l], w_ukv[l], w_branch_b[l], w_up[l], a_up[l])
        lw = {
            "g_pre_mix": row(g_pre_mix[l]), "g_post_mix": row(g_post_mix[l]),
            "g_pre_ffn": row(g_pre_ffn[l]), "g_post_ffn": row(g_post_ffn[l]),
            "w_in_p": w_in_p, "shift_w": shift_w[l], "w0": w0[l], "a0": a0[l],
            "wup_p": wup_p, "aup_p": aup_p, "g_up": bf(g_up[l]),
            "k_k": row(k_k[l]), "k_a": row(k_a[l]), "r_k": row(r_k[l]), "bd": bd,
            "ln_x_w": row(ln_x_w[l]), "ln_x_b": row(ln_x_b[l]), "w_branch_a": bf(w_branch_a[l]),
            "g_qnorm": row(g_qnorm[l]), "g_kvnorm": row(g_kvnorm[l]),
            "wuq_p": wuq_p, "wk_p": wk_p, "wv_p": wv_p, "wb_p": wb_p,
            "place_raw": place_raw, "place_rot": place_rot,
            "w_out": bf(w_out[l]), "w_router_t": w_router[l].T,
            "w_exp_gate": bf(w_exp_gate[l]), "w_exp_up": bf(w_exp_up[l]), "w_exp_down": bf(w_exp_down[l]),
        }
        mod = _modulation(c_rows, w_mod[l], b_mod[l])
        mods_lat = [mod[:dec_b, i * d:(i + 1) * d].reshape(dec_b, 1, d) for i in range(6)]
        mods_ctx = [mod[dec_b:dec_b + 1, i * d:(i + 1) * d].reshape(1, 1, d) for i in range(6)]
        zeros_state = jnp.zeros((xp.shape[0], A_HEADS, A_HEAD_DIM, A_HEAD_DIM), F32)
        xp, (ckv, kpe, s_f, s_b) = _group_layer(xp, mods_ctx, lw, zeros_state, zeros_state, None, None)
        ckv_l.append(ckv)
        kpe_l.append(kpe)
        sf_l.append(s_f)
        sb_l.append(s_b)
        past = cache_ckv.shape[2]
        kpe_pad = jnp.concatenate(
            [cache_kpe[:, l], jnp.zeros((dec_b, past, HEAD_PAD - QK_ROPE), F32)], axis=-1)
        k_ctx, v_ctx = _kv_up(cache_ckv[:, l].reshape(dec_b * past, KV_LORA),
                              kpe_pad.reshape(dec_b * past, HEAD_PAD), wk_p, wv_p, place_raw,
                              min(TOKEN_BLOCK, past))
        hw = B_HEADS * HEAD_PAD
        ctx_kv = (k_ctx.reshape(dec_b, past, hw), v_ctx.reshape(dec_b, past, hw))
        xs, _ = _group_layer(xs, mods_lat, lw, state_wkv_fwd[:, l], state_wkv_bwd[:, l], ctx_kv, rope_tabs)
    return (xp, xs, jnp.stack(ckv_l, axis=1), jnp.stack(kpe_l, axis=1),
            jnp.stack(sf_l, axis=1), jnp.stack(sb_l, axis=1))
```

```python
import functools
import math

import jax
import jax.numpy as jnp
from jax import lax
from jax.experimental import pallas as pl
from jax.experimental.pallas import tpu as pltpu

F32 = jnp.float32
BF16 = jnp.bfloat16
I32 = jnp.int32

GRID_W = 64
A_HEADS = 8
A_HEAD_DIM = 64
A_WIDTH = A_HEADS * A_HEAD_DIM
DECAY_LORA = 64
ICLR_LORA = 64
GATE_LORA = 128
DECAY_SCALE = 0.6065306597126334
GN_EPS = 64e-5
B_HEADS = 8
Q_LORA = 256
KV_LORA = 128
QK_NOPE = 64
QK_ROPE = 32
V_HEAD = 64
ROPE_BASE = 10000.0
ATTN_SCALE = 1.0 / math.sqrt(QK_NOPE + QK_ROPE)
N_EXPERTS = 16
CAPACITY_FACTOR = 2
EPS = 1e-6
RWKV_COLS = 3 * A_WIDTH + 2 * DECAY_LORA + 2 * ICLR_LORA + GATE_LORA
MLA_COLS = Q_LORA + KV_LORA + QK_ROPE

LANES = 128
HEAD_PAD = 128
M_COLS = 512
VMEM_LIMIT = 56 * 1024 * 1024

SCAN_CHUNK = 64
SCAN_GROUP = 8
TOKEN_BLOCK = 256
ROW_BLOCK = 512
EXPERT_TILE = 256
COMBINE_TILE = 128
GATHER_BLOCK = 512
GATHER_ROWS = 128


def _cparams(*sem):
    return pltpu.CompilerParams(dimension_semantics=sem, vmem_limit_bytes=VMEM_LIMIT)


def _dot(a, b):
    return jnp.dot(a, b, preferred_element_type=F32)


def _dot_nt(a, b):
    return lax.dot_general(a, b, (((1,), (1,)), ((), ())), preferred_element_type=F32)


def _dot_tn(a, b):
    return lax.dot_general(a, b, (((0,), (0,)), ((), ())), preferred_element_type=F32)


def _split2(x):
    hi = x.astype(BF16)
    lo = (x - hi.astype(F32)).astype(BF16)
    return hi, lo


def _dot_x2(a, b_bf16):
    hi, lo = _split2(a)
    return _dot(hi, b_bf16) + _dot(lo, b_bf16)


def _dot_f32(a, b):
    ah, al = _split2(a)
    bh, bl = _split2(b)
    return _dot(ah, bh) + (_dot(ah, bl) + _dot(al, bh))


def _rms(x, g):
    return (x * lax.rsqrt(jnp.mean(x * x, axis=-1, keepdims=True) + EPS)) * g


def _mod_kernel(c_ref, w_ref, b_ref, o_ref):
    c = c_ref[...]
    s = c * jax.nn.sigmoid(c)
    o_ref[...] = _dot_f32(s, w_ref[...]) + b_ref[...]


def _modulation(c_rows, w_mod, b_mod):
    rows, d = c_rows.shape
    n_out = w_mod.shape[1]
    tn = n_out // 8
    return pl.pallas_call(
        _mod_kernel,
        grid=(n_out // tn,),
        in_specs=[pl.BlockSpec((rows, d), lambda j: (0, 0)),
                  pl.BlockSpec((d, tn), lambda j: (0, j)),
                  pl.BlockSpec((1, tn), lambda j: (0, j))],
        out_specs=pl.BlockSpec((rows, tn), lambda j: (0, j)),
        out_shape=jax.ShapeDtypeStruct((rows, n_out), F32),
        compiler_params=_cparams("arbitrary"),
        name="modulation",
    )(c_rows, w_mod, b_mod.reshape(1, n_out))


def _inproj_kernel(x_ref, sc_ref, sh_ref, g_ref, w_ref, u_ref, m_ref, gt_ref, edge_ref):
    h = _rms(x_ref[...], g_ref[...]) * (1.0 + sc_ref[0]) + sh_ref[0]
    hb = h.astype(BF16)
    u = _dot(hb, w_ref[:, :RWKV_COLS])
    u_ref[...] = u
    edge_ref[0, 0:1, :] = u[0:1]
    edge_ref[0, 1:2, :] = u[u.shape[0] - 1:]
    m_ref[...] = _dot(hb, w_ref[:, RWKV_COLS:RWKV_COLS + M_COLS])
    gt_ref[...] = _dot(hb, w_ref[:, RWKV_COLS + M_COLS:]).astype(gt_ref.dtype)


def _mod_spec(mod, blocks_per_batch):
    d = mod.shape[-1]
    if mod.shape[0] == 1:
        return pl.BlockSpec((1, 1, d), lambda i: (0, 0, 0))
    return pl.BlockSpec((1, 1, d), lambda i: (i // blocks_per_batch, 0, 0))


def _inproj(x2, sc, sh, g, w_in_p, tb, blocks_per_batch):
    n, d = x2.shape
    cols = w_in_p.shape[1]
    gate_cols = cols - RWKV_COLS - M_COLS
    return pl.pallas_call(
        _inproj_kernel,
        grid=(n // tb,),
        in_specs=[pl.BlockSpec((tb, d), lambda i: (i, 0)),
                  _mod_spec(sc, blocks_per_batch), _mod_spec(sh, blocks_per_batch),
                  pl.BlockSpec((1, d), lambda i: (0, 0)),
                  pl.BlockSpec((d, cols), lambda i: (0, 0))],
        out_specs=[pl.BlockSpec((tb, RWKV_COLS), lambda i: (i, 0)),
                   pl.BlockSpec((tb, M_COLS), lambda i: (i, 0)),
                   pl.BlockSpec((tb, gate_cols), lambda i: (i, 0)),
                   pl.BlockSpec((1, 2, RWKV_COLS), lambda i: (i, 0, 0))],
        out_shape=[jax.ShapeDtypeStruct((n, RWKV_COLS), F32),
                   jax.ShapeDtypeStruct((n, M_COLS), F32),
                   jax.ShapeDtypeStruct((n, gate_cols), BF16),
                   jax.ShapeDtypeStruct((n // tb, 2, RWKV_COLS), F32)],
        compiler_params=_cparams("arbitrary"),
        name="inproj",
    )(x2, sc, sh, g, w_in_p)


def _rwkv_prep_kernel(u_ref, hp_ref, hn_ref, sw_ref, w0_ref, a0_ref, wup_ref, aup_ref, gup_ref,
                      kk_ref, ka_ref, rk_ref, bd_ref,
                      r_o, v_o, nkk_o, kdf_o, kdb_o, bf_o, bb_o, lwf_o, lwb_o, g_o, bonus_o):
    u = u_ref[0]
    tb = u.shape[0]
    row = lax.broadcasted_iota(I32, u.shape, 0)
    prev = jnp.where(row == 0, hp_ref[0, 0], pltpu.roll(u, 1, 0))
    nxt = jnp.where(row == tb - 1, hn_ref[0, 0], pltpu.roll(u, tb - 1, 0))
    xs = sw_ref[0:1] * prev + sw_ref[1:2] * u + sw_ref[2:3] * nxt
    aw = A_WIDTH
    r = xs[:, 0:aw]
    k = xs[:, aw:2 * aw]
    v = xs[:, 2 * aw:3 * aw]
    o = 3 * aw
    dw = xs[:, o:o + 2 * DECAY_LORA]
    da = xs[:, o + 2 * DECAY_LORA:o + 2 * DECAY_LORA + 2 * ICLR_LORA]
    dg = xs[:, o + 2 * DECAY_LORA + 2 * ICLR_LORA:]
    bd = bd_ref[...]
    kkr = k * kk_ref[...]
    kk = kkr * lax.rsqrt(_dot((kkr * kkr).astype(BF16), bd) + 1e-12)
    tw = jnp.tanh(dw).astype(BF16)
    dab = da.astype(BF16)
    r_o[0] = r.astype(BF16)
    v_o[0] = v.astype(BF16)
    nkk_o[0] = (-kk).astype(BF16)
    for d, (lw_o, kd_o, b_o) in enumerate(((lwf_o, kdf_o, bf_o), (lwb_o, kdb_o, bb_o))):
        lw_o[0] = -DECAY_SCALE * jax.nn.sigmoid(w0_ref[d:d + 1] + _dot(tw, wup_ref[d]))
        a = jax.nn.sigmoid(a0_ref[d:d + 1] + _dot(dab, aup_ref[d]))
        kd_o[0] = (k * (1.0 + (a - 1.0) * ka_ref[...])).astype(BF16)
        b_o[0] = (kk * a).astype(BF16)
    g_o[0] = _dot(jax.nn.sigmoid(dg).astype(BF16), gup_ref[...]).astype(BF16)
    bonus_o[0] = (_dot_x2(r * k * rk_ref[...], bd) * v).astype(BF16)


def _rwkv_prep(u3, edges, shift_w, w0, a0, wup_p, aup_p, gup, k_k, k_a, r_k, bd, tb):
    b, t, cols = u3.shape
    nb = t // tb
    edges = edges.reshape(b, nb, 2, cols)
    zero = jnp.zeros((b, 1, cols), F32)
    halo_prev = jnp.concatenate([zero, edges[:, :nb - 1, 1]], axis=1).reshape(b, nb, 1, cols)
    halo_next = jnp.concatenate([edges[:, 1:, 0], zero], axis=1).reshape(b, nb, 1, cols)
    aw = A_WIDTH
    full2 = lambda s: pl.BlockSpec(s, lambda i, j: (0, 0))
    full3 = lambda s: pl.BlockSpec(s, lambda i, j: (0, 0, 0))
    out_spec = pl.BlockSpec((1, tb, aw), lambda i, j: (i, j, 0))
    sds = lambda dt: jax.ShapeDtypeStruct((b, t, aw), dt)
    out_dtypes = [BF16] * 7 + [F32] * 2 + [BF16] * 2
    return pl.pallas_call(
        _rwkv_prep_kernel,
        grid=(b, nb),
        in_specs=[pl.BlockSpec((1, tb, cols), lambda i, j: (i, j, 0)),
                  pl.BlockSpec((1, 1, 1, cols), lambda i, j: (i, j, 0, 0)),
                  pl.BlockSpec((1, 1, 1, cols), lambda i, j: (i, j, 0, 0)),
                  full2((3, cols)), full2((2, aw)), full2((2, aw)),
                  full3(wup_p.shape), full3(aup_p.shape), full2(gup.shape),
                  full2((1, aw)), full2((1, aw)), full2((1, aw)), full2((aw, aw))],
        out_specs=[out_spec] * 11,
        out_shape=[sds(dt) for dt in out_dtypes],
        compiler_params=_cparams("arbitrary", "arbitrary"),
        name="rwkv_prep",
    )(u3, halo_prev, halo_next, shift_w, w0, a0, wup_p, aup_p, gup, k_k, k_a, r_k, bd)


def _bdot(a, b):
    return lax.dot_general(a, b, (((2,), (1,)), ((0,), (0,))), preferred_element_type=F32)


def _bdot_nt(a, b):
    return lax.dot_general(a, b, (((2,), (2,)), ((0,), (0,))), preferred_element_type=F32)


def _bdot_tn(a, b):
    return lax.dot_general(a, b, (((1,), (1,)), ((0,), (0,))), preferred_element_type=F32)


def _split_pairs(x):
    return jnp.stack([x[g][:, p * LANES:(p + 1) * LANES]
                      for g in range(x.shape[0]) for p in range(x.shape[2] // LANES)], axis=0)


def _scan_kernel(r_ref, v_ref, nkk_ref, kd_ref, b_ref, lw_ref, s0_ref, o_ref, sf_ref, s_scr, *, reverse):
    c = pl.program_id(1)
    grp, ch, aw = lw_ref.shape
    n_pairs = aw // LANES
    state_shape = s_scr.shape

    @pl.when(c == 0)
    def _():
        s_scr[...] = s0_ref[...].reshape(state_shape)

    lw = lw_ref[...]
    ri = lax.broadcasted_iota(I32, (ch, ch), 0)
    ci = lax.broadcasted_iota(I32, (ch, ch), 1)
    tri = jnp.where((ci >= ri) if reverse else (ci <= ri), 1.0, 0.0).astype(BF16)
    hi = lw.astype(BF16)
    rem = lw - hi.astype(F32)
    mid = rem.astype(BF16)
    lo = (rem - mid.astype(F32)).astype(BF16)
    li = jnp.stack([_dot(tri, hi[g]) + (_dot(tri, mid[g]) + _dot(tri, lo[g])) for g in range(grp)], axis=0)
    lt = li[:, 0:1] if reverse else li[:, ch - 1:ch]
    rho = 0.5 * lt
    e1 = jnp.exp(li - rho)
    e2 = jnp.exp(rho - li)
    er = jnp.exp(rho)
    a_rel = nkk_ref[...] * (e1 * jnp.exp(-lw))
    r_rel = r_ref[...] * e1
    b_rel = b_ref[...] * e2
    k_rel = kd_ref[...] * e2
    pairs = lambda x: _split_pairs(x.astype(BF16))
    a_abs, r_abs, b_end, k_end = pairs(a_rel * er), pairs(r_rel * er), pairs(b_rel * er), pairs(k_rel * er)
    a_rel, r_rel, b_rel, k_rel = pairs(a_rel), pairs(r_rel), pairs(b_rel), pairs(k_rel)
    v = pairs(v_ref[...])
    decay = _split_pairs(jnp.exp(lt))
    lane = lax.broadcasted_iota(I32, (1, 1, LANES), 2)
    first = lane < A_HEAD_DIM

    def blockdiag(x):
        zero = jnp.zeros_like(x)
        return jnp.concatenate([jnp.where(first, x, zero), jnp.where(first, zero, x)], axis=1)

    ri2 = lax.broadcasted_iota(I32, (1, ch, 2 * ch), 1)
    ci2 = lax.broadcasted_iota(I32, (1, ch, 2 * ch), 2)
    ci2 = jnp.where(ci2 >= ch, ci2 - ch, ci2)
    if reverse:
        incl2, strict2 = ci2 >= ri2, ci2 > ri2
    else:
        incl2, strict2 = ci2 <= ri2, ci2 < ri2
    eye2 = jnp.where(ri2 == ci2, 1.0, 0.0)
    lhs = jnp.concatenate([a_rel, r_rel], axis=1)
    with_b = _bdot_nt(lhs, blockdiag(b_rel))
    with_k = _bdot_nt(lhs, blockdiag(k_rel))
    a_ab = jnp.where(strict2, with_b[:, :ch], 0.0)
    a_ak = jnp.where(strict2, with_k[:, :ch], 0.0).astype(BF16)
    a_r = jnp.concatenate([jnp.where(incl2, with_b[:, ch:], 0.0), jnp.where(incl2, with_k[:, ch:], 0.0)],
                          axis=2).astype(BF16)
    inv = eye2
    for k in range(int(math.log2(ch))):
        p, q = (ci2 >> k, ri2 >> k) if reverse else (ri2 >> k, ci2 >> k)
        joins = ((p ^ q) * 4 + (p - q)) == 5
        lk = jnp.where(joins, a_ab, 0.0)
        if k == 0:
            inv = inv + lk
        else:
            invb = inv.astype(BF16)
            inv = inv + _bdot(_bdot(invb, blockdiag(lk.astype(BF16))).astype(BF16), blockdiag(invb))
    s = s_scr[...]
    from_state = _bdot_nt(jnp.concatenate([a_abs, r_abs], axis=1), s.astype(BF16))
    v_bd = blockdiag(v)
    x = from_state[:, :ch] + _bdot(a_ak, v_bd)
    u = _bdot(inv.astype(BF16), blockdiag(x.astype(BF16))).astype(BF16)
    o = from_state[:, ch:] + _bdot(a_r, jnp.concatenate([blockdiag(u), v_bd], axis=1))
    si = lax.broadcasted_iota(I32, (1, LANES, LANES), 1) < A_HEAD_DIM
    same_head = si == (lax.broadcasted_iota(I32, (1, LANES, LANES), 2) < A_HEAD_DIM)
    upd = _bdot_tn(jnp.concatenate([u, v], axis=1), jnp.concatenate([b_end, k_end], axis=1))
    s_scr[...] = s * decay + jnp.where(same_head, upd, 0.0)
    for g in range(grp):
        o_ref[g] = jnp.concatenate([o[g * n_pairs + p] for p in range(n_pairs)], axis=1)

    @pl.when(c == pl.num_programs(1) - 1)
    def _():
        s_fin = s_scr[...]
        for g in range(grp):
            for p in range(n_pairs):
                pair = s_fin[g * n_pairs + p]
                sf_ref[g, 2 * p] = pair[:A_HEAD_DIM, :A_HEAD_DIM]
                sf_ref[g, 2 * p + 1] = pair[A_HEAD_DIM:, A_HEAD_DIM:]


def _wkv_scan(r, v, nkk, kd, b, lw, s0, reverse):
    bsz, t, aw = r.shape
    ch = SCAN_CHUNK
    assert t % ch == 0 and 2 * ch == LANES and 2 * A_HEAD_DIM == LANES
    nc = t // ch
    n_pairs = aw // LANES
    grp = SCAN_GROUP if bsz % SCAN_GROUP == 0 else 1
    tmap = (lambda i, c: (i, nc - 1 - c, 0)) if reverse else (lambda i, c: (i, c, 0))
    seq = pl.BlockSpec((grp, ch, aw), tmap)
    st_in = pl.BlockSpec((grp, n_pairs, LANES, LANES), lambda i, c: (i, 0, 0, 0))
    st_out = pl.BlockSpec((grp,) + s0.shape[1:], lambda i, c: (i, 0, 0, 0))
    return pl.pallas_call(
        functools.partial(_scan_kernel, reverse=reverse),
        grid=(bsz // grp, nc),
        in_specs=[seq] * 6 + [st_in],
        out_specs=[seq, st_out],
        out_shape=[jax.ShapeDtypeStruct((bsz, t, aw), F32), jax.ShapeDtypeStruct(s0.shape, F32)],
        scratch_shapes=[pltpu.VMEM((grp * n_pairs, LANES, LANES), F32)],
        compiler_params=_cparams("arbitrary", "arbitrary"),
        name="wkv_scan_bwd" if reverse else "wkv_scan_fwd",
    )(r, v, nkk, kd, b, lw, _pair_states(s0))


def _pair_states(s):
    b, h, d, _ = s.shape
    eye = jnp.eye(2, dtype=s.dtype)[None, None, :, None, :, None]
    return (s.reshape(b, h // 2, 2, d, 1, d) * eye).reshape(b, h // 2, 2 * d, 2 * d)


def _denominator_lane(width):
    lane = lax.broadcasted_iota(I32, (1, width), 1)
    return jnp.where((lane & (HEAD_PAD - 1)) == V_HEAD, 1.0, 0.0)


def _rope_lanes(x, cos_t, sin_up, sin_dn):
    w = x.shape[1]
    half = QK_ROPE // 2
    return x * cos_t + pltpu.roll(x, half, 1) * sin_up + pltpu.roll(x, w - half, 1) * sin_dn


def _mla_prep_kernel(*refs, rope):
    if rope:
        (m_ref, gq_ref, gkv_ref, wuq_ref, wk_ref, wv_ref, p_ref,
         qc_ref, qu_ref, qd_ref, kc_ref, ku_ref, kd_ref, q_o, k_o, v_o, ckv_o) = refs
    else:
        (m_ref, gq_ref, gkv_ref, wuq_ref, wk_ref, wv_ref, p_ref, q_o, k_o, v_o, ckv_o) = refs
    m = m_ref[...]
    qn = _rms(m[:, :Q_LORA], gq_ref[...])
    ckv = _rms(m[:, Q_LORA:Q_LORA + KV_LORA], gkv_ref[...])
    kp = m[:, Q_LORA + KV_LORA:]
    q = _dot(qn.astype(BF16), wuq_ref[...]) * ATTN_SCALE
    if rope:
        tile = lambda ref: jnp.concatenate([ref[...]] * B_HEADS, axis=1)
        q = _rope_lanes(q, tile(qc_ref), tile(qu_ref), tile(qd_ref))
        kp = _rope_lanes(kp, kc_ref[...], ku_ref[...], kd_ref[...])
    cb = ckv.astype(BF16)
    q_o[...] = q.astype(BF16)
    k_o[...] = (_dot(cb, wk_ref[...]) + _dot(kp.astype(BF16), p_ref[...])).astype(BF16)
    v_o[...] = (_dot(cb, wv_ref[...]) + _denominator_lane(v_o.shape[1])).astype(BF16)
    ckv_o[...] = ckv


def _mla_prep(m2, g_q, g_kv, wuq_p, wk_p, wv_p, place, rope_tabs, tb, t):
    n = m2.shape[0]
    hw = B_HEADS * HEAD_PAD
    full = lambda a: pl.BlockSpec(a.shape, lambda i: (0, 0))
    ins = [m2, g_q, g_kv, wuq_p, wk_p, wv_p, place]
    specs = [pl.BlockSpec((tb, M_COLS), lambda i: (i, 0))] + [full(a) for a in ins[1:]]
    if rope_tabs is not None:
        nbt = t // tb
        ins += list(rope_tabs)
        specs += [pl.BlockSpec((tb, HEAD_PAD), lambda i: (i % nbt, 0))] * 6
    big = pl.BlockSpec((tb, hw), lambda i: (i, 0))
    return pl.pallas_call(
        functools.partial(_mla_prep_kernel, rope=rope_tabs is not None),
        grid=(n // tb,),
        in_specs=specs,
        out_specs=[big, big, big, pl.BlockSpec((tb, KV_LORA), lambda i: (i, 0))],
        out_shape=[jax.ShapeDtypeStruct((n, hw), BF16)] * 3 + [jax.ShapeDtypeStruct((n, KV_LORA), F32)],
        compiler_params=_cparams("arbitrary"),
        name="mla_prep",
    )(*ins)


def _kv_up_kernel(ckv_ref, kp_ref, wk_ref, wv_ref, p_ref, k_o, v_o):
    cb = ckv_ref[...].astype(BF16)
    k_o[...] = (_dot(cb, wk_ref[...]) + _dot(kp_ref[...].astype(BF16), p_ref[...])).astype(BF16)
    v_o[...] = (_dot(cb, wv_ref[...]) + _denominator_lane(v_o.shape[1])).astype(BF16)


def _kv_up(ckv2, kpe_pad, wk_p, wv_p, place, tb):
    n = ckv2.shape[0]
    hw = B_HEADS * HEAD_PAD
    full = lambda a: pl.BlockSpec(a.shape, lambda i: (0, 0))
    big = pl.BlockSpec((tb, hw), lambda i: (i, 0))
    return pl.pallas_call(
        _kv_up_kernel,
        grid=(n // tb,),
        in_specs=[pl.BlockSpec((tb, KV_LORA), lambda i: (i, 0)),
                  pl.BlockSpec((tb, HEAD_PAD), lambda i: (i, 0)),
                  full(wk_p), full(wv_p), full(place)],
        out_specs=[big, big],
        out_shape=[jax.ShapeDtypeStruct((n, hw), BF16)] * 2,
        compiler_params=_cparams("arbitrary"),
        name="kv_up",
    )(ckv2, kpe_pad, wk_p, wv_p, place)


def _attn_kernel(*refs, two):
    if two:
        q_ref, k1_ref, v1_ref, k2_ref, v2_ref, o_ref = refs
    else:
        q_ref, k1_ref, v1_ref, o_ref = refs
    for h in range(B_HEADS):
        hs = slice(h * HEAD_PAD, (h + 1) * HEAD_PAD)
        q = q_ref[0, :, hs]
        s1 = _dot_nt(q, k1_ref[0, :, hs])
        mx = jnp.max(s1, axis=-1, keepdims=True)
        if two:
            s2 = _dot_nt(q, k2_ref[0, :, hs])
            mx = jnp.maximum(mx, jnp.max(s2, axis=-1, keepdims=True))
        acc = _dot(jnp.exp(s1 - mx).astype(BF16), v1_ref[0, :, hs])
        if two:
            acc = acc + _dot(jnp.exp(s2 - mx).astype(BF16), v2_ref[0, :, hs])
        o_ref[0, :, hs] = (acc / acc[:, V_HEAD:V_HEAD + 1]).astype(BF16)


def _attention(q3, k1, v1, k2, v2, tq):
    b, t, hw = q3.shape
    two = k2 is not None
    qspec = pl.BlockSpec((1, tq, hw), lambda i, j: (i, j, 0))
    kv = lambda a: pl.BlockSpec((1, a.shape[1], hw), lambda i, j: (i, 0, 0))
    ins = [q3, k1, v1] + ([k2, v2] if two else [])
    return pl.pallas_call(
        functools.partial(_attn_kernel, two=two),
        grid=(b, t // tq),
        in_specs=[qspec] + [kv(a) for a in ins[1:]],
        out_specs=qspec,
        out_shape=jax.ShapeDtypeStruct((b, t, hw), BF16),
        compiler_params=_cparams("arbitrary", "arbitrary"),
        name="attention",
    )(*ins)


def _post_mix_kernel(of_ref, ob_ref, bonus_ref, g_ref, att_ref, gates_ref, x_ref,
                     g1_ref, sc2_ref, sh2_ref, lnw_ref, lnb_ref, bd_ref, wa_ref, wb_ref, wo_ref,
                     gpost_ref, gpre_ref, wr_ref, x1_o, h2_o, aff_o):
    bd = bd_ref[...]
    inv_n = 1.0 / A_HEAD_DIM
    o = of_ref[...] + ob_ref[...]
    mu = _dot_x2(o, bd) * inv_n
    oc = o - mu
    var = _dot((oc * oc).astype(BF16), bd) * inv_n
    on = (oc * lax.rsqrt(var + GN_EPS)) * lnw_ref[...] + lnb_ref[...]
    ya = _dot(((on + bonus_ref[...]) * g_ref[...]).astype(BF16), wa_ref[...])
    yb = _dot(att_ref[...], wb_ref[...])
    d = ya.shape[1]
    gates = gates_ref[...].astype(F32)
    mix = jax.nn.sigmoid(gates[:, :d]) * ya + jax.nn.sigmoid(gates[:, d:]) * yb
    y = _dot(mix.astype(BF16), wo_ref[...])
    x1 = x_ref[...] + g1_ref[0] * _rms(y, gpost_ref[...])
    x1_o[...] = x1
    h2 = _rms(x1, gpre_ref[...]) * (1.0 + sc2_ref[0]) + sh2_ref[0]
    h2_o[...] = h2.astype(BF16)
    hh, hl = _split2(h2)
    wh, wl = _split2(wr_ref[...])
    logits = _dot_nt(wh, hh) + (_dot_nt(wh, hl) + _dot_nt(wl, hh))
    z = jnp.exp(logits - jnp.max(logits, axis=0, keepdims=True))
    aff_o[...] = z / jnp.sum(z, axis=0, keepdims=True)


def _post_mix(o_f, o_b, bonus, g, att, gates, x2, g1, sc2, sh2, ln_w, ln_b, bd, wa, wb, wo,
              g_post, g_pre, w_router_t, tb, blocks_per_batch):
    n, d = x2.shape
    tok = lambda a: pl.BlockSpec((tb, a.shape[1]), lambda i: (i, 0))
    full = lambda a: pl.BlockSpec(a.shape, lambda i: (0, 0))
    ms = lambda a: _mod_spec(a, blocks_per_batch)
    return pl.pallas_call(
        _post_mix_kernel,
        grid=(n // tb,),
        in_specs=[tok(o_f), tok(o_b), tok(bonus), tok(g), tok(att), tok(gates), tok(x2),
                  ms(g1), ms(sc2), ms(sh2), full(ln_w), full(ln_b), full(bd), full(wa), full(wb),
                  full(wo), full(g_post), full(g_pre), full(w_router_t)],
        out_specs=[pl.BlockSpec((tb, d), lambda i: (i, 0)),
                   pl.BlockSpec((tb, d), lambda i: (i, 0)),
                   pl.BlockSpec((N_EXPERTS, tb), lambda i: (0, i))],
        out_shape=[jax.ShapeDtypeStruct((n, d), F32),
                   jax.ShapeDtypeStruct((n, d), BF16),
                   jax.ShapeDtypeStruct((N_EXPERTS, n), F32)],
        compiler_params=_cparams("arbitrary"),
        name="post_mix",
    )(o_f, o_b, bonus, g, att, gates, x2, g1, sc2, sh2, ln_w, ln_b, bd, wa, wb, wo,
      g_post, g_pre, w_router_t)


def _select_kernel(a_ref, pos_ref, sel_ref, *, cap, n_chunks):
    a = a_ref[...]
    rows = a.shape[0]
    a3 = a.reshape(N_EXPERTS, n_chunks, LANES)

    def count(mask):
        c = jnp.sum(jnp.where(mask, 1.0, 0.0), axis=2, keepdims=True)
        return jnp.sum(c, axis=1, keepdims=True)

    def body(i, thr):
        cand = thr | jnp.left_shift(jnp.int32(1), 30 - i)
        return jnp.where(count(a3 >= pltpu.bitcast(cand, F32)) >= cap, cand, thr)

    thr = pltpu.bitcast(lax.fori_loop(0, 31, body, jnp.zeros((N_EXPERTS, 1, LANES), I32)), F32)
    gt = a3 > thr
    eq = a3 == thr
    need = cap - count(gt)

    li = lax.broadcasted_iota(I32, (LANES, LANES), 0)
    lj = lax.broadcasted_iota(I32, (LANES, LANES), 1)
    lane_before = jnp.where(li < lj, 1.0, 0.0).astype(BF16)
    lane_all = jnp.ones((LANES, LANES), BF16)
    ci = lax.broadcasted_iota(I32, (n_chunks, n_chunks), 0)
    cj = lax.broadcasted_iota(I32, (n_chunks, n_chunks), 1)
    chunk_before = jnp.where(cj < ci, 1.0, 0.0).astype(BF16)

    def prefix(flags3):
        f2 = flags3.reshape(rows, LANES).astype(BF16)
        within = _dot(f2, lane_before).reshape(N_EXPERTS, n_chunks, LANES)
        tot = _dot(f2, lane_all).astype(BF16).reshape(N_EXPERTS, n_chunks, LANES)
        offs = [_dot(chunk_before, tot[e]) for e in range(N_EXPERTS)]
        return within + jnp.stack(offs, axis=0)

    eqf = jnp.where(eq, 1.0, 0.0)
    tie_ok = jnp.where(prefix(eqf) < need, eqf, 0.0)
    sel = jnp.where(gt, 1.0, tie_ok)
    pos_ref[...] = prefix(sel).reshape(rows, LANES).astype(I32)
    sel_ref[...] = sel.reshape(rows, LANES).astype(I32)


def _select(aff_t, cap):
    e, n = aff_t.shape
    n_chunks = n // LANES
    rows = e * n_chunks
    a2 = aff_t.reshape(rows, LANES)
    spec = pl.BlockSpec((rows, LANES), lambda i: (0, 0))
    pos, sel = pl.pallas_call(
        functools.partial(_select_kernel, cap=cap, n_chunks=n_chunks),
        grid=(1,),
        in_specs=[spec],
        out_specs=[spec, spec],
        out_shape=[jax.ShapeDtypeStruct((rows, LANES), I32)] * 2,
        compiler_params=_cparams("arbitrary"),
        name="expert_select",
    )(a2)
    return pos.reshape(e, n), sel.reshape(e, n)


def _ffn_kernel(lo_ref, hi_ref, slot_ref, aff_ref, h_ref, wg_ref, wu_ref, wd_ref, y_ref, acc_ref, wacc_ref,
                *, n_tiles, sub):
    e = pl.program_id(0)
    j = pl.program_id(1)
    rows = acc_ref.shape[0]
    tb = slot_ref.shape[2]
    n_sub = rows // sub
    acc_ref[...] = jnp.zeros_like(acc_ref)
    wacc_ref[...] = jnp.zeros_like(wacc_ref)
    for si in range(n_sub):
        r_id = lax.broadcasted_iota(I32, (sub, tb), 0) + (j * rows + si * sub)
        part = slice(si * sub, (si + 1) * sub)

        def gather_block(b, carry):
            hit = r_id == slot_ref[0, pl.ds(b, 1), :]
            tokens = h_ref[pl.ds(pl.multiple_of(b * tb, tb), tb), :]
            acc_ref[part, :] += _dot(jnp.where(hit, 1.0, 0.0).astype(BF16), tokens)
            wacc_ref[part, :] += jnp.sum(jnp.where(hit, aff_ref[0, pl.ds(b, 1), :], 0.0), axis=1, keepdims=True)
            return carry

        ix = (e * n_tiles + j) * n_sub + si
        lax.fori_loop(lo_ref[ix], hi_ref[ix], gather_block, 0)
    xe = acc_ref[...].astype(BF16)
    gate = _dot(xe, wg_ref[0])
    up = _dot(xe, wu_ref[0])
    hid = (gate * jax.nn.sigmoid(gate)) * up
    y_ref[0] = (_dot(hid.astype(BF16), wd_ref[0]) * wacc_ref[...]).astype(BF16)


def _expert_ffn(pos, slot, aff_t, h2, wg, wu, wd, cap, tb, rt):
    n, d = h2.shape
    nb = n // tb
    e, _, f = wg.shape
    n_tiles = cap // rt
    sub = min(GATHER_ROWS, rt)
    starts = jnp.concatenate([pos[:, ::tb], jnp.full((e, 1), cap, I32)], axis=1)
    edges = jnp.arange(cap // sub, dtype=I32) * sub
    lo = jnp.sum((starts[:, None, 1:] <= edges[None, :, None]).astype(I32), axis=2)
    hi = jnp.sum((starts[:, None, :-1] < (edges + sub)[None, :, None]).astype(I32), axis=2)
    grid_spec = pltpu.PrefetchScalarGridSpec(
        num_scalar_prefetch=2,
        grid=(e, n_tiles),
        in_specs=[pl.BlockSpec((1, nb, tb), lambda ei, j, lo, hi: (ei, 0, 0)),
                  pl.BlockSpec((1, nb, tb), lambda ei, j, lo, hi: (ei, 0, 0)),
                  pl.BlockSpec((n, d), lambda ei, j, lo, hi: (0, 0), pipeline_mode=pl.Buffered(1)),
                  pl.BlockSpec((1, d, f), lambda ei, j, lo, hi: (ei, 0, 0)),
                  pl.BlockSpec((1, d, f), lambda ei, j, lo, hi: (ei, 0, 0)),
                  pl.BlockSpec((1, f, d), lambda ei, j, lo, hi: (ei, 0, 0))],
        out_specs=pl.BlockSpec((1, rt, d), lambda ei, j, lo, hi: (ei, j, 0)),
        scratch_shapes=[pltpu.VMEM((rt, d), F32), pltpu.VMEM((rt, 1), F32)],
    )
    return pl.pallas_call(
        functools.partial(_ffn_kernel, n_tiles=n_tiles, sub=sub),
        grid_spec=grid_spec,
        out_shape=jax.ShapeDtypeStruct((e, cap, d), BF16),
        compiler_params=_cparams("arbitrary", "arbitrary"),
        name="expert_ffn",
    )(lo.reshape(-1), hi.reshape(-1), slot.reshape(e, nb, tb), aff_t.reshape(e, nb, tb), h2, wg, wu, wd)


def _combine_kernel(t0_ref, fetch_ref, need_ref, slot_ref, x_ref, g2_ref, gp_ref, *rest, n_win):
    y_refs, o_ref, acc_ref, first_ref = rest[:-3], rest[-3], rest[-2], rest[-1]
    blk = pl.program_id(0)
    n_blk = pl.num_programs(0)
    tb = slot_ref.shape[0]
    rows = y_refs[0].shape[1]
    lane = lax.broadcasted_iota(I32, (tb, rows), 1)

    def onehot(e, k):
        slot = slot_ref[:, e:e + 1] - (t0_ref[blk * N_EXPERTS + e] + k) * rows
        return jnp.where(lane == slot, 1.0, 0.0).astype(BF16)

    for e in range(N_EXPERTS):
        first_ref[e * rows:(e + 1) * rows, :] = y_refs[e * n_win][0]
    picks = jnp.concatenate([onehot(e, 0) for e in range(N_EXPERTS)], axis=1)
    acc_ref[...] = _dot(picks, first_ref[...])
    for k in range(1, n_win):
        for e in range(N_EXPERTS):
            @pl.when(need_ref[(k * n_blk + blk) * N_EXPERTS + e] != 0)
            def _():
                acc_ref[...] += _dot(onehot(e, k), y_refs[e * n_win + k][0])
    o_ref[...] = x_ref[...] + g2_ref[0] * _rms(acc_ref[...], gp_ref[...])


def _combine_windows(pos, tb, rt, cap):
    n_tiles = cap // rt
    lo = pos[:, ::tb]
    hi = jnp.concatenate([lo[:, 1:], jnp.full((lo.shape[0], 1), cap, I32)], axis=1)
    t0 = jnp.minimum(lo // rt, n_tiles - 1)
    t_last = jnp.where(hi > lo, (hi - 1) // rt, t0)
    fetch, need = [t0], [jnp.ones_like(t0)]
    for k in range(1, tb // rt + 1):
        need.append(t_last >= t0 + k)
        fetch.append(lax.cummax(jnp.where(need[k], t0 + k, 0), axis=1))
    flat = lambda a: a.T.reshape(-1).astype(I32)
    return flat(t0), jnp.concatenate([flat(f) for f in fetch]), jnp.concatenate([flat(f) for f in need])


def _combine(windows, slot_t, ye, x1, g2, g_post, tb, rt, blocks_per_batch):
    t0, fetch, need = windows
    n, d = x1.shape
    e, cap, _ = ye.shape
    nb = n // tb
    n_win = tb // rt + 1
    if g2.shape[0] == 1:
        g2_spec = pl.BlockSpec((1, 1, d), lambda b, t, f, nd: (0, 0, 0))
    else:
        g2_spec = pl.BlockSpec((1, 1, d), lambda b, t, f, nd: (b // blocks_per_batch, 0, 0))

    def window(ei, k):
        return pl.BlockSpec((1, rt, d), lambda b, t, f, nd: (ei, f[(k * nb + b) * e + ei], 0))

    grid_spec = pltpu.PrefetchScalarGridSpec(
        num_scalar_prefetch=3,
        grid=(nb,),
        in_specs=[pl.BlockSpec((tb, e), lambda b, t, f, nd: (b, 0)),
                  pl.BlockSpec((tb, d), lambda b, t, f, nd: (b, 0)),
                  g2_spec,
                  pl.BlockSpec((1, d), lambda b, t, f, nd: (0, 0))]
                 + [window(ei, k) for ei in range(e) for k in range(n_win)],
        out_specs=pl.BlockSpec((tb, d), lambda b, t, f, nd: (b, 0)),
        scratch_shapes=[pltpu.VMEM((tb, d), F32), pltpu.VMEM((e * rt, d), BF16)],
    )
    return pl.pallas_call(
        functools.partial(_combine_kernel, n_win=n_win),
        grid_spec=grid_spec,
        out_shape=jax.ShapeDtypeStruct((n, d), F32),
        compiler_params=_cparams("arbitrary"),
        name="moe_combine",
    )(t0, fetch, need, slot_t, x1, g2, g_post, *([ye] * (e * n_win)))


def _rope_tables(t):
    half = QK_ROPE // 2
    pos = jnp.arange(t)
    row = (pos // GRID_W).astype(F32)
    col = (pos % GRID_W).astype(F32)
    inv = ROPE_BASE ** (-jnp.arange(0, half, 2, dtype=F32) / half)
    ang = jnp.concatenate([row[:, None] * inv, col[:, None] * inv], axis=-1)
    cos, sin = jnp.cos(ang), jnp.sin(ang)

    def tabs(first_lane):
        c = jnp.ones((t, HEAD_PAD), F32)
        c = c.at[:, first_lane:first_lane + half].set(cos).at[:, first_lane + half:first_lane + 2 * half].set(cos)
        up = jnp.zeros((t, HEAD_PAD), F32).at[:, first_lane + half:first_lane + 2 * half].set(sin)
        dn = jnp.zeros((t, HEAD_PAD), F32).at[:, first_lane:first_lane + half].set(-sin)
        return c, up, dn

    return tabs(QK_NOPE + QK_ROPE) + tabs(QK_ROPE)


def _layout_weights(w_in, w_uq, w_ukv, w_branch_b, w_up, a_up):
    d = w_in.shape[0]
    deint = jnp.concatenate([jnp.arange(0, QK_ROPE, 2), jnp.arange(1, QK_ROPE, 2)])
    kpe0 = RWKV_COLS + Q_LORA + KV_LORA
    kpe_cols = w_in[:, kpe0:kpe0 + QK_ROPE]
    w_in_p = jnp.concatenate(
        [w_in[:, :kpe0 + QK_ROPE], kpe_cols[:, deint],
         jnp.zeros((d, M_COLS - MLA_COLS - QK_ROPE), F32), w_in[:, RWKV_COLS + MLA_COLS:]], axis=1)
    uq = w_uq.reshape(Q_LORA, B_HEADS, QK_NOPE + QK_ROPE)
    wuq_p = jnp.concatenate([uq, uq[:, :, QK_NOPE:][:, :, deint]], axis=2).reshape(Q_LORA, B_HEADS * HEAD_PAD)
    ukv = w_ukv.reshape(KV_LORA, B_HEADS, QK_NOPE + V_HEAD)
    zpad = jnp.zeros((KV_LORA, B_HEADS, HEAD_PAD - QK_NOPE), F32)
    wk_p = jnp.concatenate([ukv[:, :, :QK_NOPE], zpad], axis=2).reshape(KV_LORA, B_HEADS * HEAD_PAD)
    wv_p = jnp.concatenate([ukv[:, :, QK_NOPE:], zpad], axis=2).reshape(KV_LORA, B_HEADS * HEAD_PAD)
    wb = w_branch_b.reshape(B_HEADS, V_HEAD, d)
    wb_p = jnp.concatenate([wb, jnp.zeros((B_HEADS, HEAD_PAD - V_HEAD, d), F32)], axis=1).reshape(B_HEADS * HEAD_PAD, d)
    eye = jnp.eye(QK_ROPE, dtype=F32)
    z = jnp.zeros((QK_ROPE, QK_ROPE), F32)
    head_raw = jnp.concatenate([jnp.zeros((QK_ROPE, QK_NOPE), F32), eye, z], axis=1)
    head_rot = jnp.concatenate([jnp.zeros((QK_ROPE, QK_NOPE), F32), z, eye], axis=1)
    zrows = jnp.zeros((HEAD_PAD - 2 * QK_ROPE, B_HEADS * HEAD_PAD), F32)
    zr = jnp.zeros((QK_ROPE, B_HEADS * HEAD_PAD), F32)
    place_raw = jnp.concatenate([jnp.tile(head_raw, (1, B_HEADS)), zr, zrows], axis=0)
    place_rot = jnp.concatenate([zr, jnp.tile(head_rot, (1, B_HEADS)), zrows], axis=0)
    zl = jnp.zeros((DECAY_LORA, A_WIDTH), F32)
    wup_p = jnp.stack([jnp.concatenate([w_up[0], zl]), jnp.concatenate([zl, w_up[1]])])
    aup_p = jnp.stack([jnp.concatenate([a_up[0], zl]), jnp.concatenate([zl, a_up[1]])])
    bf = lambda a: a.astype(BF16)
    return (bf(w_in_p), bf(wuq_p), bf(wk_p), bf(wv_p), bf(wb_p), bf(place_raw), bf(place_rot),
            bf(wup_p), bf(aup_p))


def _group_layer(x, mods, lw, s0_f, s0_b, ctx_kv, rope_tabs):
    bsz, t, d = x.shape
    n = bsz * t
    tb = min(TOKEN_BLOCK, t)
    tw = min(ROW_BLOCK, t)
    bpb = t // tb
    bpw = t // tw
    sh1, sc1, g1, sh2, sc2, g2 = mods
    x2 = x.reshape(n, d)
    u, m, gates, edges = _inproj(x2, sc1, sh1, lw["g_pre_mix"], lw["w_in_p"], tw, bpw)
    prep = _rwkv_prep(u.reshape(bsz, t, RWKV_COLS), edges, lw["shift_w"], lw["w0"], lw["a0"], lw["wup_p"],
                      lw["aup_p"], lw["g_up"], lw["k_k"], lw["k_a"], lw["r_k"], lw["bd"], tw)
    r, v, nkk, kd_f, kd_b, b_f, b_b, lw_f, lw_b, g, bonus = prep
    o_f, s_f = _wkv_scan(r, v, nkk, kd_f, b_f, lw_f, s0_f, reverse=False)
    o_b, s_b = _wkv_scan(r, v, nkk, kd_b, b_b, lw_b, s0_b, reverse=True)
    place = lw["place_raw"] if rope_tabs is None else lw["place_rot"]
    q, k, vv, ckv = _mla_prep(m, lw["g_qnorm"], lw["g_kvnorm"], lw["wuq_p"], lw["wk_p"], lw["wv_p"],
                              place, rope_tabs, tw, t)
    hw = B_HEADS * HEAD_PAD
    k2, v2 = ctx_kv if ctx_kv is not None else (None, None)
    att = _attention(q.reshape(bsz, t, hw), k.reshape(bsz, t, hw), vv.reshape(bsz, t, hw), k2, v2, tw)
    two = lambda a: a.reshape(n, a.shape[-1])
    x1, h2, aff_t = _post_mix(two(o_f), two(o_b), two(bonus), two(g), att.reshape(n, hw), gates, x2,
                              g1, sc2, sh2, lw["ln_x_w"], lw["ln_x_b"], lw["bd"], lw["w_branch_a"],
                              lw["wb_p"], lw["w_out"], lw["g_post_mix"], lw["g_pre_ffn"],
                              lw["w_router_t"], tw, bpw)
    cap = CAPACITY_FACTOR * n // N_EXPERTS
    rt = min(EXPERT_TILE, cap)
    pos, sel = _select(aff_t, cap)
    slot = jnp.where(sel > 0, pos, -1)
    ye = _expert_ffn(pos, slot, aff_t, h2, lw["w_exp_gate"], lw["w_exp_up"], lw["w_exp_down"], cap,
                     min(GATHER_BLOCK, n), rt)
    rc = min(COMBINE_TILE, cap)
    out = _combine(_combine_windows(pos, tb, rc, cap), slot.T, ye, x1, g2, lw["g_post_ffn"], tb, rc, bpb)
    kpe = m[:, Q_LORA + KV_LORA:Q_LORA + KV_LORA + QK_ROPE]
    return out.reshape(bsz, t, d), (ckv.reshape(bsz, t, KV_LORA), kpe.reshape(bsz, t, QK_ROPE), s_f, s_b)


def kernel(x_prompt, x_sample, cache_ckv, cache_kpe, state_wkv_fwd, state_wkv_bwd, c, c_ctx,
           w_mod, b_mod, g_pre_mix, g_post_mix, g_pre_ffn, g_post_ffn, w_in, shift_w,
           w0, w_up, a0, a_up, g_up, k_k, k_a, r_k, ln_x_w, ln_x_b, w_branch_a,
           g_qnorm, w_uq, g_kvnorm, w_ukv, w_branch_b, w_out,
           w_router, w_exp_gate, w_exp_up, w_exp_down):
    depth = w_mod.shape[0]
    d = x_prompt.shape[-1]
    dec_b, dec_t = x_sample.shape[0], x_sample.shape[1]
    xp, xs = x_prompt, x_sample
    c_rows = jnp.concatenate([c, c_ctx[None, :],
                              jnp.zeros((-(dec_b + 1) % 8, d), F32)], axis=0)
    rope_tabs = _rope_tables(dec_t)
    ii = lax.broadcasted_iota(I32, (A_WIDTH, A_WIDTH), 0) // A_HEAD_DIM
    jj = lax.broadcasted_iota(I32, (A_WIDTH, A_WIDTH), 1) // A_HEAD_DIM
    bd = (ii == jj).astype(BF16)
    row = lambda a: a.reshape(1, -1)
    bf = lambda a: a.astype(BF16)
    ckv_l, kpe_l, sf_l, sb_l = [], [], [], []
    for l in range(depth):
        (w_in_p, wuq_p, wk_p, wv_p, wb_p, place_raw, place_rot, wup_p, aup_p) = _layout_weights(
            w_in[l], w_uq[l], w_ukv[l], w_branch_b[l], w_up[l], a_up[l])
        lw = {
            "g_pre_mix": row(g_pre_mix[l]), "g_post_mix": row(g_post_mix[l]),
            "g_pre_ffn": row(g_pre_ffn[l]), "g_post_ffn": row(g_post_ffn[l]),
            "w_in_p": w_in_p, "shift_w": shift_w[l], "w0": w0[l], "a0": a0[l],
            "wup_p": wup_p, "aup_p": aup_p, "g_up": bf(g_up[l]),
            "k_k": row(k_k[l]), "k_a": row(k_a[l]), "r_k": row(r_k[l]), "bd": bd,
            "ln_x_w": row(ln_x_w[l]), "ln_x_b": row(ln_x_b[l]), "w_branch_a": bf(w_branch_a[l]),
            "g_qnorm": row(g_qnorm[l]), "g_kvnorm": row(g_kvnorm[l]),
            "wuq_p": wuq_p, "wk_p": wk_p, "wv_p": wv_p, "wb_p": wb_p,
            "place_raw": place_raw, "place_rot": place_rot,
            "w_out": bf(w_out[l]), "w_router_t": w_router[l].T,
            "w_exp_gate": bf(w_exp_gate[l]), "w_exp_up": bf(w_exp_up[l]), "w_exp_down": bf(w_exp_down[l]),
        }
        mod = _modulation(c_rows, w_mod[l], b_mod[l])
        mods_lat = [mod[:dec_b, i * d:(i + 1) * d].reshape(dec_b, 1, d) for i in range(6)]
        mods_ctx = [mod[dec_b:dec_b + 1, i * d:(i + 1) * d].reshape(1, 1, d) for i in range(6)]
        zeros_state = jnp.zeros((xp.shape[0], A_HEADS, A_HEAD_DIM, A_HEAD_DIM), F32)
        xp, (ckv, kpe, s_f, s_b) = _group_layer(xp, mods_ctx, lw, zeros_state, zeros_state, None, None)
        ckv_l.append(ckv)
        kpe_l.append(kpe)
        sf_l.append(s_f)
        sb_l.append(s_b)
        past = cache_ckv.shape[2]
        kpe_pad = jnp.concatenate(
            [cache_kpe[:, l], jnp.zeros((dec_b, past, HEAD_PAD - QK_ROPE), F32)], axis=-1)
        k_ctx, v_ctx = _kv_up(cache_ckv[:, l].reshape(dec_b * past, KV_LORA),
                              kpe_pad.reshape(dec_b * past, HEAD_PAD), wk_p, wv_p, place_raw,
                              min(TOKEN_BLOCK, past))
        hw = B_HEADS * HEAD_PAD
        ctx_kv = (k_ctx.reshape(dec_b, past, hw), v_ctx.reshape(dec_b, past, hw))
        xs, _ = _group_layer(xs, mods_lat, lw, state_wkv_fwd[:, l], state_wkv_bwd[:, l], ctx_kv, rope_tabs)
    return (xp, xs, jnp.stack(ckv_l, axis=1), jnp.stack(kpe_l, axis=1),
            jnp.stack(sf_l, axis=1), jnp.stack(sb_l, axis=1))
```

```python
import functools
import math

import jax
import jax.numpy as jnp
from jax import lax
from jax.experimental import pallas as pl
from jax.experimental.pallas import tpu as pltpu

F32 = jnp.float32
BF16 = jnp.bfloat16
I32 = jnp.int32

GRID_W = 64
A_HEADS = 8
A_HEAD_DIM = 64
A_WIDTH = A_HEADS * A_HEAD_DIM
DECAY_LORA = 64
ICLR_LORA = 64
GATE_LORA = 128
DECAY_SCALE = 0.6065306597126334
GN_EPS = 64e-5
B_HEADS = 8
Q_LORA = 256
KV_LORA = 128
QK_NOPE = 64
QK_ROPE = 32
V_HEAD = 64
ROPE_BASE = 10000.0
ATTN_SCALE = 1.0 / math.sqrt(QK_NOPE + QK_ROPE)
N_EXPERTS = 16
CAPACITY_FACTOR = 2
EPS = 1e-6
RWKV_COLS = 3 * A_WIDTH + 2 * DECAY_LORA + 2 * ICLR_LORA + GATE_LORA
MLA_COLS = Q_LORA + KV_LORA + QK_ROPE

LANES = 128
HEAD_PAD = 128
M_COLS = 512
VMEM_LIMIT = 56 * 1024 * 1024

SCAN_CHUNK = 64
SCAN_GROUP = 8
SCAN_PACK = 2
TOKEN_BLOCK = 256
ROW_BLOCK = 512
EXPERT_TILE = 256
FFN_TILE = 256
GATHER_BLOCK = 512
GATHER_ROWS = 128


def _cparams(*sem):
    return pltpu.CompilerParams(dimension_semantics=sem, vmem_limit_bytes=VMEM_LIMIT)


def _dot(a, b):
    return jnp.dot(a, b, preferred_element_type=F32)


def _dot_nt(a, b):
    return lax.dot_general(a, b, (((1,), (1,)), ((), ())), preferred_element_type=F32)


def _dot_tn(a, b):
    return lax.dot_general(a, b, (((0,), (0,)), ((), ())), preferred_element_type=F32)


def _split2(x):
    hi = x.astype(BF16)
    lo = (x - hi.astype(F32)).astype(BF16)
    return hi, lo


def _dot_x2(a, b_bf16):
    hi, lo = _split2(a)
    return _dot(hi, b_bf16) + _dot(lo, b_bf16)


def _dot_f32(a, b):
    ah, al = _split2(a)
    bh, bl = _split2(b)
    return _dot(ah, bh) + (_dot(ah, bl) + _dot(al, bh))


def _rms(x, g):
    return (x * lax.rsqrt(jnp.mean(x * x, axis=-1, keepdims=True) + EPS)) * g


def _mod_kernel(c_ref, w_ref, b_ref, o_ref):
    c = c_ref[...]
    s = c * jax.nn.sigmoid(c)
    o_ref[...] = _dot_f32(s, w_ref[...]) + b_ref[...]


def _modulation(c_rows, w_mod, b_mod):
    rows, d = c_rows.shape
    n_out = w_mod.shape[1]
    tn = n_out // 8
    return pl.pallas_call(
        _mod_kernel,
        grid=(n_out // tn,),
        in_specs=[pl.BlockSpec((rows, d), lambda j: (0, 0)),
                  pl.BlockSpec((d, tn), lambda j: (0, j)),
                  pl.BlockSpec((1, tn), lambda j: (0, j))],
        out_specs=pl.BlockSpec((rows, tn), lambda j: (0, j)),
        out_shape=jax.ShapeDtypeStruct((rows, n_out), F32),
        compiler_params=_cparams("arbitrary"),
        name="modulation",
    )(c_rows, w_mod, b_mod.reshape(1, n_out))


def _inproj_kernel(x_ref, sc_ref, sh_ref, g_ref, w_ref, u_ref, m_ref, gt_ref, edge_ref):
    h = _rms(x_ref[...], g_ref[...]) * (1.0 + sc_ref[0]) + sh_ref[0]
    hb = h.astype(BF16)
    u = _dot(hb, w_ref[:, :RWKV_COLS])
    u_ref[...] = u
    edge_ref[0, 0:1, :] = u[0:1]
    edge_ref[0, 1:2, :] = u[u.shape[0] - 1:]
    m_ref[...] = _dot(hb, w_ref[:, RWKV_COLS:RWKV_COLS + M_COLS])
    gt_ref[...] = _dot(hb, w_ref[:, RWKV_COLS + M_COLS:]).astype(gt_ref.dtype)


def _mod_spec(mod, blocks_per_batch):
    d = mod.shape[-1]
    if mod.shape[0] == 1:
        return pl.BlockSpec((1, 1, d), lambda i: (0, 0, 0))
    return pl.BlockSpec((1, 1, d), lambda i: (i // blocks_per_batch, 0, 0))


def _inproj(x2, sc, sh, g, w_in_p, tb, blocks_per_batch):
    n, d = x2.shape
    cols = w_in_p.shape[1]
    gate_cols = cols - RWKV_COLS - M_COLS
    return pl.pallas_call(
        _inproj_kernel,
        grid=(n // tb,),
        in_specs=[pl.BlockSpec((tb, d), lambda i: (i, 0)),
                  _mod_spec(sc, blocks_per_batch), _mod_spec(sh, blocks_per_batch),
                  pl.BlockSpec((1, d), lambda i: (0, 0)),
                  pl.BlockSpec((d, cols), lambda i: (0, 0))],
        out_specs=[pl.BlockSpec((tb, RWKV_COLS), lambda i: (i, 0)),
                   pl.BlockSpec((tb, M_COLS), lambda i: (i, 0)),
                   pl.BlockSpec((tb, gate_cols), lambda i: (i, 0)),
                   pl.BlockSpec((1, 2, RWKV_COLS), lambda i: (i, 0, 0))],
        out_shape=[jax.ShapeDtypeStruct((n, RWKV_COLS), F32),
                   jax.ShapeDtypeStruct((n, M_COLS), F32),
                   jax.ShapeDtypeStruct((n, gate_cols), BF16),
                   jax.ShapeDtypeStruct((n // tb, 2, RWKV_COLS), F32)],
        compiler_params=_cparams("arbitrary"),
        name="inproj",
    )(x2, sc, sh, g, w_in_p)


def _rwkv_prep_kernel(u_ref, hp_ref, hn_ref, sw_ref, w0_ref, a0_ref, wup_ref, aup_ref, gup_ref,
                      kk_ref, ka_ref, rk_ref, bd_ref,
                      r_o, v_o, nkk_o, kdf_o, kdb_o, bf_o, bb_o, lwf_o, lwb_o, g_o, bonus_o):
    u = u_ref[0]
    tb = u.shape[0]
    row = lax.broadcasted_iota(I32, u.shape, 0)
    prev = jnp.where(row == 0, hp_ref[0, 0], pltpu.roll(u, 1, 0))
    nxt = jnp.where(row == tb - 1, hn_ref[0, 0], pltpu.roll(u, tb - 1, 0))
    xs = sw_ref[0:1] * prev + sw_ref[1:2] * u + sw_ref[2:3] * nxt
    aw = A_WIDTH
    r = xs[:, 0:aw]
    k = xs[:, aw:2 * aw]
    v = xs[:, 2 * aw:3 * aw]
    o = 3 * aw
    dw = xs[:, o:o + 2 * DECAY_LORA]
    da = xs[:, o + 2 * DECAY_LORA:o + 2 * DECAY_LORA + 2 * ICLR_LORA]
    dg = xs[:, o + 2 * DECAY_LORA + 2 * ICLR_LORA:]
    bd = bd_ref[...]
    kkr = k * kk_ref[...]
    kk = kkr * lax.rsqrt(_dot((kkr * kkr).astype(BF16), bd) + 1e-12)
    tw = jnp.tanh(dw).astype(BF16)
    dab = da.astype(BF16)
    r_o[0] = r.astype(BF16)
    v_o[0] = v.astype(BF16)
    nkk_o[0] = (-kk).astype(BF16)
    for d, (lw_o, kd_o, b_o) in enumerate(((lwf_o, kdf_o, bf_o), (lwb_o, kdb_o, bb_o))):
        lw_o[0] = -DECAY_SCALE * jax.nn.sigmoid(w0_ref[d:d + 1] + _dot(tw, wup_ref[d]))
        a = jax.nn.sigmoid(a0_ref[d:d + 1] + _dot(dab, aup_ref[d]))
        kd_o[0] = (k * (1.0 + (a - 1.0) * ka_ref[...])).astype(BF16)
        b_o[0] = (kk * a).astype(BF16)
    g_o[0] = _dot(jax.nn.sigmoid(dg).astype(BF16), gup_ref[...]).astype(BF16)
    bonus_o[0] = (_dot_x2(r * k * rk_ref[...], bd) * v).astype(BF16)


def _rwkv_prep(u3, edges, shift_w, w0, a0, wup_p, aup_p, gup, k_k, k_a, r_k, bd, tb):
    b, t, cols = u3.shape
    nb = t // tb
    edges = edges.reshape(b, nb, 2, cols)
    zero = jnp.zeros((b, 1, cols), F32)
    halo_prev = jnp.concatenate([zero, edges[:, :nb - 1, 1]], axis=1).reshape(b, nb, 1, cols)
    halo_next = jnp.concatenate([edges[:, 1:, 0], zero], axis=1).reshape(b, nb, 1, cols)
    aw = A_WIDTH
    full2 = lambda s: pl.BlockSpec(s, lambda i, j: (0, 0))
    full3 = lambda s: pl.BlockSpec(s, lambda i, j: (0, 0, 0))
    out_spec = pl.BlockSpec((1, tb, aw), lambda i, j: (i, j, 0))
    sds = lambda dt: jax.ShapeDtypeStruct((b, t, aw), dt)
    out_dtypes = [BF16] * 7 + [F32] * 2 + [BF16] * 2
    return pl.pallas_call(
        _rwkv_prep_kernel,
        grid=(b, nb),
        in_specs=[pl.BlockSpec((1, tb, cols), lambda i, j: (i, j, 0)),
                  pl.BlockSpec((1, 1, 1, cols), lambda i, j: (i, j, 0, 0)),
                  pl.BlockSpec((1, 1, 1, cols), lambda i, j: (i, j, 0, 0)),
                  full2((3, cols)), full2((2, aw)), full2((2, aw)),
                  full3(wup_p.shape), full3(aup_p.shape), full2(gup.shape),
                  full2((1, aw)), full2((1, aw)), full2((1, aw)), full2((aw, aw))],
        out_specs=[out_spec] * 11,
        out_shape=[sds(dt) for dt in out_dtypes],
        compiler_params=_cparams("arbitrary", "arbitrary"),
        name="rwkv_prep",
    )(u3, halo_prev, halo_next, shift_w, w0, a0, wup_p, aup_p, gup, k_k, k_a, r_k, bd)


def _bdot(a, b):
    return lax.dot_general(a, b, (((2,), (1,)), ((0,), (0,))), preferred_element_type=F32)


def _bdot_nt(a, b):
    return lax.dot_general(a, b, (((2,), (2,)), ((0,), (0,))), preferred_element_type=F32)


def _bdot_tn(a, b):
    return lax.dot_general(a, b, (((1,), (1,)), ((0,), (0,))), preferred_element_type=F32)


def _split_groups(x, w):
    return jnp.stack([x[g][:, p * w:(p + 1) * w]
                      for g in range(x.shape[0]) for p in range(x.shape[2] // w)], axis=0)


def _scan_kernel(r_ref, v_ref, nkk_ref, kd_ref, b_ref, lw_ref, s0_ref, o_ref, sf_ref, s_scr, *, reverse):
    c = pl.program_id(1)
    grp, ch, aw = lw_ref.shape
    w = s_scr.shape[-1]
    hp = w // A_HEAD_DIM
    n_groups = aw // w
    state_shape = s_scr.shape

    @pl.when(c == 0)
    def _():
        s_scr[...] = s0_ref[...].reshape(state_shape)

    lw = lw_ref[...]
    ri = lax.broadcasted_iota(I32, (ch, ch), 0)
    ci = lax.broadcasted_iota(I32, (ch, ch), 1)
    tri = jnp.where((ci >= ri) if reverse else (ci <= ri), 1.0, 0.0).astype(BF16)
    hi = lw.astype(BF16)
    rem = lw - hi.astype(F32)
    mid = rem.astype(BF16)
    lo = (rem - mid.astype(F32)).astype(BF16)
    li = jnp.stack([_dot(tri, hi[g]) + (_dot(tri, mid[g]) + _dot(tri, lo[g])) for g in range(grp)], axis=0)
    lt = li[:, 0:1] if reverse else li[:, ch - 1:ch]
    rho = 0.5 * lt
    e1 = jnp.exp(li - rho)
    e2 = jnp.exp(rho - li)
    er = jnp.exp(rho)
    a_rel = nkk_ref[...] * (e1 * jnp.exp(-lw))
    r_rel = r_ref[...] * e1
    b_rel = b_ref[...] * e2
    k_rel = kd_ref[...] * e2
    pairs = lambda x: _split_groups(x.astype(BF16), w)
    a_abs, r_abs, b_end, k_end = pairs(a_rel * er), pairs(r_rel * er), pairs(b_rel * er), pairs(k_rel * er)
    a_rel, r_rel, b_rel, k_rel = pairs(a_rel), pairs(r_rel), pairs(b_rel), pairs(k_rel)
    v = pairs(v_ref[...])
    decay = _split_groups(jnp.exp(lt), w)
    head_of_lane = lax.broadcasted_iota(I32, (1, 1, w), 2) // A_HEAD_DIM

    def blockdiag(x):
        zero = jnp.zeros_like(x)
        return jnp.concatenate([jnp.where(head_of_lane == j, x, zero) for j in range(hp)], axis=1)

    ri2 = lax.broadcasted_iota(I32, (1, ch, hp * ch), 1)
    ci2 = lax.broadcasted_iota(I32, (1, ch, hp * ch), 2) & (ch - 1)
    if reverse:
        incl2, strict2 = ci2 >= ri2, ci2 > ri2
    else:
        incl2, strict2 = ci2 <= ri2, ci2 < ri2
    eye2 = jnp.where(ri2 == ci2, 1.0, 0.0)
    lhs = jnp.concatenate([a_rel, r_rel], axis=1)
    with_b = _bdot_nt(lhs, blockdiag(b_rel))
    with_k = _bdot_nt(lhs, blockdiag(k_rel))
    a_ab = jnp.where(strict2, with_b[:, :ch], 0.0)
    a_ak = jnp.where(strict2, with_k[:, :ch], 0.0).astype(BF16)
    a_r = jnp.concatenate([jnp.where(incl2, with_b[:, ch:], 0.0), jnp.where(incl2, with_k[:, ch:], 0.0)],
                          axis=2).astype(BF16)
    inv = eye2
    for k in range(int(math.log2(ch))):
        p, q = (ci2 >> k, ri2 >> k) if reverse else (ri2 >> k, ci2 >> k)
        joins = ((p ^ q) * 4 + (p - q)) == 5
        lk = jnp.where(joins, a_ab, 0.0)
        if k == 0:
            inv = inv + lk
        else:
            invb = inv.astype(BF16)
            inv = inv + _bdot(_bdot(invb, blockdiag(lk.astype(BF16))).astype(BF16), blockdiag(invb))
    s = s_scr[...]
    from_state = _bdot_nt(jnp.concatenate([a_abs, r_abs], axis=1), s.astype(BF16))
    v_bd = blockdiag(v)
    x = from_state[:, :ch] + _bdot(a_ak, v_bd)
    u = _bdot(inv.astype(BF16), blockdiag(x.astype(BF16))).astype(BF16)
    o = from_state[:, ch:] + _bdot(a_r, jnp.concatenate([blockdiag(u), v_bd], axis=1))
    same_head = (lax.broadcasted_iota(I32, (1, w, w), 1) // A_HEAD_DIM) == head_of_lane
    upd = _bdot_tn(jnp.concatenate([u, v], axis=1), jnp.concatenate([b_end, k_end], axis=1))
    s_scr[...] = s * decay + jnp.where(same_head, upd, 0.0)
    for g in range(grp):
        o_ref[g] = jnp.concatenate([o[g * n_groups + p] for p in range(n_groups)], axis=1)

    @pl.when(c == pl.num_programs(1) - 1)
    def _():
        s_fin = s_scr[...]
        for g in range(grp):
            for p in range(n_groups):
                group = s_fin[g * n_groups + p]
                for j in range(hp):
                    part = slice(j * A_HEAD_DIM, (j + 1) * A_HEAD_DIM)
                    sf_ref[g, hp * p + j] = group[part, part]


def _wkv_scan(r, v, nkk, kd, b, lw, s0, reverse):
    bsz, t, aw = r.shape
    ch = SCAN_CHUNK
    assert t % ch == 0 and ch == A_HEAD_DIM
    nc = t // ch
    w = SCAN_PACK * A_HEAD_DIM
    n_groups = aw // w
    grp = SCAN_GROUP if bsz % SCAN_GROUP == 0 else 1
    tmap = (lambda i, c: (i, nc - 1 - c, 0)) if reverse else (lambda i, c: (i, c, 0))
    seq = pl.BlockSpec((grp, ch, aw), tmap)
    st_in = pl.BlockSpec((grp, n_groups, w, w), lambda i, c: (i, 0, 0, 0))
    st_out = pl.BlockSpec((grp,) + s0.shape[1:], lambda i, c: (i, 0, 0, 0))
    return pl.pallas_call(
        functools.partial(_scan_kernel, reverse=reverse),
        grid=(bsz // grp, nc),
        in_specs=[seq] * 6 + [st_in],
        out_specs=[seq, st_out],
        out_shape=[jax.ShapeDtypeStruct((bsz, t, aw), F32), jax.ShapeDtypeStruct(s0.shape, F32)],
        scratch_shapes=[pltpu.VMEM((grp * n_groups, w, w), F32)],
        compiler_params=_cparams("arbitrary", "arbitrary"),
        name="wkv_scan_bwd" if reverse else "wkv_scan_fwd",
    )(r, v, nkk, kd, b, lw, _group_states(s0, SCAN_PACK))


def _group_states(s, hp):
    b, h, d, _ = s.shape
    eye = jnp.eye(hp, dtype=s.dtype)[None, None, :, None, :, None]
    return (s.reshape(b, h // hp, hp, d, 1, d) * eye).reshape(b, h // hp, hp * d, hp * d)


def _denominator_lane(width):
    lane = lax.broadcasted_iota(I32, (1, width), 1)
    return jnp.where((lane & (HEAD_PAD - 1)) == V_HEAD, 1.0, 0.0)


def _rope_lanes(x, cos_t, sin_up, sin_dn):
    w = x.shape[1]
    half = QK_ROPE // 2
    return x * cos_t + pltpu.roll(x, half, 1) * sin_up + pltpu.roll(x, w - half, 1) * sin_dn


def _mla_prep_kernel(*refs, rope):
    if rope:
        (m_ref, gq_ref, gkv_ref, wuq_ref, wk_ref, wv_ref, p_ref,
         qc_ref, qu_ref, qd_ref, kc_ref, ku_ref, kd_ref, q_o, k_o, v_o, ckv_o) = refs
    else:
        (m_ref, gq_ref, gkv_ref, wuq_ref, wk_ref, wv_ref, p_ref, q_o, k_o, v_o, ckv_o) = refs
    m = m_ref[...]
    qn = _rms(m[:, :Q_LORA], gq_ref[...])
    ckv = _rms(m[:, Q_LORA:Q_LORA + KV_LORA], gkv_ref[...])
    kp = m[:, Q_LORA + KV_LORA:]
    q = _dot(qn.astype(BF16), wuq_ref[...]) * ATTN_SCALE
    if rope:
        tile = lambda ref: jnp.concatenate([ref[...]] * B_HEADS, axis=1)
        q = _rope_lanes(q, tile(qc_ref), tile(qu_ref), tile(qd_ref))
        kp = _rope_lanes(kp, kc_ref[...], ku_ref[...], kd_ref[...])
    cb = ckv.astype(BF16)
    q_o[...] = q.astype(BF16)
    k_o[...] = (_dot(cb, wk_ref[...]) + _dot(kp.astype(BF16), p_ref[...])).astype(BF16)
    v_o[...] = (_dot(cb, wv_ref[...]) + _denominator_lane(v_o.shape[1])).astype(BF16)
    ckv_o[...] = ckv


def _mla_prep(m2, g_q, g_kv, wuq_p, wk_p, wv_p, place, rope_tabs, tb, t):
    n = m2.shape[0]
    hw = B_HEADS * HEAD_PAD
    full = lambda a: pl.BlockSpec(a.shape, lambda i: (0, 0))
    ins = [m2, g_q, g_kv, wuq_p, wk_p, wv_p, place]
    specs = [pl.BlockSpec((tb, M_COLS), lambda i: (i, 0))] + [full(a) for a in ins[1:]]
    if rope_tabs is not None:
        nbt = t // tb
        ins += list(rope_tabs)
        specs += [pl.BlockSpec((tb, HEAD_PAD), lambda i: (i % nbt, 0))] * 6
    big = pl.BlockSpec((tb, hw), lambda i: (i, 0))
    return pl.pallas_call(
        functools.partial(_mla_prep_kernel, rope=rope_tabs is not None),
        grid=(n // tb,),
        in_specs=specs,
        out_specs=[big, big, big, pl.BlockSpec((tb, KV_LORA), lambda i: (i, 0))],
        out_shape=[jax.ShapeDtypeStruct((n, hw), BF16)] * 3 + [jax.ShapeDtypeStruct((n, KV_LORA), F32)],
        compiler_params=_cparams("arbitrary"),
        name="mla_prep",
    )(*ins)


def _kv_up_kernel(ckv_ref, kp_ref, wk_ref, wv_ref, p_ref, k_o, v_o):
    cb = ckv_ref[...].astype(BF16)
    k_o[...] = (_dot(cb, wk_ref[...]) + _dot(kp_ref[...].astype(BF16), p_ref[...])).astype(BF16)
    v_o[...] = (_dot(cb, wv_ref[...]) + _denominator_lane(v_o.shape[1])).astype(BF16)


def _kv_up(ckv2, kpe_pad, wk_p, wv_p, place, tb):
    n = ckv2.shape[0]
    hw = B_HEADS * HEAD_PAD
    full = lambda a: pl.BlockSpec(a.shape, lambda i: (0, 0))
    big = pl.BlockSpec((tb, hw), lambda i: (i, 0))
    return pl.pallas_call(
        _kv_up_kernel,
        grid=(n // tb,),
        in_specs=[pl.BlockSpec((tb, KV_LORA), lambda i: (i, 0)),
                  pl.BlockSpec((tb, HEAD_PAD), lambda i: (i, 0)),
                  full(wk_p), full(wv_p), full(place)],
        out_specs=[big, big],
        out_shape=[jax.ShapeDtypeStruct((n, hw), BF16)] * 2,
        compiler_params=_cparams("arbitrary"),
        name="kv_up",
    )(ckv2, kpe_pad, wk_p, wv_p, place)


def _attn_kernel(*refs, two):
    if two:
        q_ref, k1_ref, v1_ref, k2_ref, v2_ref, o_ref = refs
    else:
        q_ref, k1_ref, v1_ref, o_ref = refs
    for h in range(B_HEADS):
        hs = slice(h * HEAD_PAD, (h + 1) * HEAD_PAD)
        q = q_ref[0, :, hs]
        s1 = _dot_nt(q, k1_ref[0, :, hs])
        mx = jnp.max(s1, axis=-1, keepdims=True)
        if two:
            s2 = _dot_nt(q, k2_ref[0, :, hs])
            mx = jnp.maximum(mx, jnp.max(s2, axis=-1, keepdims=True))
        acc = _dot(jnp.exp(s1 - mx).astype(BF16), v1_ref[0, :, hs])
        if two:
            acc = acc + _dot(jnp.exp(s2 - mx).astype(BF16), v2_ref[0, :, hs])
        o_ref[0, :, hs] = (acc / acc[:, V_HEAD:V_HEAD + 1]).astype(BF16)


def _attention(q3, k1, v1, k2, v2, tq):
    b, t, hw = q3.shape
    two = k2 is not None
    qspec = pl.BlockSpec((1, tq, hw), lambda i, j: (i, j, 0))
    kv = lambda a: pl.BlockSpec((1, a.shape[1], hw), lambda i, j: (i, 0, 0))
    ins = [q3, k1, v1] + ([k2, v2] if two else [])
    return pl.pallas_call(
        functools.partial(_attn_kernel, two=two),
        grid=(b, t // tq),
        in_specs=[qspec] + [kv(a) for a in ins[1:]],
        out_specs=qspec,
        out_shape=jax.ShapeDtypeStruct((b, t, hw), BF16),
        compiler_params=_cparams("arbitrary", "arbitrary"),
        name="attention",
    )(*ins)


def _post_mix_kernel(of_ref, ob_ref, bonus_ref, g_ref, att_ref, gates_ref, x_ref,
                     g1_ref, sc2_ref, sh2_ref, lnw_ref, lnb_ref, bd_ref, wa_ref, wb_ref, wo_ref,
                     gpost_ref, gpre_ref, wr_ref, x1_o, h2_o, aff_o):
    bd = bd_ref[...]
    inv_n = 1.0 / A_HEAD_DIM
    o = of_ref[...] + ob_ref[...]
    mu = _dot_x2(o, bd) * inv_n
    oc = o - mu
    var = _dot((oc * oc).astype(BF16), bd) * inv_n
    on = (oc * lax.rsqrt(var + GN_EPS)) * lnw_ref[...] + lnb_ref[...]
    ya = _dot(((on + bonus_ref[...]) * g_ref[...]).astype(BF16), wa_ref[...])
    yb = _dot(att_ref[...], wb_ref[...])
    d = ya.shape[1]
    gates = gates_ref[...].astype(F32)
    mix = jax.nn.sigmoid(gates[:, :d]) * ya + jax.nn.sigmoid(gates[:, d:]) * yb
    y = _dot(mix.astype(BF16), wo_ref[...])
    x1 = x_ref[...] + g1_ref[0] * _rms(y, gpost_ref[...])
    x1_o[...] = x1
    h2 = _rms(x1, gpre_ref[...]) * (1.0 + sc2_ref[0]) + sh2_ref[0]
    h2_o[...] = h2.astype(BF16)
    hh, hl = _split2(h2)
    wh, wl = _split2(wr_ref[...])
    logits = _dot_nt(wh, hh) + (_dot_nt(wh, hl) + _dot_nt(wl, hh))
    z = jnp.exp(logits - jnp.max(logits, axis=0, keepdims=True))
    aff_o[...] = z / jnp.sum(z, axis=0, keepdims=True)


def _post_mix(o_f, o_b, bonus, g, att, gates, x2, g1, sc2, sh2, ln_w, ln_b, bd, wa, wb, wo,
              g_post, g_pre, w_router_t, tb, blocks_per_batch):
    n, d = x2.shape
    tok = lambda a: pl.BlockSpec((tb, a.shape[1]), lambda i: (i, 0))
    full = lambda a: pl.BlockSpec(a.shape, lambda i: (0, 0))
    ms = lambda a: _mod_spec(a, blocks_per_batch)
    return pl.pallas_call(
        _post_mix_kernel,
        grid=(n // tb,),
        in_specs=[tok(o_f), tok(o_b), tok(bonus), tok(g), tok(att), tok(gates), tok(x2),
                  ms(g1), ms(sc2), ms(sh2), full(ln_w), full(ln_b), full(bd), full(wa), full(wb),
                  full(wo), full(g_post), full(g_pre), full(w_router_t)],
        out_specs=[pl.BlockSpec((tb, d), lambda i: (i, 0)),
                   pl.BlockSpec((tb, d), lambda i: (i, 0)),
                   pl.BlockSpec((N_EXPERTS, tb), lambda i: (0, i))],
        out_shape=[jax.ShapeDtypeStruct((n, d), F32),
                   jax.ShapeDtypeStruct((n, d), BF16),
                   jax.ShapeDtypeStruct((N_EXPERTS, n), F32)],
        compiler_params=_cparams("arbitrary"),
        name="post_mix",
    )(o_f, o_b, bonus, g, att, gates, x2, g1, sc2, sh2, ln_w, ln_b, bd, wa, wb, wo,
      g_post, g_pre, w_router_t)


def _select_kernel(a_ref, pos_ref, sel_ref, *, cap, n_chunks):
    a = a_ref[...]
    rows = a.shape[0]
    a3 = a.reshape(N_EXPERTS, n_chunks, LANES)

    def count(mask):
        c = jnp.sum(jnp.where(mask, 1.0, 0.0), axis=2, keepdims=True)
        return jnp.sum(c, axis=1, keepdims=True)

    def body(i, thr):
        cand = thr | jnp.left_shift(jnp.int32(1), 30 - i)
        return jnp.where(count(a3 >= pltpu.bitcast(cand, F32)) >= cap, cand, thr)

    thr = pltpu.bitcast(lax.fori_loop(0, 31, body, jnp.zeros((N_EXPERTS, 1, LANES), I32)), F32)
    gt = a3 > thr
    eq = a3 == thr
    need = cap - count(gt)

    li = lax.broadcasted_iota(I32, (LANES, LANES), 0)
    lj = lax.broadcasted_iota(I32, (LANES, LANES), 1)
    lane_before = jnp.where(li < lj, 1.0, 0.0).astype(BF16)
    lane_all = jnp.ones((LANES, LANES), BF16)
    ci = lax.broadcasted_iota(I32, (n_chunks, n_chunks), 0)
    cj = lax.broadcasted_iota(I32, (n_chunks, n_chunks), 1)
    chunk_before = jnp.where(cj < ci, 1.0, 0.0).astype(BF16)

    def prefix(flags3):
        f2 = flags3.reshape(rows, LANES).astype(BF16)
        within = _dot(f2, lane_before).reshape(N_EXPERTS, n_chunks, LANES)
        tot = _dot(f2, lane_all).astype(BF16).reshape(N_EXPERTS, n_chunks, LANES)
        offs = [_dot(chunk_before, tot[e]) for e in range(N_EXPERTS)]
        return within + jnp.stack(offs, axis=0)

    eqf = jnp.where(eq, 1.0, 0.0)
    tie_ok = jnp.where(prefix(eqf) < need, eqf, 0.0)
    sel = jnp.where(gt, 1.0, tie_ok)
    pos_ref[...] = prefix(sel).reshape(rows, LANES).astype(I32)
    sel_ref[...] = sel.reshape(rows, LANES).astype(I32)


def _select(aff_t, cap):
    e, n = aff_t.shape
    n_chunks = n // LANES
    rows = e * n_chunks
    a2 = aff_t.reshape(rows, LANES)
    spec = pl.BlockSpec((rows, LANES), lambda i: (0, 0))
    pos, sel = pl.pallas_call(
        functools.partial(_select_kernel, cap=cap, n_chunks=n_chunks),
        grid=(1,),
        in_specs=[spec],
        out_specs=[spec, spec],
        out_shape=[jax.ShapeDtypeStruct((rows, LANES), I32)] * 2,
        compiler_params=_cparams("arbitrary"),
        name="expert_select",
    )(a2)
    return pos.reshape(e, n), sel.reshape(e, n)


def _ffn_kernel(lo_ref, hi_ref, slot_ref, aff_ref, h_ref, wg_ref, wu_ref, wd_ref, y_ref, acc_ref, wacc_ref,
                *, n_tiles, sub):
    e = pl.program_id(0)
    j = pl.program_id(1)
    rows = acc_ref.shape[0]
    n_blocks, tb = slot_ref.shape[1], slot_ref.shape[2]
    n_sub = rows // sub
    acc_ref[...] = jnp.zeros_like(acc_ref)
    wacc_ref[...] = jnp.zeros_like(wacc_ref)
    first = [lo_ref[(e * n_tiles + j) * n_sub + si] for si in range(n_sub)]
    last = [hi_ref[(e * n_tiles + j) * n_sub + si] for si in range(n_sub)]
    trips = functools.reduce(jnp.maximum, [hi - lo for lo, hi in zip(first, last)])
    row0 = lax.broadcasted_iota(I32, (sub, tb), 0) + j * rows

    def gather_step(i, carry):
        for si in range(n_sub):
            b = first[si] + i
            live = b < last[si]
            b = jnp.minimum(b, n_blocks - 1)
            part = slice(si * sub, (si + 1) * sub)
            hit = (row0 + si * sub) == jnp.where(live, slot_ref[0, pl.ds(b, 1), :], -1)
            tokens = h_ref[pl.ds(pl.multiple_of(b * tb, tb), tb), :]
            acc_ref[part, :] += _dot(jnp.where(hit, 1.0, 0.0).astype(BF16), tokens)
            wacc_ref[part, :] += jnp.sum(jnp.where(hit, aff_ref[0, pl.ds(b, 1), :], 0.0), axis=1, keepdims=True)
        return carry

    lax.fori_loop(0, trips, gather_step, 0)
    xe = acc_ref[...].astype(BF16)
    gate = _dot(xe, wg_ref[0])
    up = _dot(xe, wu_ref[0])
    hid = (gate * jax.nn.sigmoid(gate)) * up
    y_ref[0] = (_dot(hid.astype(BF16), wd_ref[0]) * wacc_ref[...]).astype(BF16)


def _expert_ffn(pos, slot, aff_t, h2, wg, wu, wd, cap, tb, rt):
    n, d = h2.shape
    nb = n // tb
    e, _, f = wg.shape
    n_tiles = cap // rt
    sub = min(GATHER_ROWS, rt)
    starts = jnp.concatenate([pos[:, ::tb], jnp.full((e, 1), cap, I32)], axis=1)
    edges = jnp.arange(cap // sub, dtype=I32) * sub
    lo = jnp.sum((starts[:, None, 1:] <= edges[None, :, None]).astype(I32), axis=2)
    hi = jnp.sum((starts[:, None, :-1] < (edges + sub)[None, :, None]).astype(I32), axis=2)
    grid_spec = pltpu.PrefetchScalarGridSpec(
        num_scalar_prefetch=2,
        grid=(e, n_tiles),
        in_specs=[pl.BlockSpec((1, nb, tb), lambda ei, j, lo, hi: (ei, 0, 0)),
                  pl.BlockSpec((1, nb, tb), lambda ei, j, lo, hi: (ei, 0, 0)),
                  pl.BlockSpec((n, d), lambda ei, j, lo, hi: (0, 0), pipeline_mode=pl.Buffered(1)),
                  pl.BlockSpec((1, d, f), lambda ei, j, lo, hi: (ei, 0, 0)),
                  pl.BlockSpec((1, d, f), lambda ei, j, lo, hi: (ei, 0, 0)),
                  pl.BlockSpec((1, f, d), lambda ei, j, lo, hi: (ei, 0, 0))],
        out_specs=pl.BlockSpec((1, rt, d), lambda ei, j, lo, hi: (ei, j, 0)),
        scratch_shapes=[pltpu.VMEM((rt, d), F32), pltpu.VMEM((rt, 1), F32)],
    )
    return pl.pallas_call(
        functools.partial(_ffn_kernel, n_tiles=n_tiles, sub=sub),
        grid_spec=grid_spec,
        out_shape=jax.ShapeDtypeStruct((e, cap, d), BF16),
        compiler_params=_cparams("arbitrary", "arbitrary"),
        name="expert_ffn",
    )(lo.reshape(-1), hi.reshape(-1), slot.reshape(e, nb, tb), aff_t.reshape(e, nb, tb), h2, wg, wu, wd)


def _combine_kernel(t0_ref, fetch_ref, need_ref, slot_ref, x_ref, g2_ref, gp_ref, *rest, n_win):
    y_refs, o_ref, acc_ref = rest[:-2], rest[-2], rest[-1]
    blk = pl.program_id(0)
    n_blk = pl.num_programs(0)
    tb = slot_ref.shape[0]
    rows = y_refs[0].shape[1]
    lane = lax.broadcasted_iota(I32, (tb, rows), 1)

    def picked(e, k):
        slot = slot_ref[:, e:e + 1] - (t0_ref[blk * N_EXPERTS + e] + k) * rows
        return _dot(jnp.where(lane == slot, 1.0, 0.0).astype(BF16), y_refs[e * n_win + k][0])

    acc = picked(0, 0)
    for e in range(1, N_EXPERTS):
        acc = acc + picked(e, 0)
    acc_ref[...] = acc
    for k in range(1, n_win):
        for e in range(N_EXPERTS):
            @pl.when(need_ref[(k * n_blk + blk) * N_EXPERTS + e] != 0)
            def _():
                acc_ref[...] += picked(e, k)
    o_ref[...] = x_ref[...] + g2_ref[0] * _rms(acc_ref[...], gp_ref[...])


def _combine_windows(pos, tb, rt, cap):
    n_tiles = cap // rt
    lo = pos[:, ::tb]
    hi = jnp.concatenate([lo[:, 1:], jnp.full((lo.shape[0], 1), cap, I32)], axis=1)
    t0 = jnp.minimum(lo // rt, n_tiles - 1)
    t_last = jnp.where(hi > lo, (hi - 1) // rt, t0)
    fetch, need = [t0], [jnp.ones_like(t0)]
    for k in range(1, tb // rt + 1):
        need.append(t_last >= t0 + k)
        fetch.append(lax.cummax(jnp.where(need[k], t0 + k, 0), axis=1))
    flat = lambda a: a.T.reshape(-1).astype(I32)
    return flat(t0), jnp.concatenate([flat(f) for f in fetch]), jnp.concatenate([flat(f) for f in need])


def _combine(windows, slot_t, ye, x1, g2, g_post, tb, rt, blocks_per_batch):
    t0, fetch, need = windows
    n, d = x1.shape
    e, cap, _ = ye.shape
    nb = n // tb
    n_win = tb // rt + 1
    if g2.shape[0] == 1:
        g2_spec = pl.BlockSpec((1, 1, d), lambda b, t, f, nd: (0, 0, 0))
    else:
        g2_spec = pl.BlockSpec((1, 1, d), lambda b, t, f, nd: (b // blocks_per_batch, 0, 0))

    def window(ei, k):
        return pl.BlockSpec((1, rt, d), lambda b, t, f, nd: (ei, f[(k * nb + b) * e + ei], 0))

    grid_spec = pltpu.PrefetchScalarGridSpec(
        num_scalar_prefetch=3,
        grid=(nb,),
        in_specs=[pl.BlockSpec((tb, e), lambda b, t, f, nd: (b, 0)),
                  pl.BlockSpec((tb, d), lambda b, t, f, nd: (b, 0)),
                  g2_spec,
                  pl.BlockSpec((1, d), lambda b, t, f, nd: (0, 0))]
                 + [window(ei, k) for ei in range(e) for k in range(n_win)],
        out_specs=pl.BlockSpec((tb, d), lambda b, t, f, nd: (b, 0)),
        scratch_shapes=[pltpu.VMEM((tb, d), F32)],
    )
    return pl.pallas_call(
        functools.partial(_combine_kernel, n_win=n_win),
        grid_spec=grid_spec,
        out_shape=jax.ShapeDtypeStruct((n, d), F32),
        compiler_params=_cparams("arbitrary"),
        name="moe_combine",
    )(t0, fetch, need, slot_t, x1, g2, g_post, *([ye] * (e * n_win)))


def _rope_tables(t):
    half = QK_ROPE // 2
    pos = jnp.arange(t)
    row = (pos // GRID_W).astype(F32)
    col = (pos % GRID_W).astype(F32)
    inv = ROPE_BASE ** (-jnp.arange(0, half, 2, dtype=F32) / half)
    ang = jnp.concatenate([row[:, None] * inv, col[:, None] * inv], axis=-1)
    cos, sin = jnp.cos(ang), jnp.sin(ang)

    def tabs(first_lane):
        left, right = first_lane, HEAD_PAD - first_lane - 2 * half
        fill = lambda value, width: jnp.full((t, width), value, F32)
        c = jnp.concatenate([fill(1.0, left), cos, cos, fill(1.0, right)], axis=1)
        up = jnp.concatenate([fill(0.0, left + half), sin, fill(0.0, right)], axis=1)
        dn = jnp.concatenate([fill(0.0, left), -sin, fill(0.0, half + right)], axis=1)
        return c, up, dn

    return tabs(QK_NOPE + QK_ROPE) + tabs(QK_ROPE)


def _layout_weights(w_in, w_uq, w_ukv, w_branch_b, w_up, a_up):
    d = w_in.shape[0]
    deint = jnp.concatenate([jnp.arange(0, QK_ROPE, 2), jnp.arange(1, QK_ROPE, 2)])
    kpe0 = RWKV_COLS + Q_LORA + KV_LORA
    kpe_cols = w_in[:, kpe0:kpe0 + QK_ROPE]
    w_in_p = jnp.concatenate(
        [w_in[:, :kpe0 + QK_ROPE], kpe_cols[:, deint],
         jnp.zeros((d, M_COLS - MLA_COLS - QK_ROPE), F32), w_in[:, RWKV_COLS + MLA_COLS:]], axis=1)
    uq = w_uq.reshape(Q_LORA, B_HEADS, QK_NOPE + QK_ROPE)
    wuq_p = jnp.concatenate([uq, uq[:, :, QK_NOPE:][:, :, deint]], axis=2).reshape(Q_LORA, B_HEADS * HEAD_PAD)
    ukv = w_ukv.reshape(KV_LORA, B_HEADS, QK_NOPE + V_HEAD)
    zpad = jnp.zeros((KV_LORA, B_HEADS, HEAD_PAD - QK_NOPE), F32)
    wk_p = jnp.concatenate([ukv[:, :, :QK_NOPE], zpad], axis=2).reshape(KV_LORA, B_HEADS * HEAD_PAD)
    wv_p = jnp.concatenate([ukv[:, :, QK_NOPE:], zpad], axis=2).reshape(KV_LORA, B_HEADS * HEAD_PAD)
    wb = w_branch_b.reshape(B_HEADS, V_HEAD, d)
    wb_p = jnp.concatenate([wb, jnp.zeros((B_HEADS, HEAD_PAD - V_HEAD, d), F32)], axis=1).reshape(B_HEADS * HEAD_PAD, d)
    eye = jnp.eye(QK_ROPE, dtype=F32)
    z = jnp.zeros((QK_ROPE, QK_ROPE), F32)
    head_raw = jnp.concatenate([jnp.zeros((QK_ROPE, QK_NOPE), F32), eye, z], axis=1)
    head_rot = jnp.concatenate([jnp.zeros((QK_ROPE, QK_NOPE), F32), z, eye], axis=1)
    zrows = jnp.zeros((HEAD_PAD - 2 * QK_ROPE, B_HEADS * HEAD_PAD), F32)
    zr = jnp.zeros((QK_ROPE, B_HEADS * HEAD_PAD), F32)
    place_raw = jnp.concatenate([jnp.tile(head_raw, (1, B_HEADS)), zr, zrows], axis=0)
    place_rot = jnp.concatenate([zr, jnp.tile(head_rot, (1, B_HEADS)), zrows], axis=0)
    zl = jnp.zeros((DECAY_LORA, A_WIDTH), F32)
    wup_p = jnp.stack([jnp.concatenate([w_up[0], zl]), jnp.concatenate([zl, w_up[1]])])
    aup_p = jnp.stack([jnp.concatenate([a_up[0], zl]), jnp.concatenate([zl, a_up[1]])])
    bf = lambda a: a.astype(BF16)
    return (bf(w_in_p), bf(wuq_p), bf(wk_p), bf(wv_p), bf(wb_p), bf(place_raw), bf(place_rot),
            bf(wup_p), bf(aup_p))


def _group_layer(x, mods, lw, s0_f, s0_b, ctx_kv, rope_tabs):
    bsz, t, d = x.shape
    n = bsz * t
    tb = min(TOKEN_BLOCK, t)
    tw = min(ROW_BLOCK, t)
    bpb = t // tb
    bpw = t // tw
    sh1, sc1, g1, sh2, sc2, g2 = mods
    x2 = x.reshape(n, d)
    u, m, gates, edges = _inproj(x2, sc1, sh1, lw["g_pre_mix"], lw["w_in_p"], tw, bpw)
    prep = _rwkv_prep(u.reshape(bsz, t, RWKV_COLS), edges, lw["shift_w"], lw["w0"], lw["a0"], lw["wup_p"],
                      lw["aup_p"], lw["g_up"], lw["k_k"], lw["k_a"], lw["r_k"], lw["bd"], tw)
    r, v, nkk, kd_f, kd_b, b_f, b_b, lw_f, lw_b, g, bonus = prep
    o_f, s_f = _wkv_scan(r, v, nkk, kd_f, b_f, lw_f, s0_f, reverse=False)
    o_b, s_b = _wkv_scan(r, v, nkk, kd_b, b_b, lw_b, s0_b, reverse=True)
    place = lw["place_raw"] if rope_tabs is None else lw["place_rot"]
    q, k, vv, ckv = _mla_prep(m, lw["g_qnorm"], lw["g_kvnorm"], lw["wuq_p"], lw["wk_p"], lw["wv_p"],
                              place, rope_tabs, tw, t)
    hw = B_HEADS * HEAD_PAD
    k2, v2 = ctx_kv if ctx_kv is not None else (None, None)
    att = _attention(q.reshape(bsz, t, hw), k.reshape(bsz, t, hw), vv.reshape(bsz, t, hw), k2, v2, tw)
    two = lambda a: a.reshape(n, a.shape[-1])
    x1, h2, aff_t = _post_mix(two(o_f), two(o_b), two(bonus), two(g), att.reshape(n, hw), gates, x2,
                              g1, sc2, sh2, lw["ln_x_w"], lw["ln_x_b"], lw["bd"], lw["w_branch_a"],
                              lw["wb_p"], lw["w_out"], lw["g_post_mix"], lw["g_pre_ffn"],
                              lw["w_router_t"], tw, bpw)
    cap = CAPACITY_FACTOR * n // N_EXPERTS
    rt = min(EXPERT_TILE, cap)
    pos, sel = _select(aff_t, cap)
    slot = jnp.where(sel > 0, pos, -1)
    ye = _expert_ffn(pos, slot, aff_t, h2, lw["w_exp_gate"], lw["w_exp_up"], lw["w_exp_down"], cap,
                     min(GATHER_BLOCK, n), min(FFN_TILE, cap))
    out = _combine(_combine_windows(pos, tb, rt, cap), slot.T, ye, x1, g2, lw["g_post_ffn"], tb, rt, bpb)
    kpe = m[:, Q_LORA + KV_LORA:Q_LORA + KV_LORA + QK_ROPE]
    return out.reshape(bsz, t, d), (ckv.reshape(bsz, t, KV_LORA), kpe.reshape(bsz, t, QK_ROPE), s_f, s_b)


def kernel(x_prompt, x_sample, cache_ckv, cache_kpe, state_wkv_fwd, state_wkv_bwd, c, c_ctx,
           w_mod, b_mod, g_pre_mix, g_post_mix, g_pre_ffn, g_post_ffn, w_in, shift_w,
           w0, w_up, a0, a_up, g_up, k_k, k_a, r_k, ln_x_w, ln_x_b, w_branch_a,
           g_qnorm, w_uq, g_kvnorm, w_ukv, w_branch_b, w_out,
           w_router, w_exp_gate, w_exp_up, w_exp_down):
    depth = w_mod.shape[0]
    d = x_prompt.shape[-1]
    dec_b, dec_t = x_sample.shape[0], x_sample.shape[1]
    xp, xs = x_prompt, x_sample
    c_rows = jnp.concatenate([c, c_ctx[None, :],
                              jnp.zeros((-(dec_b + 1) % 8, d), F32)], axis=0)
    rope_tabs = _rope_tables(dec_t)
    ii = lax.broadcasted_iota(I32, (A_WIDTH, A_WIDTH), 0) // A_HEAD_DIM
    jj = lax.broadcasted_iota(I32, (A_WIDTH, A_WIDTH), 1) // A_HEAD_DIM
    bd = (ii == jj).astype(BF16)
    row = lambda a: a.reshape(1, -1)
    bf = lambda a: a.astype(BF16)
    ckv_l, kpe_l, sf_l, sb_l = [], [], [], []
    for l in range(depth):
        (w_in_p, wuq_p, wk_p, wv_p, wb_p, place_raw, place_rot, wup_p, aup_p) = _layout_weights(
            w_in[l], w_uq[l], w_ukv[l], w_branch_b[l], w_up[l], a_up[l])
        lw = {
            "g_pre_mix": row(g_pre_mix[l]), "g_post_mix": row(g_post_mix[l]),
            "g_pre_ffn": row(g_pre_ffn[l]), "g_post_ffn": row(g_post_ffn[l]),
            "w_in_p": w_in_p, "shift_w": shift_w[l], "w0": w0[l], "a0": a0[l],
            "wup_p": wup_p, "aup_p": aup_p, "g_up": bf(g_up[l]),
            "k_k": row(k_k[l]), "k_a": row(k_a[l]), "r_k": row(r_k[l]), "bd": bd,
            "ln_x_w": row(ln_x_w[l]), "ln_x_b": row(ln_x_b[l]), "w_branch_a": bf(w_branch_a[l]),
            "g_qnorm": row(g_qnorm[l]), "g_kvnorm": row(g_kvnorm[l]),
            "wuq_p": wuq_p, "wk_p": wk_p, "wv_p": wv_p, "wb_p": wb_p,
            "place_raw": place_raw, "place_rot": place_rot,
            "w_out": bf(w_out[l]), "w_router_t": w_router[l].T,
            "w_exp_gate": bf(w_exp_gate[l]), "w_exp_up": bf(w_exp_up[l]), "w_exp_down": bf(w_exp_down[l]),
        }
        mod = _modulation(c_rows, w_mod[l], b_mod[l])
        mods_lat = [mod[:dec_b, i * d:(i + 1) * d].reshape(dec_b, 1, d) for i in range(6)]
        mods_ctx = [mod[dec_b:dec_b + 1, i * d:(i + 1) * d].reshape(1, 1, d) for i in range(6)]
        zeros_state = jnp.zeros((xp.shape[0], A_HEADS, A_HEAD_DIM, A_HEAD_DIM), F32)
        xp, (ckv, kpe, s_f, s_b) = _group_layer(xp, mods_ctx, lw, zeros_state, zeros_state, None, None)
        ckv_l.append(ckv)
        kpe_l.append(kpe)
        sf_l.append(s_f)
        sb_l.append(s_b)
        past = cache_ckv.shape[2]
        kpe_pad = jnp.concatenate(
            [cache_kpe[:, l], jnp.zeros((dec_b, past, HEAD_PAD - QK_ROPE), F32)], axis=-1)
        k_ctx, v_ctx = _kv_up(cache_ckv[:, l].reshape(dec_b * past, KV_LORA),
                              kpe_pad.reshape(dec_b * past, HEAD_PAD), wk_p, wv_p, place_raw,
                              min(TOKEN_BLOCK, past))
        hw = B_HEADS * HEAD_PAD
        ctx_kv = (k_ctx.reshape(dec_b, past, hw), v_ctx.reshape(dec_b, past, hw))
        xs, _ = _group_layer(xs, mods_lat, lw, state_wkv_fwd[:, l], state_wkv_bwd[:, l], ctx_kv, rope_tabs)
    return (xp, xs, jnp.stack(ckv_l, axis=1), jnp.stack(kpe_l, axis=1),
            jnp.stack(sf_l, axis=1), jnp.stack(sb_l, axis=1))
```

```python
import functools
import math

import jax
import jax.numpy as jnp
from jax import lax
from jax.experimental import pallas as pl
from jax.experimental.pallas import tpu as pltpu

F32 = jnp.float32
BF16 = jnp.bfloat16
I32 = jnp.int32

GRID_W = 64
A_HEADS = 8
A_HEAD_DIM = 64
A_WIDTH = A_HEADS * A_HEAD_DIM
DECAY_LORA = 64
ICLR_LORA = 64
GATE_LORA = 128
DECAY_SCALE = 0.6065306597126334
GN_EPS = 64e-5
B_HEADS = 8
Q_LORA = 256
KV_LORA = 128
QK_NOPE = 64
QK_ROPE = 32
V_HEAD = 64
ROPE_BASE = 10000.0
ATTN_SCALE = 1.0 / math.sqrt(QK_NOPE + QK_ROPE)
N_EXPERTS = 16
CAPACITY_FACTOR = 2
EPS = 1e-6
RWKV_COLS = 3 * A_WIDTH + 2 * DECAY_LORA + 2 * ICLR_LORA + GATE_LORA
MLA_COLS = Q_LORA + KV_LORA + QK_ROPE

LANES = 128
HEAD_PAD = 128
M_COLS = 512
VMEM_LIMIT = 56 * 1024 * 1024

SCAN_CHUNK = 64
SCAN_GROUP = 8
SCAN_PACK = 2
TOKEN_BLOCK = 256
ROW_BLOCK = 512
EXPERT_TILE = 256
FFN_TILE = 256
GATHER_BLOCK = 512
GATHER_ROWS = 128


def _cparams(*sem):
    return pltpu.CompilerParams(dimension_semantics=sem, vmem_limit_bytes=VMEM_LIMIT)


def _dot(a, b):
    return jnp.dot(a, b, preferred_element_type=F32)


def _dot_nt(a, b):
    return lax.dot_general(a, b, (((1,), (1,)), ((), ())), preferred_element_type=F32)


def _dot_tn(a, b):
    return lax.dot_general(a, b, (((0,), (0,)), ((), ())), preferred_element_type=F32)


def _split2(x):
    hi = x.astype(BF16)
    lo = (x - hi.astype(F32)).astype(BF16)
    return hi, lo


def _dot_x2(a, b_bf16):
    hi, lo = _split2(a)
    return _dot(hi, b_bf16) + _dot(lo, b_bf16)


def _dot_f32(a, b):
    ah, al = _split2(a)
    bh, bl = _split2(b)
    return _dot(ah, bh) + (_dot(ah, bl) + _dot(al, bh))


def _rms(x, g):
    return (x * lax.rsqrt(jnp.mean(x * x, axis=-1, keepdims=True) + EPS)) * g


def _mod_kernel(c_ref, w_ref, b_ref, o_ref):
    c = c_ref[...]
    s = c * jax.nn.sigmoid(c)
    o_ref[...] = _dot_f32(s, w_ref[...]) + b_ref[...]


def _modulation(c_rows, w_mod, b_mod):
    rows, d = c_rows.shape
    n_out = w_mod.shape[1]
    tn = n_out // 8
    return pl.pallas_call(
        _mod_kernel,
        grid=(n_out // tn,),
        in_specs=[pl.BlockSpec((rows, d), lambda j: (0, 0)),
                  pl.BlockSpec((d, tn), lambda j: (0, j)),
                  pl.BlockSpec((1, tn), lambda j: (0, j))],
        out_specs=pl.BlockSpec((rows, tn), lambda j: (0, j)),
        out_shape=jax.ShapeDtypeStruct((rows, n_out), F32),
        compiler_params=_cparams("arbitrary"),
        name="modulation",
    )(c_rows, w_mod, b_mod.reshape(1, n_out))


def _inproj_kernel(x_ref, sc_ref, sh_ref, g_ref, w_ref, u_ref, m_ref, gt_ref, edge_ref):
    h = _rms(x_ref[...], g_ref[...]) * (1.0 + sc_ref[0]) + sh_ref[0]
    hb = h.astype(BF16)
    u = _dot(hb, w_ref[:, :RWKV_COLS])
    u_ref[...] = u
    edge_ref[0, 0:1, :] = u[0:1]
    edge_ref[0, 1:2, :] = u[u.shape[0] - 1:]
    m_ref[...] = _dot(hb, w_ref[:, RWKV_COLS:RWKV_COLS + M_COLS])
    gt_ref[...] = _dot(hb, w_ref[:, RWKV_COLS + M_COLS:]).astype(gt_ref.dtype)


def _mod_spec(mod, blocks_per_batch):
    d = mod.shape[-1]
    if mod.shape[0] == 1:
        return pl.BlockSpec((1, 1, d), lambda i: (0, 0, 0))
    return pl.BlockSpec((1, 1, d), lambda i: (i // blocks_per_batch, 0, 0))


def _inproj(x2, sc, sh, g, w_in_p, tb, blocks_per_batch):
    n, d = x2.shape
    cols = w_in_p.shape[1]
    gate_cols = cols - RWKV_COLS - M_COLS
    return pl.pallas_call(
        _inproj_kernel,
        grid=(n // tb,),
        in_specs=[pl.BlockSpec((tb, d), lambda i: (i, 0)),
                  _mod_spec(sc, blocks_per_batch), _mod_spec(sh, blocks_per_batch),
                  pl.BlockSpec((1, d), lambda i: (0, 0)),
                  pl.BlockSpec((d, cols), lambda i: (0, 0))],
        out_specs=[pl.BlockSpec((tb, RWKV_COLS), lambda i: (i, 0)),
                   pl.BlockSpec((tb, M_COLS), lambda i: (i, 0)),
                   pl.BlockSpec((tb, gate_cols), lambda i: (i, 0)),
                   pl.BlockSpec((1, 2, RWKV_COLS), lambda i: (i, 0, 0))],
        out_shape=[jax.ShapeDtypeStruct((n, RWKV_COLS), F32),
                   jax.ShapeDtypeStruct((n, M_COLS), F32),
                   jax.ShapeDtypeStruct((n, gate_cols), BF16),
                   jax.ShapeDtypeStruct((n // tb, 2, RWKV_COLS), F32)],
        compiler_params=_cparams("arbitrary"),
        name="inproj",
    )(x2, sc, sh, g, w_in_p)


def _rwkv_prep_kernel(u_ref, hp_ref, hn_ref, sw_ref, w0_ref, a0_ref, wup_ref, aup_ref, gup_ref,
                      kk_ref, ka_ref, rk_ref, bd_ref,
                      r_o, v_o, nkk_o, kdf_o, kdb_o, bf_o, bb_o, lwf_o, lwb_o, g_o, bonus_o):
    u = u_ref[0]
    tb = u.shape[0]
    row = lax.broadcasted_iota(I32, u.shape, 0)
    prev = jnp.where(row == 0, hp_ref[0, 0], pltpu.roll(u, 1, 0))
    nxt = jnp.where(row == tb - 1, hn_ref[0, 0], pltpu.roll(u, tb - 1, 0))
    xs = sw_ref[0:1] * prev + sw_ref[1:2] * u + sw_ref[2:3] * nxt
    aw = A_WIDTH
    r = xs[:, 0:aw]
    k = xs[:, aw:2 * aw]
    v = xs[:, 2 * aw:3 * aw]
    o = 3 * aw
    dw = xs[:, o:o + 2 * DECAY_LORA]
    da = xs[:, o + 2 * DECAY_LORA:o + 2 * DECAY_LORA + 2 * ICLR_LORA]
    dg = xs[:, o + 2 * DECAY_LORA + 2 * ICLR_LORA:]
    bd = bd_ref[...]
    kkr = k * kk_ref[...]
    kk = kkr * lax.rsqrt(_dot((kkr * kkr).astype(BF16), bd) + 1e-12)
    tw = jnp.tanh(dw).astype(BF16)
    dab = da.astype(BF16)
    r_o[0] = r.astype(BF16)
    v_o[0] = v.astype(BF16)
    nkk_o[0] = (-kk).astype(BF16)
    for d, (lw_o, kd_o, b_o) in enumerate(((lwf_o, kdf_o, bf_o), (lwb_o, kdb_o, bb_o))):
        lw_o[0] = -DECAY_SCALE * jax.nn.sigmoid(w0_ref[d:d + 1] + _dot(tw, wup_ref[d]))
        a = jax.nn.sigmoid(a0_ref[d:d + 1] + _dot(dab, aup_ref[d]))
        kd_o[0] = (k * (1.0 + (a - 1.0) * ka_ref[...])).astype(BF16)
        b_o[0] = (kk * a).astype(BF16)
    g_o[0] = _dot(jax.nn.sigmoid(dg).astype(BF16), gup_ref[...]).astype(BF16)
    bonus_o[0] = (_dot_x2(r * k * rk_ref[...], bd) * v).astype(BF16)


def _rwkv_prep(u3, edges, shift_w, w0, a0, wup_p, aup_p, gup, k_k, k_a, r_k, bd, tb):
    b, t, cols = u3.shape
    nb = t // tb
    zero = jnp.zeros((b, 1, cols), F32)
    if nb == 1:
        halo_prev = halo_next = zero.reshape(b, 1, 1, cols)
    else:
        edges = edges.reshape(b, nb, 2, cols)
        halo_prev = jnp.concatenate([zero, edges[:, :nb - 1, 1]], axis=1).reshape(b, nb, 1, cols)
        halo_next = jnp.concatenate([edges[:, 1:, 0], zero], axis=1).reshape(b, nb, 1, cols)
    aw = A_WIDTH
    full2 = lambda s: pl.BlockSpec(s, lambda i, j: (0, 0))
    full3 = lambda s: pl.BlockSpec(s, lambda i, j: (0, 0, 0))
    out_spec = pl.BlockSpec((1, tb, aw), lambda i, j: (i, j, 0))
    sds = lambda dt: jax.ShapeDtypeStruct((b, t, aw), dt)
    out_dtypes = [BF16] * 7 + [F32] * 2 + [BF16] * 2
    return pl.pallas_call(
        _rwkv_prep_kernel,
        grid=(b, nb),
        in_specs=[pl.BlockSpec((1, tb, cols), lambda i, j: (i, j, 0)),
                  pl.BlockSpec((1, 1, 1, cols), lambda i, j: (i, j, 0, 0)),
                  pl.BlockSpec((1, 1, 1, cols), lambda i, j: (i, j, 0, 0)),
                  full2((3, cols)), full2((2, aw)), full2((2, aw)),
                  full3(wup_p.shape), full3(aup_p.shape), full2(gup.shape),
                  full2((1, aw)), full2((1, aw)), full2((1, aw)), full2((aw, aw))],
        out_specs=[out_spec] * 11,
        out_shape=[sds(dt) for dt in out_dtypes],
        compiler_params=_cparams("arbitrary", "arbitrary"),
        name="rwkv_prep",
    )(u3, halo_prev, halo_next, shift_w, w0, a0, wup_p, aup_p, gup, k_k, k_a, r_k, bd)


def _bdot(a, b):
    return lax.dot_general(a, b, (((2,), (1,)), ((0,), (0,))), preferred_element_type=F32)


def _bdot_nt(a, b):
    return lax.dot_general(a, b, (((2,), (2,)), ((0,), (0,))), preferred_element_type=F32)


def _bdot_tn(a, b):
    return lax.dot_general(a, b, (((1,), (1,)), ((0,), (0,))), preferred_element_type=F32)


def _split_groups(x, w):
    return jnp.stack([x[g][:, p * w:(p + 1) * w]
                      for g in range(x.shape[0]) for p in range(x.shape[2] // w)], axis=0)


def _scan_kernel(r_ref, v_ref, nkk_ref, kd_ref, b_ref, lw_ref, s0_ref, o_ref, sf_ref, s_scr, *, reverse):
    c = pl.program_id(1)
    grp, ch, aw = lw_ref.shape
    w = s_scr.shape[-1]
    hp = w // A_HEAD_DIM
    n_groups = aw // w
    state_shape = s_scr.shape

    @pl.when(c == 0)
    def _():
        s_scr[...] = s0_ref[...].reshape(state_shape)

    lw = lw_ref[...]
    ri = lax.broadcasted_iota(I32, (ch, ch), 0)
    ci = lax.broadcasted_iota(I32, (ch, ch), 1)
    tri = jnp.where((ci >= ri) if reverse else (ci <= ri), 1.0, 0.0).astype(BF16)
    hi = lw.astype(BF16)
    rem = lw - hi.astype(F32)
    mid = rem.astype(BF16)
    lo = (rem - mid.astype(F32)).astype(BF16)
    li = jnp.stack([_dot(tri, hi[g]) + (_dot(tri, mid[g]) + _dot(tri, lo[g])) for g in range(grp)], axis=0)
    lt = li[:, 0:1] if reverse else li[:, ch - 1:ch]
    rho = 0.5 * lt
    e1 = jnp.exp(li - rho)
    e2 = jnp.exp(rho - li)
    er = jnp.exp(rho)
    a_rel = nkk_ref[...] * (e1 * jnp.exp(-lw))
    r_rel = r_ref[...] * e1
    b_rel = b_ref[...] * e2
    k_rel = kd_ref[...] * e2
    pairs = lambda x: _split_groups(x.astype(BF16), w)
    a_abs, r_abs, b_end, k_end = pairs(a_rel * er), pairs(r_rel * er), pairs(b_rel * er), pairs(k_rel * er)
    a_rel, r_rel, b_rel, k_rel = pairs(a_rel), pairs(r_rel), pairs(b_rel), pairs(k_rel)
    v = pairs(v_ref[...])
    decay = _split_groups(jnp.exp(lt), w)
    head_of_lane = lax.broadcasted_iota(I32, (1, 1, w), 2) // A_HEAD_DIM

    def blockdiag(x):
        zero = jnp.zeros_like(x)
        return jnp.concatenate([jnp.where(head_of_lane == j, x, zero) for j in range(hp)], axis=1)

    ri2 = lax.broadcasted_iota(I32, (1, ch, hp * ch), 1)
    ci2 = lax.broadcasted_iota(I32, (1, ch, hp * ch), 2) & (ch - 1)
    if reverse:
        incl2, strict2 = ci2 >= ri2, ci2 > ri2
    else:
        incl2, strict2 = ci2 <= ri2, ci2 < ri2
    eye2 = jnp.where(ri2 == ci2, 1.0, 0.0)
    lhs = jnp.concatenate([a_rel, r_rel], axis=1)
    with_b = _bdot_nt(lhs, blockdiag(b_rel))
    with_k = _bdot_nt(lhs, blockdiag(k_rel))
    a_ab = jnp.where(strict2, with_b[:, :ch], 0.0)
    a_ak = jnp.where(strict2, with_k[:, :ch], 0.0).astype(BF16)
    a_r = jnp.concatenate([jnp.where(incl2, with_b[:, ch:], 0.0), jnp.where(incl2, with_k[:, ch:], 0.0)],
                          axis=2).astype(BF16)
    inv = eye2
    for k in range(int(math.log2(ch))):
        p, q = (ci2 >> k, ri2 >> k) if reverse else (ri2 >> k, ci2 >> k)
        joins = ((p ^ q) * 4 + (p - q)) == 5
        lk = jnp.where(joins, a_ab, 0.0)
        if k == 0:
            inv = inv + lk
        else:
            invb = inv.astype(BF16)
            inv = inv + _bdot(_bdot(invb, blockdiag(lk.astype(BF16))).astype(BF16), blockdiag(invb))
    s = s_scr[...]
    from_state = _bdot_nt(jnp.concatenate([a_abs, r_abs], axis=1), s.astype(BF16))
    v_bd = blockdiag(v)
    x = from_state[:, :ch] + _bdot(a_ak, v_bd)
    u = _bdot(inv.astype(BF16), blockdiag(x.astype(BF16))).astype(BF16)
    o = from_state[:, ch:] + _bdot(a_r, jnp.concatenate([blockdiag(u), v_bd], axis=1))
    same_head = (lax.broadcasted_iota(I32, (1, w, w), 1) // A_HEAD_DIM) == head_of_lane
    upd = _bdot_tn(jnp.concatenate([u, v], axis=1), jnp.concatenate([b_end, k_end], axis=1))
    s_scr[...] = s * decay + jnp.where(same_head, upd, 0.0)
    for g in range(grp):
        o_ref[g] = jnp.concatenate([o[g * n_groups + p] for p in range(n_groups)], axis=1)

    @pl.when(c == pl.num_programs(1) - 1)
    def _():
        s_fin = s_scr[...]
        for g in range(grp):
            for p in range(n_groups):
                group = s_fin[g * n_groups + p]
                for j in range(hp):
                    part = slice(j * A_HEAD_DIM, (j + 1) * A_HEAD_DIM)
                    sf_ref[g, hp * p + j] = group[part, part]


def _wkv_scan(r, v, nkk, kd, b, lw, s0, reverse):
    bsz, t, aw = r.shape
    ch = SCAN_CHUNK
    assert t % ch == 0 and ch == A_HEAD_DIM
    nc = t // ch
    w = SCAN_PACK * A_HEAD_DIM
    n_groups = aw // w
    grp = SCAN_GROUP if bsz % SCAN_GROUP == 0 else 1
    tmap = (lambda i, c: (i, nc - 1 - c, 0)) if reverse else (lambda i, c: (i, c, 0))
    seq = pl.BlockSpec((grp, ch, aw), tmap)
    st_in = pl.BlockSpec((grp, n_groups, w, w), lambda i, c: (i, 0, 0, 0))
    st_out = pl.BlockSpec((grp,) + s0.shape[1:], lambda i, c: (i, 0, 0, 0))
    return pl.pallas_call(
        functools.partial(_scan_kernel, reverse=reverse),
        grid=(bsz // grp, nc),
        in_specs=[seq] * 6 + [st_in],
        out_specs=[seq, st_out],
        out_shape=[jax.ShapeDtypeStruct((bsz, t, aw), F32), jax.ShapeDtypeStruct(s0.shape, F32)],
        scratch_shapes=[pltpu.VMEM((grp * n_groups, w, w), F32)],
        compiler_params=_cparams("arbitrary", "arbitrary"),
        name="wkv_scan_bwd" if reverse else "wkv_scan_fwd",
    )(r, v, nkk, kd, b, lw, _group_states(s0, SCAN_PACK))


def _group_states(s, hp):
    b, h, d, _ = s.shape
    eye = jnp.eye(hp, dtype=s.dtype)[None, None, :, None, :, None]
    return (s.reshape(b, h // hp, hp, d, 1, d) * eye).reshape(b, h // hp, hp * d, hp * d)


def _denominator_lane(width):
    lane = lax.broadcasted_iota(I32, (1, width), 1)
    return jnp.where((lane & (HEAD_PAD - 1)) == V_HEAD, 1.0, 0.0)


def _rope_lanes(x, cos_t, sin_up, sin_dn):
    w = x.shape[1]
    half = QK_ROPE // 2
    return x * cos_t + pltpu.roll(x, half, 1) * sin_up + pltpu.roll(x, w - half, 1) * sin_dn


def _mla_prep_kernel(*refs, rope):
    if rope:
        (m_ref, gq_ref, gkv_ref, wuq_ref, wk_ref, wv_ref, p_ref,
         qc_ref, qu_ref, qd_ref, kc_ref, ku_ref, kd_ref, q_o, k_o, v_o, ckv_o) = refs
    else:
        (m_ref, gq_ref, gkv_ref, wuq_ref, wk_ref, wv_ref, p_ref, q_o, k_o, v_o, ckv_o) = refs
    m = m_ref[...]
    qn = _rms(m[:, :Q_LORA], gq_ref[...])
    ckv = _rms(m[:, Q_LORA:Q_LORA + KV_LORA], gkv_ref[...])
    kp = m[:, Q_LORA + KV_LORA:]
    q = _dot(qn.astype(BF16), wuq_ref[...]) * ATTN_SCALE
    if rope:
        tile = lambda ref: jnp.concatenate([ref[...]] * B_HEADS, axis=1)
        q = _rope_lanes(q, tile(qc_ref), tile(qu_ref), tile(qd_ref))
        kp = _rope_lanes(kp, kc_ref[...], ku_ref[...], kd_ref[...])
    cb = ckv.astype(BF16)
    q_o[...] = q.astype(BF16)
    k_o[...] = (_dot(cb, wk_ref[...]) + _dot(kp.astype(BF16), p_ref[...])).astype(BF16)
    v_o[...] = (_dot(cb, wv_ref[...]) + _denominator_lane(v_o.shape[1])).astype(BF16)
    ckv_o[...] = ckv


def _mla_prep(m2, g_q, g_kv, wuq_p, wk_p, wv_p, place, rope_tabs, tb, t):
    n = m2.shape[0]
    hw = B_HEADS * HEAD_PAD
    full = lambda a: pl.BlockSpec(a.shape, lambda i: (0, 0))
    ins = [m2, g_q, g_kv, wuq_p, wk_p, wv_p, place]
    specs = [pl.BlockSpec((tb, M_COLS), lambda i: (i, 0))] + [full(a) for a in ins[1:]]
    if rope_tabs is not None:
        nbt = t // tb
        ins += list(rope_tabs)
        specs += [pl.BlockSpec((tb, HEAD_PAD), lambda i: (i % nbt, 0))] * 6
    big = pl.BlockSpec((tb, hw), lambda i: (i, 0))
    return pl.pallas_call(
        functools.partial(_mla_prep_kernel, rope=rope_tabs is not None),
        grid=(n // tb,),
        in_specs=specs,
        out_specs=[big, big, big, pl.BlockSpec((tb, KV_LORA), lambda i: (i, 0))],
        out_shape=[jax.ShapeDtypeStruct((n, hw), BF16)] * 3 + [jax.ShapeDtypeStruct((n, KV_LORA), F32)],
        compiler_params=_cparams("arbitrary"),
        name="mla_prep",
    )(*ins)


def _kv_up_kernel(ckv_ref, kp_ref, wk_ref, wv_ref, p_ref, k_o, v_o):
    cb = ckv_ref[...].astype(BF16)
    k_o[...] = (_dot(cb, wk_ref[...]) + _dot(kp_ref[...].astype(BF16), p_ref[...])).astype(BF16)
    v_o[...] = (_dot(cb, wv_ref[...]) + _denominator_lane(v_o.shape[1])).astype(BF16)


def _kv_up(ckv2, kpe_pad, wk_p, wv_p, place, tb):
    n = ckv2.shape[0]
    hw = B_HEADS * HEAD_PAD
    full = lambda a: pl.BlockSpec(a.shape, lambda i: (0, 0))
    big = pl.BlockSpec((tb, hw), lambda i: (i, 0))
    return pl.pallas_call(
        _kv_up_kernel,
        grid=(n // tb,),
        in_specs=[pl.BlockSpec((tb, KV_LORA), lambda i: (i, 0)),
                  pl.BlockSpec((tb, HEAD_PAD), lambda i: (i, 0)),
                  full(wk_p), full(wv_p), full(place)],
        out_specs=[big, big],
        out_shape=[jax.ShapeDtypeStruct((n, hw), BF16)] * 2,
        compiler_params=_cparams("arbitrary"),
        name="kv_up",
    )(ckv2, kpe_pad, wk_p, wv_p, place)


def _attn_kernel(*refs, two):
    if two:
        q_ref, k1_ref, v1_ref, k2_ref, v2_ref, o_ref = refs
    else:
        q_ref, k1_ref, v1_ref, o_ref = refs
        heads = [slice(h * HEAD_PAD, (h + 1) * HEAD_PAD) for h in range(B_HEADS)]
        scores = [_dot_nt(q_ref[0, :, hs], k1_ref[0, :, hs]) for hs in heads]
        probs = [jnp.exp(s - jnp.max(s, axis=-1, keepdims=True)).astype(BF16) for s in scores]
        accs = [_dot(p, v1_ref[0, :, hs]) for p, hs in zip(probs, heads)]
        for acc, hs in zip(accs, heads):
            o_ref[0, :, hs] = (acc / acc[:, V_HEAD:V_HEAD + 1]).astype(BF16)
        return
    for h in range(B_HEADS):
        hs = slice(h * HEAD_PAD, (h + 1) * HEAD_PAD)
        q = q_ref[0, :, hs]
        s1 = _dot_nt(q, k1_ref[0, :, hs])
        mx = jnp.max(s1, axis=-1, keepdims=True)
        if two:
            s2 = _dot_nt(q, k2_ref[0, :, hs])
            mx = jnp.maximum(mx, jnp.max(s2, axis=-1, keepdims=True))
        acc = _dot(jnp.exp(s1 - mx).astype(BF16), v1_ref[0, :, hs])
        if two:
            acc = acc + _dot(jnp.exp(s2 - mx).astype(BF16), v2_ref[0, :, hs])
        o_ref[0, :, hs] = (acc / acc[:, V_HEAD:V_HEAD + 1]).astype(BF16)


def _attention(q3, k1, v1, k2, v2, tq):
    b, t, hw = q3.shape
    two = k2 is not None
    qspec = pl.BlockSpec((1, tq, hw), lambda i, j: (i, j, 0))
    kv = lambda a: pl.BlockSpec((1, a.shape[1], hw), lambda i, j: (i, 0, 0))
    ins = [q3, k1, v1] + ([k2, v2] if two else [])
    return pl.pallas_call(
        functools.partial(_attn_kernel, two=two),
        grid=(b, t // tq),
        in_specs=[qspec] + [kv(a) for a in ins[1:]],
        out_specs=qspec,
        out_shape=jax.ShapeDtypeStruct((b, t, hw), BF16),
        compiler_params=_cparams("arbitrary", "arbitrary"),
        name="attention",
    )(*ins)


def _post_mix_kernel(of_ref, ob_ref, bonus_ref, g_ref, att_ref, gates_ref, x_ref,
                     g1_ref, sc2_ref, sh2_ref, lnw_ref, lnb_ref, bd_ref, wa_ref, wb_ref, wo_ref,
                     gpost_ref, gpre_ref, wr_ref, x1_o, h2_o, aff_o):
    bd = bd_ref[...]
    inv_n = 1.0 / A_HEAD_DIM
    o = of_ref[...] + ob_ref[...]
    mu = _dot_x2(o, bd) * inv_n
    oc = o - mu
    var = _dot((oc * oc).astype(BF16), bd) * inv_n
    on = (oc * lax.rsqrt(var + GN_EPS)) * lnw_ref[...] + lnb_ref[...]
    ya = _dot(((on + bonus_ref[...]) * g_ref[...]).astype(BF16), wa_ref[...])
    yb = _dot(att_ref[...], wb_ref[...])
    d = ya.shape[1]
    gates = gates_ref[...].astype(F32)
    mix = jax.nn.sigmoid(gates[:, :d]) * ya + jax.nn.sigmoid(gates[:, d:]) * yb
    y = _dot(mix.astype(BF16), wo_ref[...])
    x1 = x_ref[...] + g1_ref[0] * _rms(y, gpost_ref[...])
    x1_o[...] = x1
    h2 = _rms(x1, gpre_ref[...]) * (1.0 + sc2_ref[0]) + sh2_ref[0]
    h2_o[...] = h2.astype(BF16)
    hh, hl = _split2(h2)
    wh, wl = _split2(wr_ref[...])
    logits = _dot_nt(wh, hh) + (_dot_nt(wh, hl) + _dot_nt(wl, hh))
    z = jnp.exp(logits - jnp.max(logits, axis=0, keepdims=True))
    aff_o[...] = z / jnp.sum(z, axis=0, keepdims=True)


def _post_mix(o_f, o_b, bonus, g, att, gates, x2, g1, sc2, sh2, ln_w, ln_b, bd, wa, wb, wo,
              g_post, g_pre, w_router_t, tb, blocks_per_batch):
    n, d = x2.shape
    tok = lambda a: pl.BlockSpec((tb, a.shape[1]), lambda i: (i, 0))
    full = lambda a: pl.BlockSpec(a.shape, lambda i: (0, 0))
    ms = lambda a: _mod_spec(a, blocks_per_batch)
    return pl.pallas_call(
        _post_mix_kernel,
        grid=(n // tb,),
        in_specs=[tok(o_f), tok(o_b), tok(bonus), tok(g), tok(att), tok(gates), tok(x2),
                  ms(g1), ms(sc2), ms(sh2), full(ln_w), full(ln_b), full(bd), full(wa), full(wb),
                  full(wo), full(g_post), full(g_pre), full(w_router_t)],
        out_specs=[pl.BlockSpec((tb, d), lambda i: (i, 0)),
                   pl.BlockSpec((tb, d), lambda i: (i, 0)),
                   pl.BlockSpec((N_EXPERTS, tb), lambda i: (0, i))],
        out_shape=[jax.ShapeDtypeStruct((n, d), F32),
                   jax.ShapeDtypeStruct((n, d), BF16),
                   jax.ShapeDtypeStruct((N_EXPERTS, n), F32)],
        compiler_params=_cparams("arbitrary"),
        name="post_mix",
    )(o_f, o_b, bonus, g, att, gates, x2, g1, sc2, sh2, ln_w, ln_b, bd, wa, wb, wo,
      g_post, g_pre, w_router_t)


def _select_kernel(a_ref, pos_ref, sel_ref, *, cap, n_chunks):
    a = a_ref[...]
    rows = a.shape[0]
    a3 = a.reshape(N_EXPERTS, n_chunks, LANES)

    def count(mask):
        c = jnp.sum(jnp.where(mask, 1.0, 0.0), axis=2, keepdims=True)
        return jnp.sum(c, axis=1, keepdims=True)

    def body(i, thr):
        cand = thr | jnp.left_shift(jnp.int32(1), 30 - i)
        return jnp.where(count(a3 >= pltpu.bitcast(cand, F32)) >= cap, cand, thr)

    thr = pltpu.bitcast(lax.fori_loop(0, 31, body, jnp.zeros((N_EXPERTS, 1, LANES), I32)), F32)
    gt = a3 > thr
    eq = a3 == thr
    need = cap - count(gt)

    li = lax.broadcasted_iota(I32, (LANES, LANES), 0)
    lj = lax.broadcasted_iota(I32, (LANES, LANES), 1)
    lane_before = jnp.where(li < lj, 1.0, 0.0).astype(BF16)
    lane_all = jnp.ones((LANES, LANES), BF16)
    ci = lax.broadcasted_iota(I32, (n_chunks, n_chunks), 0)
    cj = lax.broadcasted_iota(I32, (n_chunks, n_chunks), 1)
    chunk_before = jnp.where(cj < ci, 1.0, 0.0).astype(BF16)

    def prefix(flags3):
        f2 = flags3.reshape(rows, LANES).astype(BF16)
        within = _dot(f2, lane_before).reshape(N_EXPERTS, n_chunks, LANES)
        tot = _dot(f2, lane_all).astype(BF16).reshape(N_EXPERTS, n_chunks, LANES)
        offs = [_dot(chunk_before, tot[e]) for e in range(N_EXPERTS)]
        return within + jnp.stack(offs, axis=0)

    eqf = jnp.where(eq, 1.0, 0.0)
    tie_ok = jnp.where(prefix(eqf) < need, eqf, 0.0)
    sel = jnp.where(gt, 1.0, tie_ok)
    pos_ref[...] = prefix(sel).reshape(rows, LANES).astype(I32)
    sel_ref[...] = sel.reshape(rows, LANES).astype(I32)


def _select(aff_t, cap):
    e, n = aff_t.shape
    n_chunks = n // LANES
    rows = e * n_chunks
    a2 = aff_t.reshape(rows, LANES)
    spec = pl.BlockSpec((rows, LANES), lambda i: (0, 0))
    pos, sel = pl.pallas_call(
        functools.partial(_select_kernel, cap=cap, n_chunks=n_chunks),
        grid=(1,),
        in_specs=[spec],
        out_specs=[spec, spec],
        out_shape=[jax.ShapeDtypeStruct((rows, LANES), I32)] * 2,
        compiler_params=_cparams("arbitrary"),
        name="expert_select",
    )(a2)
    return pos.reshape(e, n), sel.reshape(e, n)


def _ffn_kernel(lo_ref, hi_ref, slot_ref, aff_ref, h_ref, wg_ref, wu_ref, wd_ref, y_ref, acc_ref, wacc_ref,
                *, n_tiles, sub):
    e = pl.program_id(0)
    j = pl.program_id(1)
    rows = acc_ref.shape[0]
    n_blocks, tb = slot_ref.shape[1], slot_ref.shape[2]
    n_sub = rows // sub
    acc_ref[...] = jnp.zeros_like(acc_ref)
    wacc_ref[...] = jnp.zeros_like(wacc_ref)
    first = [lo_ref[(e * n_tiles + j) * n_sub + si] for si in range(n_sub)]
    last = [hi_ref[(e * n_tiles + j) * n_sub + si] for si in range(n_sub)]
    trips = functools.reduce(jnp.maximum, [hi - lo for lo, hi in zip(first, last)])
    row0 = lax.broadcasted_iota(I32, (sub, tb), 0) + j * rows

    def gather_step(i, carry):
        for si in range(n_sub):
            b = first[si] + i
            live = b < last[si]
            b = jnp.minimum(b, n_blocks - 1)
            part = slice(si * sub, (si + 1) * sub)
            hit = (row0 + si * sub) == jnp.where(live, slot_ref[0, pl.ds(b, 1), :], -1)
            tokens = h_ref[pl.ds(pl.multiple_of(b * tb, tb), tb), :]
            acc_ref[part, :] += _dot(jnp.where(hit, 1.0, 0.0).astype(BF16), tokens)
            wacc_ref[part, :] += jnp.sum(jnp.where(hit, aff_ref[0, pl.ds(b, 1), :], 0.0), axis=1, keepdims=True)
        return carry

    lax.fori_loop(0, trips, gather_step, 0)
    xe = acc_ref[...].astype(BF16)
    gate = _dot(xe, wg_ref[0])
    up = _dot(xe, wu_ref[0])
    hid = (gate * jax.nn.sigmoid(gate)) * up
    y_ref[0] = (_dot(hid.astype(BF16), wd_ref[0]) * wacc_ref[...]).astype(BF16)


def _expert_ffn(pos, slot, aff_t, h2, wg, wu, wd, cap, tb, rt):
    n, d = h2.shape
    nb = n // tb
    e, _, f = wg.shape
    n_tiles = cap // rt
    sub = min(GATHER_ROWS, rt)
    starts = jnp.concatenate([pos[:, ::tb], jnp.full((e, 1), cap, I32)], axis=1)
    edges = jnp.arange(cap // sub, dtype=I32) * sub
    lo = jnp.sum((starts[:, None, 1:] <= edges[None, :, None]).astype(I32), axis=2)
    hi = jnp.sum((starts[:, None, :-1] < (edges + sub)[None, :, None]).astype(I32), axis=2)
    grid_spec = pltpu.PrefetchScalarGridSpec(
        num_scalar_prefetch=2,
        grid=(e, n_tiles),
        in_specs=[pl.BlockSpec((1, nb, tb), lambda ei, j, lo, hi: (ei, 0, 0)),
                  pl.BlockSpec((1, nb, tb), lambda ei, j, lo, hi: (ei, 0, 0)),
                  pl.BlockSpec((n, d), lambda ei, j, lo, hi: (0, 0), pipeline_mode=pl.Buffered(1)),
                  pl.BlockSpec((1, d, f), lambda ei, j, lo, hi: (ei, 0, 0)),
                  pl.BlockSpec((1, d, f), lambda ei, j, lo, hi: (ei, 0, 0)),
                  pl.BlockSpec((1, f, d), lambda ei, j, lo, hi: (ei, 0, 0))],
        out_specs=pl.BlockSpec((1, rt, d), lambda ei, j, lo, hi: (ei, j, 0)),
        scratch_shapes=[pltpu.VMEM((rt, d), F32), pltpu.VMEM((rt, 1), F32)],
    )
    return pl.pallas_call(
        functools.partial(_ffn_kernel, n_tiles=n_tiles, sub=sub),
        grid_spec=grid_spec,
        out_shape=jax.ShapeDtypeStruct((e, cap, d), BF16),
        compiler_params=_cparams("arbitrary", "arbitrary"),
        name="expert_ffn",
    )(lo.reshape(-1), hi.reshape(-1), slot.reshape(e, nb, tb), aff_t.reshape(e, nb, tb), h2, wg, wu, wd)


def _combine_kernel(t0_ref, fetch_ref, need_ref, slot_ref, x_ref, g2_ref, gp_ref, *rest, n_win):
    y_refs, o_ref, acc_ref = rest[:-2], rest[-2], rest[-1]
    blk = pl.program_id(0)
    n_blk = pl.num_programs(0)
    tb = slot_ref.shape[0]
    rows = y_refs[0].shape[1]
    lane = lax.broadcasted_iota(I32, (tb, rows), 1)

    def picked(e, k):
        slot = slot_ref[:, e:e + 1] - (t0_ref[blk * N_EXPERTS + e] + k) * rows
        return _dot(jnp.where(lane == slot, 1.0, 0.0).astype(BF16), y_refs[e * n_win + k][0])

    acc = picked(0, 0)
    for e in range(1, N_EXPERTS):
        acc = acc + picked(e, 0)
    acc_ref[...] = acc
    for k in range(1, n_win):
        for e in range(N_EXPERTS):
            @pl.when(need_ref[(k * n_blk + blk) * N_EXPERTS + e] != 0)
            def _():
                acc_ref[...] += picked(e, k)
    o_ref[...] = x_ref[...] + g2_ref[0] * _rms(acc_ref[...], gp_ref[...])


def _combine_windows(pos, tb, rt, cap):
    n_tiles = cap // rt
    lo = pos[:, ::tb]
    hi = jnp.concatenate([lo[:, 1:], jnp.full((lo.shape[0], 1), cap, I32)], axis=1)
    t0 = jnp.minimum(lo // rt, n_tiles - 1)
    t_last = jnp.where(hi > lo, (hi - 1) // rt, t0)
    fetch, need = [t0], [jnp.ones_like(t0)]
    for k in range(1, tb // rt + 1):
        need.append(t_last >= t0 + k)
        fetch.append(lax.cummax(jnp.where(need[k], t0 + k, 0), axis=1))
    flat = lambda a: a.T.reshape(-1).astype(I32)
    return flat(t0), jnp.concatenate([flat(f) for f in fetch]), jnp.concatenate([flat(f) for f in need])


def _combine(windows, slot_t, ye, x1, g2, g_post, tb, rt, blocks_per_batch):
    t0, fetch, need = windows
    n, d = x1.shape
    e, cap, _ = ye.shape
    nb = n // tb
    n_win = tb // rt + 1
    if g2.shape[0] == 1:
        g2_spec = pl.BlockSpec((1, 1, d), lambda b, t, f, nd: (0, 0, 0))
    else:
        g2_spec = pl.BlockSpec((1, 1, d), lambda b, t, f, nd: (b // blocks_per_batch, 0, 0))

    def window(ei, k):
        return pl.BlockSpec((1, rt, d), lambda b, t, f, nd: (ei, f[(k * nb + b) * e + ei], 0))

    grid_spec = pltpu.PrefetchScalarGridSpec(
        num_scalar_prefetch=3,
        grid=(nb,),
        in_specs=[pl.BlockSpec((tb, e), lambda b, t, f, nd: (b, 0)),
                  pl.BlockSpec((tb, d), lambda b, t, f, nd: (b, 0)),
                  g2_spec,
                  pl.BlockSpec((1, d), lambda b, t, f, nd: (0, 0))]
                 + [window(ei, k) for ei in range(e) for k in range(n_win)],
        out_specs=pl.BlockSpec((tb, d), lambda b, t, f, nd: (b, 0)),
        scratch_shapes=[pltpu.VMEM((tb, d), F32)],
    )
    return pl.pallas_call(
        functools.partial(_combine_kernel, n_win=n_win),
        grid_spec=grid_spec,
        out_shape=jax.ShapeDtypeStruct((n, d), F32),
        compiler_params=_cparams("arbitrary"),
        name="moe_combine",
    )(t0, fetch, need, slot_t, x1, g2, g_post, *([ye] * (e * n_win)))


def _rope_tables(t):
    half = QK_ROPE // 2
    pos = jnp.arange(t)
    row = (pos // GRID_W).astype(F32)
    col = (pos % GRID_W).astype(F32)
    inv = ROPE_BASE ** (-jnp.arange(0, half, 2, dtype=F32) / half)
    ang = jnp.concatenate([row[:, None] * inv, col[:, None] * inv], axis=-1)
    cos, sin = jnp.cos(ang), jnp.sin(ang)

    def tabs(first_lane):
        left, right = first_lane, HEAD_PAD - first_lane - 2 * half
        fill = lambda value, width: jnp.full((t, width), value, F32)
        c = jnp.concatenate([fill(1.0, left), cos, cos, fill(1.0, right)], axis=1)
        up = jnp.concatenate([fill(0.0, left + half), sin, fill(0.0, right)], axis=1)
        dn = jnp.concatenate([fill(0.0, left), -sin, fill(0.0, half + right)], axis=1)
        return c, up, dn

    return tabs(QK_NOPE + QK_ROPE) + tabs(QK_ROPE)


def _layout_weights(w_in, w_uq, w_ukv, w_branch_b, w_up, a_up):
    d = w_in.shape[0]
    deint = jnp.concatenate([jnp.arange(0, QK_ROPE, 2), jnp.arange(1, QK_ROPE, 2)])
    kpe0 = RWKV_COLS + Q_LORA + KV_LORA
    kpe_cols = w_in[:, kpe0:kpe0 + QK_ROPE]
    w_in_p = jnp.concatenate(
        [w_in[:, :kpe0 + QK_ROPE], kpe_cols[:, deint],
         jnp.zeros((d, M_COLS - MLA_COLS - QK_ROPE), F32), w_in[:, RWKV_COLS + MLA_COLS:]], axis=1)
    uq = w_uq.reshape(Q_LORA, B_HEADS, QK_NOPE + QK_ROPE)
    wuq_p = jnp.concatenate([uq, uq[:, :, QK_NOPE:][:, :, deint]], axis=2).reshape(Q_LORA, B_HEADS * HEAD_PAD)
    ukv = w_ukv.reshape(KV_LORA, B_HEADS, QK_NOPE + V_HEAD)
    zpad = jnp.zeros((KV_LORA, B_HEADS, HEAD_PAD - QK_NOPE), F32)
    wk_p = jnp.concatenate([ukv[:, :, :QK_NOPE], zpad], axis=2).reshape(KV_LORA, B_HEADS * HEAD_PAD)
    wv_p = jnp.concatenate([ukv[:, :, QK_NOPE:], zpad], axis=2).reshape(KV_LORA, B_HEADS * HEAD_PAD)
    wb = w_branch_b.reshape(B_HEADS, V_HEAD, d)
    wb_p = jnp.concatenate([wb, jnp.zeros((B_HEADS, HEAD_PAD - V_HEAD, d), F32)], axis=1).reshape(B_HEADS * HEAD_PAD, d)
    eye = jnp.eye(QK_ROPE, dtype=F32)
    z = jnp.zeros((QK_ROPE, QK_ROPE), F32)
    head_raw = jnp.concatenate([jnp.zeros((QK_ROPE, QK_NOPE), F32), eye, z], axis=1)
    head_rot = jnp.concatenate([jnp.zeros((QK_ROPE, QK_NOPE), F32), z, eye], axis=1)
    zrows = jnp.zeros((HEAD_PAD - 2 * QK_ROPE, B_HEADS * HEAD_PAD), F32)
    zr = jnp.zeros((QK_ROPE, B_HEADS * HEAD_PAD), F32)
    place_raw = jnp.concatenate([jnp.tile(head_raw, (1, B_HEADS)), zr, zrows], axis=0)
    place_rot = jnp.concatenate([zr, jnp.tile(head_rot, (1, B_HEADS)), zrows], axis=0)
    zl = jnp.zeros((DECAY_LORA, A_WIDTH), F32)
    wup_p = jnp.stack([jnp.concatenate([w_up[0], zl]), jnp.concatenate([zl, w_up[1]])])
    aup_p = jnp.stack([jnp.concatenate([a_up[0], zl]), jnp.concatenate([zl, a_up[1]])])
    bf = lambda a: a.astype(BF16)
    return (bf(w_in_p), bf(wuq_p), bf(wk_p), bf(wv_p), bf(wb_p), bf(place_raw), bf(place_rot),
            bf(wup_p), bf(aup_p))


def _group_layer(x, mods, lw, s0_f, s0_b, ctx_kv, rope_tabs):
    bsz, t, d = x.shape
    n = bsz * t
    tb = min(TOKEN_BLOCK, t)
    tw = min(ROW_BLOCK, t)
    bpb = t // tb
    bpw = t // tw
    tf = min(ROW_BLOCK, n) if mods[0].shape[0] == 1 and rope_tabs is None else tw
    sh1, sc1, g1, sh2, sc2, g2 = mods
    x2 = x.reshape(n, d)
    u, m, gates, edges = _inproj(x2, sc1, sh1, lw["g_pre_mix"], lw["w_in_p"], tf, bpw)
    prep = _rwkv_prep(u.reshape(bsz, t, RWKV_COLS), edges, lw["shift_w"], lw["w0"], lw["a0"], lw["wup_p"],
                      lw["aup_p"], lw["g_up"], lw["k_k"], lw["k_a"], lw["r_k"], lw["bd"], tw)
    r, v, nkk, kd_f, kd_b, b_f, b_b, lw_f, lw_b, g, bonus = prep
    o_f, s_f = _wkv_scan(r, v, nkk, kd_f, b_f, lw_f, s0_f, reverse=False)
    o_b, s_b = _wkv_scan(r, v, nkk, kd_b, b_b, lw_b, s0_b, reverse=True)
    place = lw["place_raw"] if rope_tabs is None else lw["place_rot"]
    q, k, vv, ckv = _mla_prep(m, lw["g_qnorm"], lw["g_kvnorm"], lw["wuq_p"], lw["wk_p"], lw["wv_p"],
                              place, rope_tabs, tf, t)
    hw = B_HEADS * HEAD_PAD
    k2, v2 = ctx_kv if ctx_kv is not None else (None, None)
    att = _attention(q.reshape(bsz, t, hw), k.reshape(bsz, t, hw), vv.reshape(bsz, t, hw), k2, v2, tw)
    two = lambda a: a.reshape(n, a.shape[-1])
    x1, h2, aff_t = _post_mix(two(o_f), two(o_b), two(bonus), two(g), att.reshape(n, hw), gates, x2,
                              g1, sc2, sh2, lw["ln_x_w"], lw["ln_x_b"], lw["bd"], lw["w_branch_a"],
                              lw["wb_p"], lw["w_out"], lw["g_post_mix"], lw["g_pre_ffn"],
                              lw["w_router_t"], tf, bpw)
    cap = CAPACITY_FACTOR * n // N_EXPERTS
    rt = min(EXPERT_TILE, cap)
    pos, sel = _select(aff_t, cap)
    slot = jnp.where(sel > 0, pos, -1)
    ye = _expert_ffn(pos, slot, aff_t, h2, lw["w_exp_gate"], lw["w_exp_up"], lw["w_exp_down"], cap,
                     min(GATHER_BLOCK, n), min(FFN_TILE, cap))
    out = _combine(_combine_windows(pos, tb, rt, cap), slot.T, ye, x1, g2, lw["g_post_ffn"], tb, rt, bpb)
    kpe = m[:, Q_LORA + KV_LORA:Q_LORA + KV_LORA + QK_ROPE]
    return out.reshape(bsz, t, d), (ckv.reshape(bsz, t, KV_LORA), kpe.reshape(bsz, t, QK_ROPE), s_f, s_b)


def kernel(x_prompt, x_sample, cache_ckv, cache_kpe, state_wkv_fwd, state_wkv_bwd, c, c_ctx,
           w_mod, b_mod, g_pre_mix, g_post_mix, g_pre_ffn, g_post_ffn, w_in, shift_w,
           w0, w_up, a0, a_up, g_up, k_k, k_a, r_k, ln_x_w, ln_x_b, w_branch_a,
           g_qnorm, w_uq, g_kvnorm, w_ukv, w_branch_b, w_out,
           w_router, w_exp_gate, w_exp_up, w_exp_down):
    depth = w_mod.shape[0]
    d = x_prompt.shape[-1]
    dec_b, dec_t = x_sample.shape[0], x_sample.shape[1]
    xp, xs = x_prompt, x_sample
    c_rows = jnp.concatenate([c, c_ctx[None, :],
                              jnp.zeros((-(dec_b + 1) % 8, d), F32)], axis=0)
    rope_tabs = _rope_tables(dec_t)
    ii = lax.broadcasted_iota(I32, (A_WIDTH, A_WIDTH), 0) // A_HEAD_DIM
    jj = lax.broadcasted_iota(I32, (A_WIDTH, A_WIDTH), 1) // A_HEAD_DIM
    bd = (ii == jj).astype(BF16)
    row = lambda a: a.reshape(1, -1)
    bf = lambda a: a.astype(BF16)
    ckv_l, kpe_l, sf_l, sb_l = [], [], [], []
    for l in range(depth):
        (w_in_p, wuq_p, wk_p, wv_p, wb_p, place_raw, place_rot, wup_p, aup_p) = _layout_weights(
            w_in[l], w_uq[l], w_ukv[l], w_branch_b[l], w_up[l], a_up[l])
        lw = {
            "g_pre_mix": row(g_pre_mix[l]), "g_post_mix": row(g_post_mix[l]),
            "g_pre_ffn": row(g_pre_ffn[l]), "g_post_ffn": row(g_post_ffn[l]),
            "w_in_p": w_in_p, "shift_w": shift_w[l], "w0": w0[l], "a0": a0[l],
            "wup_p": wup_p, "aup_p": aup_p, "g_up": bf(g_up[l]),
            "k_k": row(k_k[l]), "k_a": row(k_a[l]), "r_k": row(r_k[l]), "bd": bd,
            "ln_x_w": row(ln_x_w[l]), "ln_x_b": row(ln_x_b[l]), "w_branch_a": bf(w_branch_a[l]),
            "g_qnorm": row(g_qnorm[l]), "g_kvnorm": row(g_kvnorm[l]),
            "wuq_p": wuq_p, "wk_p": wk_p, "wv_p": wv_p, "wb_p": wb_p,
            "place_raw": place_raw, "place_rot": place_rot,
            "w_out": bf(w_out[l]), "w_router_t": w_router[l].T,
            "w_exp_gate": bf(w_exp_gate[l]), "w_exp_up": bf(w_exp_up[l]), "w_exp_down": bf(w_exp_down[l]),
        }
        mod = _modulation(c_rows, w_mod[l], b_mod[l])
        mods_lat = [mod[:dec_b, i * d:(i + 1) * d].reshape(dec_b, 1, d) for i in range(6)]
        mods_ctx = [mod[dec_b:dec_b + 1, i * d:(i + 1) * d].reshape(1, 1, d) for i in range(6)]
        zeros_state = jnp.zeros((xp.shape[0], A_HEADS, A_HEAD_DIM, A_HEAD_DIM), F32)
        xp, (ckv, kpe, s_f, s_b) = _group_layer(xp, mods_ctx, lw, zeros_state, zeros_state, None, None)
        ckv_l.append(ckv)
        kpe_l.append(kpe)
        sf_l.append(s_f)
        sb_l.append(s_b)
        past = cache_ckv.shape[2]
        kpe_pad = jnp.concatenate(
            [cache_kpe[:, l], jnp.zeros((dec_b, past, HEAD_PAD - QK_ROPE), F32)], axis=-1)
        k_ctx, v_ctx = _kv_up(cache_ckv[:, l].reshape(dec_b * past, KV_LORA),
                              kpe_pad.reshape(dec_b * past, HEAD_PAD), wk_p, wv_p, place_raw,
                              min(TOKEN_BLOCK, past))
        hw = B_HEADS * HEAD_PAD
        ctx_kv = (k_ctx.reshape(dec_b, past, hw), v_ctx.reshape(dec_b, past, hw))
        xs, _ = _group_layer(xs, mods_lat, lw, state_wkv_fwd[:, l], state_wkv_bwd[:, l], ctx_kv, rope_tabs)
    return (xp, xs, jnp.stack(ckv_l, axis=1), jnp.stack(kpe_l, axis=1),
            jnp.stack(sf_l, axis=1), jnp.stack(sb_l, axis=1))
```

```python
import functools
import math

import jax
import jax.numpy as jnp
from jax import lax
from jax.experimental import pallas as pl
from jax.experimental.pallas import tpu as pltpu

F32 = jnp.float32
BF16 = jnp.bfloat16
I32 = jnp.int32

GRID_W = 64
A_HEADS = 8
A_HEAD_DIM = 64
A_WIDTH = A_HEADS * A_HEAD_DIM
DECAY_LORA = 64
ICLR_LORA = 64
GATE_LORA = 128
DECAY_SCALE = 0.6065306597126334
GN_EPS = 64e-5
B_HEADS = 8
Q_LORA = 256
KV_LORA = 128
QK_NOPE = 64
QK_ROPE = 32
V_HEAD = 64
ROPE_BASE = 10000.0
ATTN_SCALE = 1.0 / math.sqrt(QK_NOPE + QK_ROPE)
N_EXPERTS = 16
CAPACITY_FACTOR = 2
EPS = 1e-6
RWKV_COLS = 3 * A_WIDTH + 2 * DECAY_LORA + 2 * ICLR_LORA + GATE_LORA
MLA_COLS = Q_LORA + KV_LORA + QK_ROPE

LANES = 128
HEAD_PAD = 128
M_COLS = 512
VMEM_LIMIT = 56 * 1024 * 1024

SCAN_CHUNK = 64
SCAN_GROUP = 8
SCAN_PACK = 2
TOKEN_BLOCK = 256
ROW_BLOCK = 512
EXPERT_TILE = 256
FFN_TILE = 256
GATHER_BLOCK = 512
GATHER_ROWS = 128


def _cparams(*sem):
    return pltpu.CompilerParams(dimension_semantics=sem, vmem_limit_bytes=VMEM_LIMIT)


def _dot(a, b):
    return jnp.dot(a, b, preferred_element_type=F32)


def _dot_nt(a, b):
    return lax.dot_general(a, b, (((1,), (1,)), ((), ())), preferred_element_type=F32)


def _dot_tn(a, b):
    return lax.dot_general(a, b, (((0,), (0,)), ((), ())), preferred_element_type=F32)


def _split2(x):
    hi = x.astype(BF16)
    lo = (x - hi.astype(F32)).astype(BF16)
    return hi, lo


def _dot_x2(a, b_bf16):
    hi, lo = _split2(a)
    return _dot(hi, b_bf16) + _dot(lo, b_bf16)


def _dot_f32(a, b):
    ah, al = _split2(a)
    bh, bl = _split2(b)
    return _dot(ah, bh) + (_dot(ah, bl) + _dot(al, bh))


def _rms(x, g):
    return (x * lax.rsqrt(jnp.mean(x * x, axis=-1, keepdims=True) + EPS)) * g


def _mod_kernel(c_ref, w_ref, b_ref, o_ref):
    c = c_ref[...]
    s = c * jax.nn.sigmoid(c)
    o_ref[...] = _dot_f32(s, w_ref[...]) + b_ref[...]


def _modulation(c_rows, w_mod, b_mod):
    rows, d = c_rows.shape
    n_out = w_mod.shape[1]
    tn = n_out // 8
    return pl.pallas_call(
        _mod_kernel,
        grid=(n_out // tn,),
        in_specs=[pl.BlockSpec((rows, d), lambda j: (0, 0)),
                  pl.BlockSpec((d, tn), lambda j: (0, j)),
                  pl.BlockSpec((1, tn), lambda j: (0, j))],
        out_specs=pl.BlockSpec((rows, tn), lambda j: (0, j)),
        out_shape=jax.ShapeDtypeStruct((rows, n_out), F32),
        compiler_params=_cparams("arbitrary"),
        name="modulation",
    )(c_rows, w_mod, b_mod.reshape(1, n_out))


def _inproj_kernel(x_ref, sc_ref, sh_ref, g_ref, w_ref, u_ref, m_ref, gt_ref, edge_ref):
    h = _rms(x_ref[...], g_ref[...]) * (1.0 + sc_ref[0]) + sh_ref[0]
    hb = h.astype(BF16)
    u = _dot(hb, w_ref[:, :RWKV_COLS])
    u_ref[...] = u
    edge_ref[0, 0:1, :] = u[0:1]
    edge_ref[0, 1:2, :] = u[u.shape[0] - 1:]
    m_ref[...] = _dot(hb, w_ref[:, RWKV_COLS:RWKV_COLS + M_COLS])
    gt_ref[...] = _dot(hb, w_ref[:, RWKV_COLS + M_COLS:]).astype(gt_ref.dtype)


def _mod_spec(mod, blocks_per_batch):
    d = mod.shape[-1]
    if mod.shape[0] == 1:
        return pl.BlockSpec((1, 1, d), lambda i: (0, 0, 0))
    return pl.BlockSpec((1, 1, d), lambda i: (i // blocks_per_batch, 0, 0))


def _inproj(x2, sc, sh, g, w_in_p, tb, blocks_per_batch):
    n, d = x2.shape
    cols = w_in_p.shape[1]
    gate_cols = cols - RWKV_COLS - M_COLS
    return pl.pallas_call(
        _inproj_kernel,
        grid=(n // tb,),
        in_specs=[pl.BlockSpec((tb, d), lambda i: (i, 0)),
                  _mod_spec(sc, blocks_per_batch), _mod_spec(sh, blocks_per_batch),
                  pl.BlockSpec((1, d), lambda i: (0, 0)),
                  pl.BlockSpec((d, cols), lambda i: (0, 0))],
        out_specs=[pl.BlockSpec((tb, RWKV_COLS), lambda i: (i, 0)),
                   pl.BlockSpec((tb, M_COLS), lambda i: (i, 0)),
                   pl.BlockSpec((tb, gate_cols), lambda i: (i, 0)),
                   pl.BlockSpec((1, 2, RWKV_COLS), lambda i: (i, 0, 0))],
        out_shape=[jax.ShapeDtypeStruct((n, RWKV_COLS), F32),
                   jax.ShapeDtypeStruct((n, M_COLS), F32),
                   jax.ShapeDtypeStruct((n, gate_cols), BF16),
                   jax.ShapeDtypeStruct((n // tb, 2, RWKV_COLS), F32)],
        compiler_params=_cparams("arbitrary"),
        name="inproj",
    )(x2, sc, sh, g, w_in_p)


def _rwkv_prep_kernel(u_ref, hp_ref, hn_ref, sw_ref, w0_ref, a0_ref, wup_ref, aup_ref, gup_ref,
                      kk_ref, ka_ref, rk_ref, bd_ref,
                      r_o, v_o, nkk_o, kdf_o, kdb_o, bf_o, bb_o, lwf_o, lwb_o, g_o, bonus_o):
    u = u_ref[0]
    tb = u.shape[0]
    row = lax.broadcasted_iota(I32, u.shape, 0)
    prev = jnp.where(row == 0, hp_ref[0, 0], pltpu.roll(u, 1, 0))
    nxt = jnp.where(row == tb - 1, hn_ref[0, 0], pltpu.roll(u, tb - 1, 0))
    xs = sw_ref[0:1] * prev + sw_ref[1:2] * u + sw_ref[2:3] * nxt
    aw = A_WIDTH
    r = xs[:, 0:aw]
    k = xs[:, aw:2 * aw]
    v = xs[:, 2 * aw:3 * aw]
    o = 3 * aw
    dw = xs[:, o:o + 2 * DECAY_LORA]
    da = xs[:, o + 2 * DECAY_LORA:o + 2 * DECAY_LORA + 2 * ICLR_LORA]
    dg = xs[:, o + 2 * DECAY_LORA + 2 * ICLR_LORA:]
    bd = bd_ref[...]
    kkr = k * kk_ref[...]
    kk = kkr * lax.rsqrt(_dot((kkr * kkr).astype(BF16), bd) + 1e-12)
    tw = jnp.tanh(dw).astype(BF16)
    dab = da.astype(BF16)
    r_o[0] = r.astype(BF16)
    v_o[0] = v.astype(BF16)
    nkk_o[0] = (-kk).astype(BF16)
    for d, (lw_o, kd_o, b_o) in enumerate(((lwf_o, kdf_o, bf_o), (lwb_o, kdb_o, bb_o))):
        lw_o[0] = -DECAY_SCALE * jax.nn.sigmoid(w0_ref[d:d + 1] + _dot(tw, wup_ref[d]))
        a = jax.nn.sigmoid(a0_ref[d:d + 1] + _dot(dab, aup_ref[d]))
        kd_o[0] = (k * (1.0 + (a - 1.0) * ka_ref[...])).astype(BF16)
        b_o[0] = (kk * a).astype(BF16)
    g_o[0] = _dot(jax.nn.sigmoid(dg).astype(BF16), gup_ref[...]).astype(BF16)
    bonus_o[0] = (_dot_x2(r * k * rk_ref[...], bd) * v).astype(BF16)


def _rwkv_prep(u3, edges, shift_w, w0, a0, wup_p, aup_p, gup, k_k, k_a, r_k, bd, tb):
    b, t, cols = u3.shape
    nb = t // tb
    zero = jnp.zeros((b, 1, cols), F32)
    if nb == 1:
        halo_prev = halo_next = zero.reshape(b, 1, 1, cols)
    else:
        edges = edges.reshape(b, nb, 2, cols)
        halo_prev = jnp.concatenate([zero, edges[:, :nb - 1, 1]], axis=1).reshape(b, nb, 1, cols)
        halo_next = jnp.concatenate([edges[:, 1:, 0], zero], axis=1).reshape(b, nb, 1, cols)
    aw = A_WIDTH
    full2 = lambda s: pl.BlockSpec(s, lambda i, j: (0, 0))
    full3 = lambda s: pl.BlockSpec(s, lambda i, j: (0, 0, 0))
    out_spec = pl.BlockSpec((1, tb, aw), lambda i, j: (i, j, 0))
    sds = lambda dt: jax.ShapeDtypeStruct((b, t, aw), dt)
    out_dtypes = [BF16] * 7 + [F32] * 2 + [BF16] * 2
    return pl.pallas_call(
        _rwkv_prep_kernel,
        grid=(b, nb),
        in_specs=[pl.BlockSpec((1, tb, cols), lambda i, j: (i, j, 0)),
                  pl.BlockSpec((1, 1, 1, cols), lambda i, j: (i, j, 0, 0)),
                  pl.BlockSpec((1, 1, 1, cols), lambda i, j: (i, j, 0, 0)),
                  full2((3, cols)), full2((2, aw)), full2((2, aw)),
                  full3(wup_p.shape), full3(aup_p.shape), full2(gup.shape),
                  full2((1, aw)), full2((1, aw)), full2((1, aw)), full2((aw, aw))],
        out_specs=[out_spec] * 11,
        out_shape=[sds(dt) for dt in out_dtypes],
        compiler_params=_cparams("arbitrary", "arbitrary"),
        name="rwkv_prep",
    )(u3, halo_prev, halo_next, shift_w, w0, a0, wup_p, aup_p, gup, k_k, k_a, r_k, bd)


def _bdot(a, b):
    return lax.dot_general(a, b, (((2,), (1,)), ((0,), (0,))), preferred_element_type=F32)


def _bdot_nt(a, b):
    return lax.dot_general(a, b, (((2,), (2,)), ((0,), (0,))), preferred_element_type=F32)


def _bdot_tn(a, b):
    return lax.dot_general(a, b, (((1,), (1,)), ((0,), (0,))), preferred_element_type=F32)


def _split_groups(x, w):
    return jnp.stack([x[g][:, p * w:(p + 1) * w]
                      for g in range(x.shape[0]) for p in range(x.shape[2] // w)], axis=0)


def _scan_kernel(*refs, reverse, add_to):
    if add_to:
        r_ref, v_ref, nkk_ref, kd_ref, b_ref, lw_ref, s0_ref, other_ref, o_ref, sf_ref, s_scr = refs
    else:
        r_ref, v_ref, nkk_ref, kd_ref, b_ref, lw_ref, s0_ref, o_ref, sf_ref, s_scr = refs
    c = pl.program_id(1)
    grp, ch, aw = lw_ref.shape
    w = s_scr.shape[-1]
    hp = w // A_HEAD_DIM
    n_groups = aw // w
    state_shape = s_scr.shape

    @pl.when(c == 0)
    def _():
        s_scr[...] = s0_ref[...].reshape(state_shape)

    lw = lw_ref[...]
    ri = lax.broadcasted_iota(I32, (ch, ch), 0)
    ci = lax.broadcasted_iota(I32, (ch, ch), 1)
    tri = jnp.where((ci >= ri) if reverse else (ci <= ri), 1.0, 0.0).astype(BF16)
    hi = lw.astype(BF16)
    rem = lw - hi.astype(F32)
    mid = rem.astype(BF16)
    lo = (rem - mid.astype(F32)).astype(BF16)
    li = jnp.stack([_dot(tri, hi[g]) + (_dot(tri, mid[g]) + _dot(tri, lo[g])) for g in range(grp)], axis=0)
    lt = li[:, 0:1] if reverse else li[:, ch - 1:ch]
    rho = 0.5 * lt
    e1 = jnp.exp(li - rho)
    e2 = jnp.exp(rho - li)
    er = jnp.exp(rho)
    a_rel = nkk_ref[...] * (e1 * jnp.exp(-lw))
    r_rel = r_ref[...] * e1
    b_rel = b_ref[...] * e2
    k_rel = kd_ref[...] * e2
    pairs = lambda x: _split_groups(x.astype(BF16), w)
    a_abs, r_abs, b_end, k_end = pairs(a_rel * er), pairs(r_rel * er), pairs(b_rel * er), pairs(k_rel * er)
    a_rel, r_rel, b_rel, k_rel = pairs(a_rel), pairs(r_rel), pairs(b_rel), pairs(k_rel)
    v = pairs(v_ref[...])
    decay = _split_groups(jnp.exp(lt), w)
    head_of_lane = lax.broadcasted_iota(I32, (1, 1, w), 2) // A_HEAD_DIM

    def blockdiag(x):
        zero = jnp.zeros_like(x)
        return jnp.concatenate([jnp.where(head_of_lane == j, x, zero) for j in range(hp)], axis=1)

    ri2 = lax.broadcasted_iota(I32, (1, ch, hp * ch), 1)
    ci2 = lax.broadcasted_iota(I32, (1, ch, hp * ch), 2) & (ch - 1)
    if reverse:
        incl2, strict2 = ci2 >= ri2, ci2 > ri2
    else:
        incl2, strict2 = ci2 <= ri2, ci2 < ri2
    eye2 = jnp.where(ri2 == ci2, 1.0, 0.0)
    lhs = jnp.concatenate([a_rel, r_rel], axis=1)
    with_b = _bdot_nt(lhs, blockdiag(b_rel))
    with_k = _bdot_nt(lhs, blockdiag(k_rel))
    a_ab = jnp.where(strict2, with_b[:, :ch], 0.0)
    a_ak = jnp.where(strict2, with_k[:, :ch], 0.0).astype(BF16)
    a_r = jnp.concatenate([jnp.where(incl2, with_b[:, ch:], 0.0), jnp.where(incl2, with_k[:, ch:], 0.0)],
                          axis=2).astype(BF16)
    inv = eye2
    for k in range(int(math.log2(ch))):
        p, q = (ci2 >> k, ri2 >> k) if reverse else (ri2 >> k, ci2 >> k)
        joins = ((p ^ q) * 4 + (p - q)) == 5
        lk = jnp.where(joins, a_ab, 0.0)
        if k == 0:
            inv = inv + lk
        else:
            invb = inv.astype(BF16)
            inv = inv + _bdot(_bdot(invb, blockdiag(lk.astype(BF16))).astype(BF16), blockdiag(invb))
    s = s_scr[...]
    from_state = _bdot_nt(jnp.concatenate([a_abs, r_abs], axis=1), s.astype(BF16))
    v_bd = blockdiag(v)
    x = from_state[:, :ch] + _bdot(a_ak, v_bd)
    u = _bdot(inv.astype(BF16), blockdiag(x.astype(BF16))).astype(BF16)
    o = from_state[:, ch:] + _bdot(a_r, jnp.concatenate([blockdiag(u), v_bd], axis=1))
    same_head = (lax.broadcasted_iota(I32, (1, w, w), 1) // A_HEAD_DIM) == head_of_lane
    upd = _bdot_tn(jnp.concatenate([u, v], axis=1), jnp.concatenate([b_end, k_end], axis=1))
    s_scr[...] = s * decay + jnp.where(same_head, upd, 0.0)
    for g in range(grp):
        o_g = jnp.concatenate([o[g * n_groups + p] for p in range(n_groups)], axis=1)
        o_ref[g] = o_g + other_ref[g] if add_to else o_g

    @pl.when(c == pl.num_programs(1) - 1)
    def _():
        s_fin = s_scr[...]
        for g in range(grp):
            for p in range(n_groups):
                group = s_fin[g * n_groups + p]
                for j in range(hp):
                    part = slice(j * A_HEAD_DIM, (j + 1) * A_HEAD_DIM)
                    sf_ref[g, hp * p + j] = group[part, part]


def _wkv_scan(r, v, nkk, kd, b, lw, s0, reverse, add_to=None):
    bsz, t, aw = r.shape
    ch = SCAN_CHUNK
    assert t % ch == 0 and ch == A_HEAD_DIM
    nc = t // ch
    w = SCAN_PACK * A_HEAD_DIM
    n_groups = aw // w
    grp = SCAN_GROUP if bsz % SCAN_GROUP == 0 else 1
    tmap = (lambda i, c: (i, nc - 1 - c, 0)) if reverse else (lambda i, c: (i, c, 0))
    seq = pl.BlockSpec((grp, ch, aw), tmap)
    st_in = pl.BlockSpec((grp, n_groups, w, w), lambda i, c: (i, 0, 0, 0))
    st_out = pl.BlockSpec((grp,) + s0.shape[1:], lambda i, c: (i, 0, 0, 0))
    extra = [] if add_to is None else [add_to]
    return pl.pallas_call(
        functools.partial(_scan_kernel, reverse=reverse, add_to=add_to is not None),
        grid=(bsz // grp, nc),
        in_specs=[seq] * 6 + [st_in] + [seq] * len(extra),
        out_specs=[seq, st_out],
        out_shape=[jax.ShapeDtypeStruct((bsz, t, aw), F32), jax.ShapeDtypeStruct(s0.shape, F32)],
        scratch_shapes=[pltpu.VMEM((grp * n_groups, w, w), F32)],
        compiler_params=_cparams("arbitrary", "arbitrary"),
        name="wkv_scan_bwd" if reverse else "wkv_scan_fwd",
    )(r, v, nkk, kd, b, lw, _group_states(s0, SCAN_PACK), *extra)


def _group_states(s, hp):
    b, h, d, _ = s.shape
    eye = jnp.eye(hp, dtype=s.dtype)[None, None, :, None, :, None]
    return (s.reshape(b, h // hp, hp, d, 1, d) * eye).reshape(b, h // hp, hp * d, hp * d)


def _denominator_lane(width):
    lane = lax.broadcasted_iota(I32, (1, width), 1)
    return jnp.where((lane & (HEAD_PAD - 1)) == V_HEAD, 1.0, 0.0)


def _rope_lanes(x, cos_t, sin_up, sin_dn):
    w = x.shape[1]
    half = QK_ROPE // 2
    return x * cos_t + pltpu.roll(x, half, 1) * sin_up + pltpu.roll(x, w - half, 1) * sin_dn


def _mla_prep_kernel(*refs, rope):
    if rope:
        (m_ref, gq_ref, gkv_ref, wuq_ref, wk_ref, wv_ref, p_ref,
         qc_ref, qu_ref, qd_ref, kc_ref, ku_ref, kd_ref, q_o, k_o, v_o, ckv_o) = refs
    else:
        (m_ref, gq_ref, gkv_ref, wuq_ref, wk_ref, wv_ref, p_ref, q_o, k_o, v_o, ckv_o) = refs
    m = m_ref[...]
    qn = _rms(m[:, :Q_LORA], gq_ref[...])
    ckv = _rms(m[:, Q_LORA:Q_LORA + KV_LORA], gkv_ref[...])
    kp = m[:, Q_LORA + KV_LORA:]
    q = _dot(qn.astype(BF16), wuq_ref[...]) * ATTN_SCALE
    if rope:
        tile = lambda ref: jnp.concatenate([ref[...]] * B_HEADS, axis=1)
        q = _rope_lanes(q, tile(qc_ref), tile(qu_ref), tile(qd_ref))
        kp = _rope_lanes(kp, kc_ref[...], ku_ref[...], kd_ref[...])
    cb = ckv.astype(BF16)
    q_o[...] = q.astype(BF16)
    k_o[...] = (_dot(cb, wk_ref[...]) + _dot(kp.astype(BF16), p_ref[...])).astype(BF16)
    v_o[...] = (_dot(cb, wv_ref[...]) + _denominator_lane(v_o.shape[1])).astype(BF16)
    ckv_o[...] = ckv


def _mla_prep(m2, g_q, g_kv, wuq_p, wk_p, wv_p, place, rope_tabs, tb, t):
    n = m2.shape[0]
    hw = B_HEADS * HEAD_PAD
    full = lambda a: pl.BlockSpec(a.shape, lambda i: (0, 0))
    ins = [m2, g_q, g_kv, wuq_p, wk_p, wv_p, place]
    specs = [pl.BlockSpec((tb, M_COLS), lambda i: (i, 0))] + [full(a) for a in ins[1:]]
    if rope_tabs is not None:
        nbt = t // tb
        ins += list(rope_tabs)
        specs += [pl.BlockSpec((tb, HEAD_PAD), lambda i: (i % nbt, 0))] * 6
    big = pl.BlockSpec((tb, hw), lambda i: (i, 0))
    return pl.pallas_call(
        functools.partial(_mla_prep_kernel, rope=rope_tabs is not None),
        grid=(n // tb,),
        in_specs=specs,
        out_specs=[big, big, big, pl.BlockSpec((tb, KV_LORA), lambda i: (i, 0))],
        out_shape=[jax.ShapeDtypeStruct((n, hw), BF16)] * 3 + [jax.ShapeDtypeStruct((n, KV_LORA), F32)],
        compiler_params=_cparams("arbitrary"),
        name="mla_prep",
    )(*ins)


def _kv_up_kernel(ckv_ref, kp_ref, wk_ref, wv_ref, p_ref, k_o, v_o):
    cb = ckv_ref[...].astype(BF16)
    k_o[...] = (_dot(cb, wk_ref[...]) + _dot(kp_ref[...].astype(BF16), p_ref[...])).astype(BF16)
    v_o[...] = (_dot(cb, wv_ref[...]) + _denominator_lane(v_o.shape[1])).astype(BF16)


def _kv_up(ckv2, kpe_pad, wk_p, wv_p, place, tb):
    n = ckv2.shape[0]
    hw = B_HEADS * HEAD_PAD
    full = lambda a: pl.BlockSpec(a.shape, lambda i: (0, 0))
    big = pl.BlockSpec((tb, hw), lambda i: (i, 0))
    return pl.pallas_call(
        _kv_up_kernel,
        grid=(n // tb,),
        in_specs=[pl.BlockSpec((tb, KV_LORA), lambda i: (i, 0)),
                  pl.BlockSpec((tb, HEAD_PAD), lambda i: (i, 0)),
                  full(wk_p), full(wv_p), full(place)],
        out_specs=[big, big],
        out_shape=[jax.ShapeDtypeStruct((n, hw), BF16)] * 2,
        compiler_params=_cparams("arbitrary"),
        name="kv_up",
    )(ckv2, kpe_pad, wk_p, wv_p, place)


def _attn_kernel(*refs, two):
    if two:
        q_ref, k1_ref, v1_ref, k2_ref, v2_ref, o_ref = refs
    else:
        q_ref, k1_ref, v1_ref, o_ref = refs
        heads = [slice(h * HEAD_PAD, (h + 1) * HEAD_PAD) for h in range(B_HEADS)]
        scores = [_dot_nt(q_ref[0, :, hs], k1_ref[0, :, hs]) for hs in heads]
        probs = [jnp.exp(s - jnp.max(s, axis=-1, keepdims=True)).astype(BF16) for s in scores]
        accs = [_dot(p, v1_ref[0, :, hs]) for p, hs in zip(probs, heads)]
        _store_heads(o_ref, [acc / acc[:, V_HEAD:V_HEAD + 1] for acc in accs])
        return
    outs = []
    for h in range(B_HEADS):
        hs = slice(h * HEAD_PAD, (h + 1) * HEAD_PAD)
        q = q_ref[0, :, hs]
        s1 = _dot_nt(q, k1_ref[0, :, hs])
        mx = jnp.max(s1, axis=-1, keepdims=True)
        if two:
            s2 = _dot_nt(q, k2_ref[0, :, hs])
            mx = jnp.maximum(mx, jnp.max(s2, axis=-1, keepdims=True))
        acc = _dot(jnp.exp(s1 - mx).astype(BF16), v1_ref[0, :, hs])
        if two:
            acc = acc + _dot(jnp.exp(s2 - mx).astype(BF16), v2_ref[0, :, hs])
        outs.append(acc / acc[:, V_HEAD:V_HEAD + 1])
        if h % 2 == 1:
            _store_heads(o_ref, outs, first_head=h - 1)
            outs = []


def _store_heads(o_ref, outs, first_head=0):
    for i in range(0, len(outs), 2):
        lanes = slice((first_head + i) * V_HEAD, (first_head + i + 2) * V_HEAD)
        o_ref[0, :, lanes] = jnp.concatenate([outs[i][:, :V_HEAD], outs[i + 1][:, :V_HEAD]], axis=1).astype(BF16)


def _attention(q3, k1, v1, k2, v2, tq):
    b, t, hw = q3.shape
    two = k2 is not None
    qspec = pl.BlockSpec((1, tq, hw), lambda i, j: (i, j, 0))
    kv = lambda a: pl.BlockSpec((1, a.shape[1], hw), lambda i, j: (i, 0, 0))
    ins = [q3, k1, v1] + ([k2, v2] if two else [])
    return pl.pallas_call(
        functools.partial(_attn_kernel, two=two),
        grid=(b, t // tq),
        in_specs=[qspec] + [kv(a) for a in ins[1:]],
        out_specs=pl.BlockSpec((1, tq, B_HEADS * V_HEAD), lambda i, j: (i, j, 0)),
        out_shape=jax.ShapeDtypeStruct((b, t, B_HEADS * V_HEAD), BF16),
        compiler_params=_cparams("arbitrary", "arbitrary"),
        name="attention",
    )(*ins)


def _post_mix_kernel(o_ref, bonus_ref, g_ref, att_ref, gates_ref, x_ref,
                     g1_ref, sc2_ref, sh2_ref, lnw_ref, lnb_ref, bd_ref, wa_ref, wb_ref, wo_ref,
                     gpost_ref, gpre_ref, wr_ref, x1_o, h2_o, aff_o):
    bd = bd_ref[...]
    inv_n = 1.0 / A_HEAD_DIM
    o = o_ref[...]
    mu = _dot_x2(o, bd) * inv_n
    oc = o - mu
    var = _dot((oc * oc).astype(BF16), bd) * inv_n
    on = (oc * lax.rsqrt(var + GN_EPS)) * lnw_ref[...] + lnb_ref[...]
    ya = _dot(((on + bonus_ref[...]) * g_ref[...]).astype(BF16), wa_ref[...])
    yb = _dot(att_ref[...], wb_ref[...])
    d = ya.shape[1]
    gates = gates_ref[...].astype(F32)
    mix = jax.nn.sigmoid(gates[:, :d]) * ya + jax.nn.sigmoid(gates[:, d:]) * yb
    y = _dot(mix.astype(BF16), wo_ref[...])
    x1 = x_ref[...] + g1_ref[0] * _rms(y, gpost_ref[...])
    x1_o[...] = x1
    h2 = _rms(x1, gpre_ref[...]) * (1.0 + sc2_ref[0]) + sh2_ref[0]
    h2_o[...] = h2.astype(BF16)
    hh, hl = _split2(h2)
    wh, wl = _split2(wr_ref[...])
    logits = _dot_nt(wh, hh) + (_dot_nt(wh, hl) + _dot_nt(wl, hh))
    z = jnp.exp(logits - jnp.max(logits, axis=0, keepdims=True))
    aff_o[...] = z / jnp.sum(z, axis=0, keepdims=True)


def _post_mix(o, bonus, g, att, gates, x2, g1, sc2, sh2, ln_w, ln_b, bd, wa, wb, wo,
              g_post, g_pre, w_router_t, tb, blocks_per_batch):
    n, d = x2.shape
    tok = lambda a: pl.BlockSpec((tb, a.shape[1]), lambda i: (i, 0))
    full = lambda a: pl.BlockSpec(a.shape, lambda i: (0, 0))
    ms = lambda a: _mod_spec(a, blocks_per_batch)
    return pl.pallas_call(
        _post_mix_kernel,
        grid=(n // tb,),
        in_specs=[tok(o), tok(bonus), tok(g), tok(att), tok(gates), tok(x2),
                  ms(g1), ms(sc2), ms(sh2), full(ln_w), full(ln_b), full(bd), full(wa), full(wb),
                  full(wo), full(g_post), full(g_pre), full(w_router_t)],
        out_specs=[pl.BlockSpec((tb, d), lambda i: (i, 0)),
                   pl.BlockSpec((tb, d), lambda i: (i, 0)),
                   pl.BlockSpec((N_EXPERTS, tb), lambda i: (0, i))],
        out_shape=[jax.ShapeDtypeStruct((n, d), F32),
                   jax.ShapeDtypeStruct((n, d), BF16),
                   jax.ShapeDtypeStruct((N_EXPERTS, n), F32)],
        compiler_params=_cparams("arbitrary"),
        name="post_mix",
    )(o, bonus, g, att, gates, x2, g1, sc2, sh2, ln_w, ln_b, bd, wa, wb, wo,
      g_post, g_pre, w_router_t)


def _select_kernel(a_ref, pos_ref, sel_ref, *, cap, n_chunks):
    a = a_ref[...]
    rows = a.shape[0]
    a3 = a.reshape(N_EXPERTS, n_chunks, LANES)

    def count(mask):
        c = jnp.sum(jnp.where(mask, 1.0, 0.0), axis=2, keepdims=True)
        return jnp.sum(c, axis=1, keepdims=True)

    def body(i, thr):
        cand = thr | jnp.left_shift(jnp.int32(1), 30 - i)
        return jnp.where(count(a3 >= pltpu.bitcast(cand, F32)) >= cap, cand, thr)

    thr = pltpu.bitcast(lax.fori_loop(0, 31, body, jnp.zeros((N_EXPERTS, 1, LANES), I32)), F32)
    gt = a3 > thr
    eq = a3 == thr
    need = cap - count(gt)

    li = lax.broadcasted_iota(I32, (LANES, LANES), 0)
    lj = lax.broadcasted_iota(I32, (LANES, LANES), 1)
    lane_before = jnp.where(li < lj, 1.0, 0.0).astype(BF16)
    lane_all = jnp.ones((LANES, LANES), BF16)
    ci = lax.broadcasted_iota(I32, (n_chunks, n_chunks), 0)
    cj = lax.broadcasted_iota(I32, (n_chunks, n_chunks), 1)
    chunk_before = jnp.where(cj < ci, 1.0, 0.0).astype(BF16)

    def prefix(flags3):
        f2 = flags3.reshape(rows, LANES).astype(BF16)
        within = _dot(f2, lane_before).reshape(N_EXPERTS, n_chunks, LANES)
        tot = _dot(f2, lane_all).astype(BF16).reshape(N_EXPERTS, n_chunks, LANES)
        offs = [_dot(chunk_before, tot[e]) for e in range(N_EXPERTS)]
        return within + jnp.stack(offs, axis=0)

    eqf = jnp.where(eq, 1.0, 0.0)
    tie_ok = jnp.where(prefix(eqf) < need, eqf, 0.0)
    sel = jnp.where(gt, 1.0, tie_ok)
    pos_ref[...] = prefix(sel).reshape(rows, LANES).astype(I32)
    sel_ref[...] = sel.reshape(rows, LANES).astype(I32)


def _select(aff_t, cap):
    e, n = aff_t.shape
    n_chunks = n // LANES
    rows = e * n_chunks
    a2 = aff_t.reshape(rows, LANES)
    spec = pl.BlockSpec((rows, LANES), lambda i: (0, 0))
    pos, sel = pl.pallas_call(
        functools.partial(_select_kernel, cap=cap, n_chunks=n_chunks),
        grid=(1,),
        in_specs=[spec],
        out_specs=[spec, spec],
        out_shape=[jax.ShapeDtypeStruct((rows, LANES), I32)] * 2,
        compiler_params=_cparams("arbitrary"),
        name="expert_select",
    )(a2)
    return pos.reshape(e, n), sel.reshape(e, n)


def _ffn_kernel(lo_ref, hi_ref, slot_ref, aff_ref, h_ref, wg_ref, wu_ref, wd_ref, y_ref, acc_ref, wacc_ref,
                *, n_tiles, sub):
    e = pl.program_id(0)
    j = pl.program_id(1)
    rows = acc_ref.shape[0]
    n_blocks, tb = slot_ref.shape[1], slot_ref.shape[2]
    n_sub = rows // sub
    acc_ref[...] = jnp.zeros_like(acc_ref)
    wacc_ref[...] = jnp.zeros_like(wacc_ref)
    first = [lo_ref[(e * n_tiles + j) * n_sub + si] for si in range(n_sub)]
    last = [hi_ref[(e * n_tiles + j) * n_sub + si] for si in range(n_sub)]
    trips = functools.reduce(jnp.maximum, [hi - lo for lo, hi in zip(first, last)])
    row0 = lax.broadcasted_iota(I32, (sub, tb), 0) + j * rows

    def gather_step(i, carry):
        for si in range(n_sub):
            b = first[si] + i
            live = b < last[si]
            b = jnp.minimum(b, n_blocks - 1)
            part = slice(si * sub, (si + 1) * sub)
            hit = (row0 + si * sub) == jnp.where(live, slot_ref[0, pl.ds(b, 1), :], -1)
            tokens = h_ref[pl.ds(pl.multiple_of(b * tb, tb), tb), :]
            acc_ref[part, :] += _dot(jnp.where(hit, 1.0, 0.0).astype(BF16), tokens)
            wacc_ref[part, :] += jnp.sum(jnp.where(hit, aff_ref[0, pl.ds(b, 1), :], 0.0), axis=1, keepdims=True)
        return carry

    lax.fori_loop(0, trips, gather_step, 0)
    xe = acc_ref[...].astype(BF16)
    gate = _dot(xe, wg_ref[0])
    up = _dot(xe, wu_ref[0])
    hid = (gate * jax.nn.sigmoid(gate)) * up
    y_ref[0] = (_dot(hid.astype(BF16), wd_ref[0]) * wacc_ref[...]).astype(BF16)


def _expert_ffn(pos, slot, aff_t, h2, wg, wu, wd, cap, tb, rt):
    n, d = h2.shape
    nb = n // tb
    e, _, f = wg.shape
    n_tiles = cap // rt
    sub = min(GATHER_ROWS, rt)
    starts = jnp.concatenate([pos[:, ::tb], jnp.full((e, 1), cap, I32)], axis=1)
    edges = jnp.arange(cap // sub, dtype=I32) * sub
    lo = jnp.sum((starts[:, None, 1:] <= edges[None, :, None]).astype(I32), axis=2)
    hi = jnp.sum((starts[:, None, :-1] < (edges + sub)[None, :, None]).astype(I32), axis=2)
    grid_spec = pltpu.PrefetchScalarGridSpec(
        num_scalar_prefetch=2,
        grid=(e, n_tiles),
        in_specs=[pl.BlockSpec((1, nb, tb), lambda ei, j, lo, hi: (ei, 0, 0)),
                  pl.BlockSpec((1, nb, tb), lambda ei, j, lo, hi: (ei, 0, 0)),
                  pl.BlockSpec((n, d), lambda ei, j, lo, hi: (0, 0), pipeline_mode=pl.Buffered(1)),
                  pl.BlockSpec((1, d, f), lambda ei, j, lo, hi: (ei, 0, 0)),
                  pl.BlockSpec((1, d, f), lambda ei, j, lo, hi: (ei, 0, 0)),
                  pl.BlockSpec((1, f, d), lambda ei, j, lo, hi: (ei, 0, 0))],
        out_specs=pl.BlockSpec((1, rt, d), lambda ei, j, lo, hi: (ei, j, 0)),
        scratch_shapes=[pltpu.VMEM((rt, d), F32), pltpu.VMEM((rt, 1), F32)],
    )
    return pl.pallas_call(
        functools.partial(_ffn_kernel, n_tiles=n_tiles, sub=sub),
        grid_spec=grid_spec,
        out_shape=jax.ShapeDtypeStruct((e, cap, d), BF16),
        compiler_params=_cparams("arbitrary", "arbitrary"),
        name="expert_ffn",
    )(lo.reshape(-1), hi.reshape(-1), slot.reshape(e, nb, tb), aff_t.reshape(e, nb, tb), h2, wg, wu, wd)


def _combine_kernel(t0_ref, fetch_ref, need_ref, slot_ref, x_ref, g2_ref, gp_ref, *rest, n_win):
    y_refs, o_ref, acc_ref = rest[:-2], rest[-2], rest[-1]
    blk = pl.program_id(0)
    n_blk = pl.num_programs(0)
    tb = slot_ref.shape[0]
    rows = y_refs[0].shape[1]
    lane = lax.broadcasted_iota(I32, (tb, rows), 1)

    def picked(e, k):
        slot = slot_ref[:, e:e + 1] - (t0_ref[blk * N_EXPERTS + e] + k) * rows
        return _dot(jnp.where(lane == slot, 1.0, 0.0).astype(BF16), y_refs[e * n_win + k][0])

    acc = picked(0, 0)
    for e in range(1, N_EXPERTS):
        acc = acc + picked(e, 0)
    acc_ref[...] = acc
    for k in range(1, n_win):
        for e in range(N_EXPERTS):
            @pl.when(need_ref[(k * n_blk + blk) * N_EXPERTS + e] != 0)
            def _():
                acc_ref[...] += picked(e, k)
    o_ref[...] = x_ref[...] + g2_ref[0] * _rms(acc_ref[...], gp_ref[...])


def _combine_windows(pos, tb, rt, cap):
    n_tiles = cap // rt
    lo = pos[:, ::tb]
    hi = jnp.concatenate([lo[:, 1:], jnp.full((lo.shape[0], 1), cap, I32)], axis=1)
    t0 = jnp.minimum(lo // rt, n_tiles - 1)
    t_last = jnp.where(hi > lo, (hi - 1) // rt, t0)
    fetch, need = [t0], [jnp.ones_like(t0)]
    for k in range(1, tb // rt + 1):
        need.append(t_last >= t0 + k)
        fetch.append(lax.cummax(jnp.where(need[k], t0 + k, 0), axis=1))
    flat = lambda a: a.T.reshape(-1).astype(I32)
    return flat(t0), jnp.concatenate([flat(f) for f in fetch]), jnp.concatenate([flat(f) for f in need])


def _combine(windows, slot_t, ye, x1, g2, g_post, tb, rt, blocks_per_batch):
    t0, fetch, need = windows
    n, d = x1.shape
    e, cap, _ = ye.shape
    nb = n // tb
    n_win = tb // rt + 1
    if g2.shape[0] == 1:
        g2_spec = pl.BlockSpec((1, 1, d), lambda b, t, f, nd: (0, 0, 0))
    else:
        g2_spec = pl.BlockSpec((1, 1, d), lambda b, t, f, nd: (b // blocks_per_batch, 0, 0))

    def window(ei, k):
        return pl.BlockSpec((1, rt, d), lambda b, t, f, nd: (ei, f[(k * nb + b) * e + ei], 0))

    grid_spec = pltpu.PrefetchScalarGridSpec(
        num_scalar_prefetch=3,
        grid=(nb,),
        in_specs=[pl.BlockSpec((tb, e), lambda b, t, f, nd: (b, 0)),
                  pl.BlockSpec((tb, d), lambda b, t, f, nd: (b, 0)),
                  g2_spec,
                  pl.BlockSpec((1, d), lambda b, t, f, nd: (0, 0))]
                 + [window(ei, k) for ei in range(e) for k in range(n_win)],
        out_specs=pl.BlockSpec((tb, d), lambda b, t, f, nd: (b, 0)),
        scratch_shapes=[pltpu.VMEM((tb, d), F32)],
    )
    return pl.pallas_call(
        functools.partial(_combine_kernel, n_win=n_win),
        grid_spec=grid_spec,
        out_shape=jax.ShapeDtypeStruct((n, d), F32),
        compiler_params=_cparams("arbitrary"),
        name="moe_combine",
    )(t0, fetch, need, slot_t, x1, g2, g_post, *([ye] * (e * n_win)))


def _rope_tables(t):
    half = QK_ROPE // 2
    pos = jnp.arange(t)
    row = (pos // GRID_W).astype(F32)
    col = (pos % GRID_W).astype(F32)
    inv = ROPE_BASE ** (-jnp.arange(0, half, 2, dtype=F32) / half)
    ang = jnp.concatenate([row[:, None] * inv, col[:, None] * inv], axis=-1)
    cos, sin = jnp.cos(ang), jnp.sin(ang)

    def tabs(first_lane):
        left, right = first_lane, HEAD_PAD - first_lane - 2 * half
        fill = lambda value, width: jnp.full((t, width), value, F32)
        c = jnp.concatenate([fill(1.0, left), cos, cos, fill(1.0, right)], axis=1)
        up = jnp.concatenate([fill(0.0, left + half), sin, fill(0.0, right)], axis=1)
        dn = jnp.concatenate([fill(0.0, left), -sin, fill(0.0, half + right)], axis=1)
        return c, up, dn

    return tabs(QK_NOPE + QK_ROPE) + tabs(QK_ROPE)


def _layout_weights(w_in, w_uq, w_ukv, w_branch_b, w_up, a_up):
    d = w_in.shape[0]
    deint = jnp.concatenate([jnp.arange(0, QK_ROPE, 2), jnp.arange(1, QK_ROPE, 2)])
    kpe0 = RWKV_COLS + Q_LORA + KV_LORA
    kpe_cols = w_in[:, kpe0:kpe0 + QK_ROPE]
    w_in_p = jnp.concatenate(
        [w_in[:, :kpe0 + QK_ROPE], kpe_cols[:, deint],
         jnp.zeros((d, M_COLS - MLA_COLS - QK_ROPE), F32), w_in[:, RWKV_COLS + MLA_COLS:]], axis=1)
    uq = w_uq.reshape(Q_LORA, B_HEADS, QK_NOPE + QK_ROPE)
    wuq_p = jnp.concatenate([uq, uq[:, :, QK_NOPE:][:, :, deint]], axis=2).reshape(Q_LORA, B_HEADS * HEAD_PAD)
    ukv = w_ukv.reshape(KV_LORA, B_HEADS, QK_NOPE + V_HEAD)
    zpad = jnp.zeros((KV_LORA, B_HEADS, HEAD_PAD - QK_NOPE), F32)
    wk_p = jnp.concatenate([ukv[:, :, :QK_NOPE], zpad], axis=2).reshape(KV_LORA, B_HEADS * HEAD_PAD)
    wv_p = jnp.concatenate([ukv[:, :, QK_NOPE:], zpad], axis=2).reshape(KV_LORA, B_HEADS * HEAD_PAD)
    wb_p = w_branch_b
    eye = jnp.eye(QK_ROPE, dtype=F32)
    z = jnp.zeros((QK_ROPE, QK_ROPE), F32)
    head_raw = jnp.concatenate([jnp.zeros((QK_ROPE, QK_NOPE), F32), eye, z], axis=1)
    head_rot = jnp.concatenate([jnp.zeros((QK_ROPE, QK_NOPE), F32), z, eye], axis=1)
    zrows = jnp.zeros((HEAD_PAD - 2 * QK_ROPE, B_HEADS * HEAD_PAD), F32)
    zr = jnp.zeros((QK_ROPE, B_HEADS * HEAD_PAD), F32)
    place_raw = jnp.concatenate([jnp.tile(head_raw, (1, B_HEADS)), zr, zrows], axis=0)
    place_rot = jnp.concatenate([zr, jnp.tile(head_rot, (1, B_HEADS)), zrows], axis=0)
    zl = jnp.zeros((DECAY_LORA, A_WIDTH), F32)
    wup_p = jnp.stack([jnp.concatenate([w_up[0], zl]), jnp.concatenate([zl, w_up[1]])])
    aup_p = jnp.stack([jnp.concatenate([a_up[0], zl]), jnp.concatenate([zl, a_up[1]])])
    bf = lambda a: a.astype(BF16)
    return (bf(w_in_p), bf(wuq_p), bf(wk_p), bf(wv_p), bf(wb_p), bf(place_raw), bf(place_rot),
            bf(wup_p), bf(aup_p))


def _group_layer(x, mods, lw, s0_f, s0_b, ctx_kv, rope_tabs):
    bsz, t, d = x.shape
    n = bsz * t
    tb = min(TOKEN_BLOCK, t)
    tw = min(ROW_BLOCK, t)
    bpb = t // tb
    bpw = t // tw
    tf = min(ROW_BLOCK, n) if mods[0].shape[0] == 1 and rope_tabs is None else tw
    sh1, sc1, g1, sh2, sc2, g2 = mods
    x2 = x.reshape(n, d)
    u, m, gates, edges = _inproj(x2, sc1, sh1, lw["g_pre_mix"], lw["w_in_p"], tf, bpw)
    prep = _rwkv_prep(u.reshape(bsz, t, RWKV_COLS), edges, lw["shift_w"], lw["w0"], lw["a0"], lw["wup_p"],
                      lw["aup_p"], lw["g_up"], lw["k_k"], lw["k_a"], lw["r_k"], lw["bd"], tw)
    r, v, nkk, kd_f, kd_b, b_f, b_b, lw_f, lw_b, g, bonus = prep
    o_f, s_f = _wkv_scan(r, v, nkk, kd_f, b_f, lw_f, s0_f, reverse=False)
    o_sum, s_b = _wkv_scan(r, v, nkk, kd_b, b_b, lw_b, s0_b, reverse=True, add_to=o_f)
    place = lw["place_raw"] if rope_tabs is None else lw["place_rot"]
    q, k, vv, ckv = _mla_prep(m, lw["g_qnorm"], lw["g_kvnorm"], lw["wuq_p"], lw["wk_p"], lw["wv_p"],
                              place, rope_tabs, tf, t)
    hw = B_HEADS * HEAD_PAD
    k2, v2 = ctx_kv if ctx_kv is not None else (None, None)
    att = _attention(q.reshape(bsz, t, hw), k.reshape(bsz, t, hw), vv.reshape(bsz, t, hw), k2, v2, tw)
    two = lambda a: a.reshape(n, a.shape[-1])
    x1, h2, aff_t = _post_mix(two(o_sum), two(bonus), two(g), two(att), gates, x2,
                              g1, sc2, sh2, lw["ln_x_w"], lw["ln_x_b"], lw["bd"], lw["w_branch_a"],
                              lw["wb_p"], lw["w_out"], lw["g_post_mix"], lw["g_pre_ffn"],
                              lw["w_router_t"], tf, bpw)
    cap = CAPACITY_FACTOR * n // N_EXPERTS
    rt = min(EXPERT_TILE, cap)
    pos, sel = _select(aff_t, cap)
    slot = jnp.where(sel > 0, pos, -1)
    ye = _expert_ffn(pos, slot, aff_t, h2, lw["w_exp_gate"], lw["w_exp_up"], lw["w_exp_down"], cap,
                     min(GATHER_BLOCK, n), min(FFN_TILE, cap))
    out = _combine(_combine_windows(pos, tb, rt, cap), slot.T, ye, x1, g2, lw["g_post_ffn"], tb, rt, bpb)
    kpe = m[:, Q_LORA + KV_LORA:Q_LORA + KV_LORA + QK_ROPE]
    return out.reshape(bsz, t, d), (ckv.reshape(bsz, t, KV_LORA), kpe.reshape(bsz, t, QK_ROPE), s_f, s_b)


def kernel(x_prompt, x_sample, cache_ckv, cache_kpe, state_wkv_fwd, state_wkv_bwd, c, c_ctx,
           w_mod, b_mod, g_pre_mix, g_post_mix, g_pre_ffn, g_post_ffn, w_in, shift_w,
           w0, w_up, a0, a_up, g_up, k_k, k_a, r_k, ln_x_w, ln_x_b, w_branch_a,
           g_qnorm, w_uq, g_kvnorm, w_ukv, w_branch_b, w_out,
           w_router, w_exp_gate, w_exp_up, w_exp_down):
    depth = w_mod.shape[0]
    d = x_prompt.shape[-1]
    dec_b, dec_t = x_sample.shape[0], x_sample.shape[1]
    xp, xs = x_prompt, x_sample
    c_rows = jnp.concatenate([c, c_ctx[None, :],
                              jnp.zeros((-(dec_b + 1) % 8, d), F32)], axis=0)
    rope_tabs = _rope_tables(dec_t)
    ii = lax.broadcasted_iota(I32, (A_WIDTH, A_WIDTH), 0) // A_HEAD_DIM
    jj = lax.broadcasted_iota(I32, (A_WIDTH, A_WIDTH), 1) // A_HEAD_DIM
    bd = (ii == jj).astype(BF16)
    row = lambda a: a.reshape(1, -1)
    bf = lambda a: a.astype(BF16)
    ckv_l, kpe_l, sf_l, sb_l = [], [], [], []
    for l in range(depth):
        (w_in_p, wuq_p, wk_p, wv_p, wb_p, place_raw, place_rot, wup_p, aup_p) = _layout_weights(
            w_in[l], w_uq[l], w_ukv[l], w_branch_b[l], w_up[l], a_up[l])
        lw = {
            "g_pre_mix": row(g_pre_mix[l]), "g_post_mix": row(g_post_mix[l]),
            "g_pre_ffn": row(g_pre_ffn[l]), "g_post_ffn": row(g_post_ffn[l]),
            "w_in_p": w_in_p, "shift_w": shift_w[l], "w0": w0[l], "a0": a0[l],
            "wup_p": wup_p, "aup_p": aup_p, "g_up": bf(g_up[l]),
            "k_k": row(k_k[l]), "k_a": row(k_a[l]), "r_k": row(r_k[l]), "bd": bd,
            "ln_x_w": row(ln_x_w[l]), "ln_x_b": row(ln_x_b[l]), "w_branch_a": bf(w_branch_a[l]),
            "g_qnorm": row(g_qnorm[l]), "g_kvnorm": row(g_kvnorm[l]),
            "wuq_p": wuq_p, "wk_p": wk_p, "wv_p": wv_p, "wb_p": wb_p,
            "place_raw": place_raw, "place_rot": place_rot,
            "w_out": bf(w_out[l]), "w_router_t": w_router[l].T,
            "w_exp_gate": bf(w_exp_gate[l]), "w_exp_up": bf(w_exp_up[l]), "w_exp_down": bf(w_exp_down[l]),
        }
        mod = _modulation(c_rows, w_mod[l], b_mod[l])
        mods_lat = [mod[:dec_b, i * d:(i + 1) * d].reshape(dec_b, 1, d) for i in range(6)]
        mods_ctx = [mod[dec_b:dec_b + 1, i * d:(i + 1) * d].reshape(1, 1, d) for i in range(6)]
        zeros_state = jnp.zeros((xp.shape[0], A_HEADS, A_HEAD_DIM, A_HEAD_DIM), F32)
        xp, (ckv, kpe, s_f, s_b) = _group_layer(xp, mods_ctx, lw, zeros_state, zeros_state, None, None)
        ckv_l.append(ckv)
        kpe_l.append(kpe)
        sf_l.append(s_f)
        sb_l.append(s_b)
        past = cache_ckv.shape[2]
        kpe_pad = jnp.concatenate(
            [cache_kpe[:, l], jnp.zeros((dec_b, past, HEAD_PAD - QK_ROPE), F32)], axis=-1)
        k_ctx, v_ctx = _kv_up(cache_ckv[:, l].reshape(dec_b * past, KV_LORA),
                              kpe_pad.reshape(dec_b * past, HEAD_PAD), wk_p, wv_p, place_raw,
                              min(TOKEN_BLOCK, past))
        hw = B_HEADS * HEAD_PAD
        ctx_kv = (k_ctx.reshape(dec_b, past, hw), v_ctx.reshape(dec_b, past, hw))
        xs, _ = _group_layer(xs, mods_lat, lw, state_wkv_fwd[:, l], state_wkv_bwd[:, l], ctx_kv, rope_tabs)
    return (xp, xs, jnp.stack(ckv_l, axis=1), jnp.stack(kpe_l, axis=1),
            jnp.stack(sf_l, axis=1), jnp.stack(sb_l, axis=1))
```

```python
import functools
import math

import jax
import jax.numpy as jnp
from jax import lax
from jax.experimental import pallas as pl
from jax.experimental.pallas import tpu as pltpu

F32 = jnp.float32
BF16 = jnp.bfloat16
I32 = jnp.int32

GRID_W = 64
A_HEADS = 8
A_HEAD_DIM = 64
A_WIDTH = A_HEADS * A_HEAD_DIM
DECAY_LORA = 64
ICLR_LORA = 64
GATE_LORA = 128
DECAY_SCALE = 0.6065306597126334
GN_EPS = 64e-5
B_HEADS = 8
Q_LORA = 256
KV_LORA = 128
QK_NOPE = 64
QK_ROPE = 32
V_HEAD = 64
ROPE_BASE = 10000.0
ATTN_SCALE = 1.0 / math.sqrt(QK_NOPE + QK_ROPE)
N_EXPERTS = 16
CAPACITY_FACTOR = 2
EPS = 1e-6
RWKV_COLS = 3 * A_WIDTH + 2 * DECAY_LORA + 2 * ICLR_LORA + GATE_LORA
MLA_COLS = Q_LORA + KV_LORA + QK_ROPE

LANES = 128
HEAD_PAD = 128
M_COLS = 512
VMEM_LIMIT = 56 * 1024 * 1024

SCAN_CHUNK = 64
SCAN_GROUP = 8
SCAN_PACK = 2
TOKEN_BLOCK = 512
ROW_BLOCK = 512
EXPERT_TILE = 256
FFN_TILE = 256
GATHER_BLOCK = 512
GATHER_ROWS = 128


def _cparams(*sem):
    return pltpu.CompilerParams(dimension_semantics=sem, vmem_limit_bytes=VMEM_LIMIT)


def _dot(a, b):
    return jnp.dot(a, b, preferred_element_type=F32)


def _dot_nt(a, b):
    return lax.dot_general(a, b, (((1,), (1,)), ((), ())), preferred_element_type=F32)


def _dot_tn(a, b):
    return lax.dot_general(a, b, (((0,), (0,)), ((), ())), preferred_element_type=F32)


def _split2(x):
    hi = x.astype(BF16)
    lo = (x - hi.astype(F32)).astype(BF16)
    return hi, lo


def _dot_x2(a, b_bf16):
    hi, lo = _split2(a)
    return _dot(hi, b_bf16) + _dot(lo, b_bf16)


def _dot_f32(a, b):
    ah, al = _split2(a)
    bh, bl = _split2(b)
    return _dot(ah, bh) + (_dot(ah, bl) + _dot(al, bh))


def _rms(x, g):
    return (x * lax.rsqrt(jnp.mean(x * x, axis=-1, keepdims=True) + EPS)) * g


def _mod_kernel(c_ref, w_ref, b_ref, o_ref):
    c = c_ref[...]
    s = c * jax.nn.sigmoid(c)
    o_ref[...] = _dot_f32(s, w_ref[...]) + b_ref[...]


def _modulation(c_rows, w_mod, b_mod):
    rows, d = c_rows.shape
    n_out = w_mod.shape[1]
    tn = n_out // 8
    return pl.pallas_call(
        _mod_kernel,
        grid=(n_out // tn,),
        in_specs=[pl.BlockSpec((rows, d), lambda j: (0, 0)),
                  pl.BlockSpec((d, tn), lambda j: (0, j)),
                  pl.BlockSpec((1, tn), lambda j: (0, j))],
        out_specs=pl.BlockSpec((rows, tn), lambda j: (0, j)),
        out_shape=jax.ShapeDtypeStruct((rows, n_out), F32),
        compiler_params=_cparams("arbitrary"),
        name="modulation",
    )(c_rows, w_mod, b_mod.reshape(1, n_out))


def _inproj_kernel(x_ref, sc_ref, sh_ref, g_ref, w_ref, u_ref, m_ref, gt_ref, edge_ref):
    h = _rms(x_ref[...], g_ref[...]) * (1.0 + sc_ref[0]) + sh_ref[0]
    hb = h.astype(BF16)
    u = _dot(hb, w_ref[:, :RWKV_COLS])
    u_ref[...] = u
    edge_ref[0, 0:1, :] = u[0:1]
    edge_ref[0, 1:2, :] = u[u.shape[0] - 1:]
    m_ref[...] = _dot(hb, w_ref[:, RWKV_COLS:RWKV_COLS + M_COLS])
    gt_ref[...] = _dot(hb, w_ref[:, RWKV_COLS + M_COLS:]).astype(gt_ref.dtype)


def _mod_spec(mod, blocks_per_batch):
    d = mod.shape[-1]
    if mod.shape[0] == 1:
        return pl.BlockSpec((1, 1, d), lambda i: (0, 0, 0))
    return pl.BlockSpec((1, 1, d), lambda i: (i // blocks_per_batch, 0, 0))


def _inproj(x2, sc, sh, g, w_in_p, tb, blocks_per_batch):
    n, d = x2.shape
    cols = w_in_p.shape[1]
    gate_cols = cols - RWKV_COLS - M_COLS
    return pl.pallas_call(
        _inproj_kernel,
        grid=(n // tb,),
        in_specs=[pl.BlockSpec((tb, d), lambda i: (i, 0)),
                  _mod_spec(sc, blocks_per_batch), _mod_spec(sh, blocks_per_batch),
                  pl.BlockSpec((1, d), lambda i: (0, 0)),
                  pl.BlockSpec((d, cols), lambda i: (0, 0))],
        out_specs=[pl.BlockSpec((tb, RWKV_COLS), lambda i: (i, 0)),
                   pl.BlockSpec((tb, M_COLS), lambda i: (i, 0)),
                   pl.BlockSpec((tb, gate_cols), lambda i: (i, 0)),
                   pl.BlockSpec((1, 2, RWKV_COLS), lambda i: (i, 0, 0))],
        out_shape=[jax.ShapeDtypeStruct((n, RWKV_COLS), F32),
                   jax.ShapeDtypeStruct((n, M_COLS), F32),
                   jax.ShapeDtypeStruct((n, gate_cols), BF16),
                   jax.ShapeDtypeStruct((n // tb, 2, RWKV_COLS), F32)],
        compiler_params=_cparams("arbitrary"),
        name="inproj",
    )(x2, sc, sh, g, w_in_p)


def _rwkv_prep_kernel(u_ref, hp_ref, hn_ref, sw_ref, w0_ref, a0_ref, wup_ref, aup_ref, gup_ref,
                      kk_ref, ka_ref, rk_ref, bd_ref,
                      r_o, v_o, nkk_o, kdf_o, kdb_o, bf_o, bb_o, lwf_o, lwb_o, g_o, bonus_o):
    u = u_ref[0]
    tb = u.shape[0]
    row = lax.broadcasted_iota(I32, u.shape, 0)
    prev = jnp.where(row == 0, hp_ref[0, 0], pltpu.roll(u, 1, 0))
    nxt = jnp.where(row == tb - 1, hn_ref[0, 0], pltpu.roll(u, tb - 1, 0))
    xs = sw_ref[0:1] * prev + sw_ref[1:2] * u + sw_ref[2:3] * nxt
    aw = A_WIDTH
    r = xs[:, 0:aw]
    k = xs[:, aw:2 * aw]
    v = xs[:, 2 * aw:3 * aw]
    o = 3 * aw
    dw = xs[:, o:o + 2 * DECAY_LORA]
    da = xs[:, o + 2 * DECAY_LORA:o + 2 * DECAY_LORA + 2 * ICLR_LORA]
    dg = xs[:, o + 2 * DECAY_LORA + 2 * ICLR_LORA:]
    bd = bd_ref[...]
    kkr = k * kk_ref[...]
    kk = kkr * lax.rsqrt(_dot((kkr * kkr).astype(BF16), bd) + 1e-12)
    tw = jnp.tanh(dw).astype(BF16)
    dab = da.astype(BF16)
    r_o[0] = r.astype(BF16)
    v_o[0] = v.astype(BF16)
    nkk_o[0] = (-kk).astype(BF16)
    for d, (lw_o, kd_o, b_o) in enumerate(((lwf_o, kdf_o, bf_o), (lwb_o, kdb_o, bb_o))):
        lw_o[0] = -DECAY_SCALE * jax.nn.sigmoid(w0_ref[d:d + 1] + _dot(tw, wup_ref[d]))
        a = jax.nn.sigmoid(a0_ref[d:d + 1] + _dot(dab, aup_ref[d]))
        kd_o[0] = (k * (1.0 + (a - 1.0) * ka_ref[...])).astype(BF16)
        b_o[0] = (kk * a).astype(BF16)
    g_o[0] = _dot(jax.nn.sigmoid(dg).astype(BF16), gup_ref[...]).astype(BF16)
    bonus_o[0] = (_dot_x2(r * k * rk_ref[...], bd) * v).astype(BF16)


def _rwkv_prep(u3, edges, shift_w, w0, a0, wup_p, aup_p, gup, k_k, k_a, r_k, bd, tb):
    b, t, cols = u3.shape
    nb = t // tb
    zero = jnp.zeros((b, 1, cols), F32)
    if nb == 1:
        halo_prev = halo_next = zero.reshape(b, 1, 1, cols)
    else:
        edges = edges.reshape(b, nb, 2, cols)
        halo_prev = jnp.concatenate([zero, edges[:, :nb - 1, 1]], axis=1).reshape(b, nb, 1, cols)
        halo_next = jnp.concatenate([edges[:, 1:, 0], zero], axis=1).reshape(b, nb, 1, cols)
    aw = A_WIDTH
    full2 = lambda s: pl.BlockSpec(s, lambda i, j: (0, 0))
    full3 = lambda s: pl.BlockSpec(s, lambda i, j: (0, 0, 0))
    out_spec = pl.BlockSpec((1, tb, aw), lambda i, j: (i, j, 0))
    sds = lambda dt: jax.ShapeDtypeStruct((b, t, aw), dt)
    out_dtypes = [BF16] * 7 + [F32] * 2 + [BF16] * 2
    return pl.pallas_call(
        _rwkv_prep_kernel,
        grid=(b, nb),
        in_specs=[pl.BlockSpec((1, tb, cols), lambda i, j: (i, j, 0)),
                  pl.BlockSpec((1, 1, 1, cols), lambda i, j: (i, j, 0, 0)),
                  pl.BlockSpec((1, 1, 1, cols), lambda i, j: (i, j, 0, 0)),
                  full2((3, cols)), full2((2, aw)), full2((2, aw)),
                  full3(wup_p.shape), full3(aup_p.shape), full2(gup.shape),
                  full2((1, aw)), full2((1, aw)), full2((1, aw)), full2((aw, aw))],
        out_specs=[out_spec] * 11,
        out_shape=[sds(dt) for dt in out_dtypes],
        compiler_params=_cparams("arbitrary", "arbitrary"),
        name="rwkv_prep",
    )(u3, halo_prev, halo_next, shift_w, w0, a0, wup_p, aup_p, gup, k_k, k_a, r_k, bd)


def _bdot(a, b):
    return lax.dot_general(a, b, (((2,), (1,)), ((0,), (0,))), preferred_element_type=F32)


def _bdot_nt(a, b):
    return lax.dot_general(a, b, (((2,), (2,)), ((0,), (0,))), preferred_element_type=F32)


def _bdot_tn(a, b):
    return lax.dot_general(a, b, (((1,), (1,)), ((0,), (0,))), preferred_element_type=F32)


def _split_groups(x, w):
    return jnp.stack([x[g][:, p * w:(p + 1) * w]
                      for g in range(x.shape[0]) for p in range(x.shape[2] // w)], axis=0)


def _scan_kernel(*refs, reverse, add_to):
    if add_to:
        r_ref, v_ref, nkk_ref, kd_ref, b_ref, lw_ref, s0_ref, other_ref, o_ref, sf_ref, s_scr = refs
    else:
        r_ref, v_ref, nkk_ref, kd_ref, b_ref, lw_ref, s0_ref, o_ref, sf_ref, s_scr = refs
    c = pl.program_id(1)
    grp, ch, aw = lw_ref.shape
    w = s_scr.shape[-1]
    hp = w // A_HEAD_DIM
    n_groups = aw // w
    state_shape = s_scr.shape

    @pl.when(c == 0)
    def _():
        s_scr[...] = s0_ref[...].reshape(state_shape)

    lw = lw_ref[...]
    ri = lax.broadcasted_iota(I32, (ch, ch), 0)
    ci = lax.broadcasted_iota(I32, (ch, ch), 1)
    tri = jnp.where((ci >= ri) if reverse else (ci <= ri), 1.0, 0.0).astype(BF16)
    hi = lw.astype(BF16)
    rem = lw - hi.astype(F32)
    mid = rem.astype(BF16)
    lo = (rem - mid.astype(F32)).astype(BF16)
    li = jnp.stack([_dot(tri, hi[g]) + (_dot(tri, mid[g]) + _dot(tri, lo[g])) for g in range(grp)], axis=0)
    lt = li[:, 0:1] if reverse else li[:, ch - 1:ch]
    rho = 0.5 * lt
    e1 = jnp.exp(li - rho)
    e2 = jnp.exp(rho - li)
    er = jnp.exp(rho)
    a_rel = nkk_ref[...] * (e1 * jnp.exp(-lw))
    r_rel = r_ref[...] * e1
    b_rel = b_ref[...] * e2
    k_rel = kd_ref[...] * e2
    pairs = lambda x: _split_groups(x.astype(BF16), w)
    a_abs, r_abs, b_end, k_end = pairs(a_rel * er), pairs(r_rel * er), pairs(b_rel * er), pairs(k_rel * er)
    a_rel, r_rel, b_rel, k_rel = pairs(a_rel), pairs(r_rel), pairs(b_rel), pairs(k_rel)
    v = pairs(v_ref[...])
    decay = _split_groups(jnp.exp(lt), w)
    head_of_lane = lax.broadcasted_iota(I32, (1, 1, w), 2) // A_HEAD_DIM

    def blockdiag(x):
        zero = jnp.zeros_like(x)
        return jnp.concatenate([jnp.where(head_of_lane == j, x, zero) for j in range(hp)], axis=1)

    ri2 = lax.broadcasted_iota(I32, (1, ch, hp * ch), 1)
    ci2 = lax.broadcasted_iota(I32, (1, ch, hp * ch), 2) & (ch - 1)
    if reverse:
        incl2, strict2 = ci2 >= ri2, ci2 > ri2
    else:
        incl2, strict2 = ci2 <= ri2, ci2 < ri2
    eye2 = jnp.where(ri2 == ci2, 1.0, 0.0)
    lhs = jnp.concatenate([a_rel, r_rel], axis=1)
    with_b = _bdot_nt(lhs, blockdiag(b_rel))
    with_k = _bdot_nt(lhs, blockdiag(k_rel))
    a_ab = jnp.where(strict2, with_b[:, :ch], 0.0)
    a_ak = jnp.where(strict2, with_k[:, :ch], 0.0).astype(BF16)
    a_r = jnp.concatenate([jnp.where(incl2, with_b[:, ch:], 0.0), jnp.where(incl2, with_k[:, ch:], 0.0)],
                          axis=2).astype(BF16)
    inv = eye2
    for k in range(int(math.log2(ch))):
        p, q = (ci2 >> k, ri2 >> k) if reverse else (ri2 >> k, ci2 >> k)
        joins = ((p ^ q) * 4 + (p - q)) == 5
        lk = jnp.where(joins, a_ab, 0.0)
        if k == 0:
            inv = inv + lk
        else:
            invb = inv.astype(BF16)
            inv = inv + _bdot(_bdot(invb, blockdiag(lk.astype(BF16))).astype(BF16), blockdiag(invb))
    s = s_scr[...]
    from_state = _bdot_nt(jnp.concatenate([a_abs, r_abs], axis=1), s.astype(BF16))
    v_bd = blockdiag(v)
    x = from_state[:, :ch] + _bdot(a_ak, v_bd)
    u = _bdot(inv.astype(BF16), blockdiag(x.astype(BF16))).astype(BF16)
    o = from_state[:, ch:] + _bdot(a_r, jnp.concatenate([blockdiag(u), v_bd], axis=1))
    same_head = (lax.broadcasted_iota(I32, (1, w, w), 1) // A_HEAD_DIM) == head_of_lane
    upd = _bdot_tn(jnp.concatenate([u, v], axis=1), jnp.concatenate([b_end, k_end], axis=1))
    s_scr[...] = s * decay + jnp.where(same_head, upd, 0.0)
    for g in range(grp):
        o_g = jnp.concatenate([o[g * n_groups + p] for p in range(n_groups)], axis=1)
        o_ref[g] = o_g + other_ref[g] if add_to else o_g

    @pl.when(c == pl.num_programs(1) - 1)
    def _():
        s_fin = s_scr[...]
        for g in range(grp):
            for p in range(n_groups):
                group = s_fin[g * n_groups + p]
                for j in range(hp):
                    part = slice(j * A_HEAD_DIM, (j + 1) * A_HEAD_DIM)
                    sf_ref[g, hp * p + j] = group[part, part]


def _wkv_scan(r, v, nkk, kd, b, lw, s0, reverse, add_to=None):
    bsz, t, aw = r.shape
    ch = SCAN_CHUNK
    assert t % ch == 0 and ch == A_HEAD_DIM
    nc = t // ch
    w = SCAN_PACK * A_HEAD_DIM
    n_groups = aw // w
    grp = SCAN_GROUP if bsz % SCAN_GROUP == 0 else 1
    tmap = (lambda i, c: (i, nc - 1 - c, 0)) if reverse else (lambda i, c: (i, c, 0))
    seq = pl.BlockSpec((grp, ch, aw), tmap)
    st_in = pl.BlockSpec((grp, n_groups, w, w), lambda i, c: (i, 0, 0, 0))
    st_out = pl.BlockSpec((grp,) + s0.shape[1:], lambda i, c: (i, 0, 0, 0))
    extra = [] if add_to is None else [add_to]
    return pl.pallas_call(
        functools.partial(_scan_kernel, reverse=reverse, add_to=add_to is not None),
        grid=(bsz // grp, nc),
        in_specs=[seq] * 6 + [st_in] + [seq] * len(extra),
        out_specs=[seq, st_out],
        out_shape=[jax.ShapeDtypeStruct((bsz, t, aw), F32), jax.ShapeDtypeStruct(s0.shape, F32)],
        scratch_shapes=[pltpu.VMEM((grp * n_groups, w, w), F32)],
        compiler_params=_cparams("arbitrary", "arbitrary"),
        name="wkv_scan_bwd" if reverse else "wkv_scan_fwd",
    )(r, v, nkk, kd, b, lw, _group_states(s0, SCAN_PACK), *extra)


def _group_states(s, hp):
    b, h, d, _ = s.shape
    eye = jnp.eye(hp, dtype=s.dtype)[None, None, :, None, :, None]
    return (s.reshape(b, h // hp, hp, d, 1, d) * eye).reshape(b, h // hp, hp * d, hp * d)


def _denominator_lane(width):
    lane = lax.broadcasted_iota(I32, (1, width), 1)
    return jnp.where((lane & (HEAD_PAD - 1)) == V_HEAD, 1.0, 0.0)


def _rope_lanes(x, cos_t, sin_up, sin_dn):
    w = x.shape[1]
    half = QK_ROPE // 2
    return x * cos_t + pltpu.roll(x, half, 1) * sin_up + pltpu.roll(x, w - half, 1) * sin_dn


def _mla_prep_kernel(*refs, rope):
    if rope:
        (m_ref, gq_ref, gkv_ref, wuq_ref, wk_ref, wv_ref, p_ref,
         qc_ref, qu_ref, qd_ref, kc_ref, ku_ref, kd_ref, q_o, k_o, v_o, ckv_o) = refs
    else:
        (m_ref, gq_ref, gkv_ref, wuq_ref, wk_ref, wv_ref, p_ref, q_o, k_o, v_o, ckv_o) = refs
    m = m_ref[...]
    qn = _rms(m[:, :Q_LORA], gq_ref[...])
    ckv = _rms(m[:, Q_LORA:Q_LORA + KV_LORA], gkv_ref[...])
    kp = m[:, Q_LORA + KV_LORA:]
    q = _dot(qn.astype(BF16), wuq_ref[...]) * ATTN_SCALE
    if rope:
        tile = lambda ref: jnp.concatenate([ref[...]] * B_HEADS, axis=1)
        q = _rope_lanes(q, tile(qc_ref), tile(qu_ref), tile(qd_ref))
        kp = _rope_lanes(kp, kc_ref[...], ku_ref[...], kd_ref[...])
    cb = ckv.astype(BF16)
    q_o[...] = q.astype(BF16)
    k_o[...] = (_dot(cb, wk_ref[...]) + _dot(kp.astype(BF16), p_ref[...])).astype(BF16)
    v_o[...] = (_dot(cb, wv_ref[...]) + _denominator_lane(v_o.shape[1])).astype(BF16)
    ckv_o[...] = ckv


def _mla_prep(m2, g_q, g_kv, wuq_p, wk_p, wv_p, place, rope_tabs, tb, t):
    n = m2.shape[0]
    hw = B_HEADS * HEAD_PAD
    full = lambda a: pl.BlockSpec(a.shape, lambda i: (0, 0))
    ins = [m2, g_q, g_kv, wuq_p, wk_p, wv_p, place]
    specs = [pl.BlockSpec((tb, M_COLS), lambda i: (i, 0))] + [full(a) for a in ins[1:]]
    if rope_tabs is not None:
        nbt = t // tb
        ins += list(rope_tabs)
        specs += [pl.BlockSpec((tb, HEAD_PAD), lambda i: (i % nbt, 0))] * 6
    big = pl.BlockSpec((tb, hw), lambda i: (i, 0))
    return pl.pallas_call(
        functools.partial(_mla_prep_kernel, rope=rope_tabs is not None),
        grid=(n // tb,),
        in_specs=specs,
        out_specs=[big, big, big, pl.BlockSpec((tb, KV_LORA), lambda i: (i, 0))],
        out_shape=[jax.ShapeDtypeStruct((n, hw), BF16)] * 3 + [jax.ShapeDtypeStruct((n, KV_LORA), F32)],
        compiler_params=_cparams("arbitrary"),
        name="mla_prep",
    )(*ins)


def _kv_up_kernel(ckv_ref, kp_ref, wk_ref, wv_ref, p_ref, k_o, v_o):
    cb = ckv_ref[...].astype(BF16)
    k_o[...] = (_dot(cb, wk_ref[...]) + _dot(kp_ref[...].astype(BF16), p_ref[...])).astype(BF16)
    v_o[...] = (_dot(cb, wv_ref[...]) + _denominator_lane(v_o.shape[1])).astype(BF16)


def _kv_up(ckv2, kpe_pad, wk_p, wv_p, place, tb):
    n = ckv2.shape[0]
    hw = B_HEADS * HEAD_PAD
    full = lambda a: pl.BlockSpec(a.shape, lambda i: (0, 0))
    big = pl.BlockSpec((tb, hw), lambda i: (i, 0))
    return pl.pallas_call(
        _kv_up_kernel,
        grid=(n // tb,),
        in_specs=[pl.BlockSpec((tb, KV_LORA), lambda i: (i, 0)),
                  pl.BlockSpec((tb, HEAD_PAD), lambda i: (i, 0)),
                  full(wk_p), full(wv_p), full(place)],
        out_specs=[big, big],
        out_shape=[jax.ShapeDtypeStruct((n, hw), BF16)] * 2,
        compiler_params=_cparams("arbitrary"),
        name="kv_up",
    )(ckv2, kpe_pad, wk_p, wv_p, place)


def _attn_kernel(*refs, two):
    if two:
        q_ref, k1_ref, v1_ref, k2_ref, v2_ref, o_ref = refs
    else:
        q_ref, k1_ref, v1_ref, o_ref = refs
        heads = [slice(h * HEAD_PAD, (h + 1) * HEAD_PAD) for h in range(B_HEADS)]
        scores = [_dot_nt(q_ref[0, :, hs], k1_ref[0, :, hs]) for hs in heads]
        probs = [jnp.exp(s - jnp.max(s, axis=-1, keepdims=True)).astype(BF16) for s in scores]
        accs = [_dot(p, v1_ref[0, :, hs]) for p, hs in zip(probs, heads)]
        _store_heads(o_ref, [acc / acc[:, V_HEAD:V_HEAD + 1] for acc in accs])
        return
    outs = []
    for h in range(B_HEADS):
        hs = slice(h * HEAD_PAD, (h + 1) * HEAD_PAD)
        q = q_ref[0, :, hs]
        s1 = _dot_nt(q, k1_ref[0, :, hs])
        mx = jnp.max(s1, axis=-1, keepdims=True)
        if two:
            s2 = _dot_nt(q, k2_ref[0, :, hs])
            mx = jnp.maximum(mx, jnp.max(s2, axis=-1, keepdims=True))
        acc = _dot(jnp.exp(s1 - mx).astype(BF16), v1_ref[0, :, hs])
        if two:
            acc = acc + _dot(jnp.exp(s2 - mx).astype(BF16), v2_ref[0, :, hs])
        outs.append(acc / acc[:, V_HEAD:V_HEAD + 1])
        if h % 2 == 1:
            _store_heads(o_ref, outs, first_head=h - 1)
            outs = []


def _store_heads(o_ref, outs, first_head=0):
    for i in range(0, len(outs), 2):
        lanes = slice((first_head + i) * V_HEAD, (first_head + i + 2) * V_HEAD)
        o_ref[0, :, lanes] = jnp.concatenate([outs[i][:, :V_HEAD], outs[i + 1][:, :V_HEAD]], axis=1).astype(BF16)


def _attention(q3, k1, v1, k2, v2, tq):
    b, t, hw = q3.shape
    two = k2 is not None
    qspec = pl.BlockSpec((1, tq, hw), lambda i, j: (i, j, 0))
    kv = lambda a: pl.BlockSpec((1, a.shape[1], hw), lambda i, j: (i, 0, 0))
    ins = [q3, k1, v1] + ([k2, v2] if two else [])
    return pl.pallas_call(
        functools.partial(_attn_kernel, two=two),
        grid=(b, t // tq),
        in_specs=[qspec] + [kv(a) for a in ins[1:]],
        out_specs=pl.BlockSpec((1, tq, B_HEADS * V_HEAD), lambda i, j: (i, j, 0)),
        out_shape=jax.ShapeDtypeStruct((b, t, B_HEADS * V_HEAD), BF16),
        compiler_params=_cparams("arbitrary", "arbitrary"),
        name="attention",
    )(*ins)


def _post_mix_kernel(o_ref, bonus_ref, g_ref, att_ref, gates_ref, x_ref,
                     g1_ref, sc2_ref, sh2_ref, lnw_ref, lnb_ref, bd_ref, wa_ref, wb_ref, wo_ref,
                     gpost_ref, gpre_ref, wr_ref, x1_o, h2_o, aff_o):
    bd = bd_ref[...]
    inv_n = 1.0 / A_HEAD_DIM
    o = o_ref[...]
    mu = _dot_x2(o, bd) * inv_n
    oc = o - mu
    var = _dot((oc * oc).astype(BF16), bd) * inv_n
    on = (oc * lax.rsqrt(var + GN_EPS)) * lnw_ref[...] + lnb_ref[...]
    ya = _dot(((on + bonus_ref[...]) * g_ref[...]).astype(BF16), wa_ref[...])
    yb = _dot(att_ref[...], wb_ref[...])
    d = ya.shape[1]
    gates = gates_ref[...].astype(F32)
    mix = jax.nn.sigmoid(gates[:, :d]) * ya + jax.nn.sigmoid(gates[:, d:]) * yb
    y = _dot(mix.astype(BF16), wo_ref[...])
    x1 = x_ref[...] + g1_ref[0] * _rms(y, gpost_ref[...])
    x1_o[...] = x1
    h2 = _rms(x1, gpre_ref[...]) * (1.0 + sc2_ref[0]) + sh2_ref[0]
    h2_o[...] = h2.astype(BF16)
    hh, hl = _split2(h2)
    wh, wl = _split2(wr_ref[...])
    logits = _dot_nt(wh, hh) + (_dot_nt(wh, hl) + _dot_nt(wl, hh))
    z = jnp.exp(logits - jnp.max(logits, axis=0, keepdims=True))
    aff_o[...] = z / jnp.sum(z, axis=0, keepdims=True)


def _post_mix(o, bonus, g, att, gates, x2, g1, sc2, sh2, ln_w, ln_b, bd, wa, wb, wo,
              g_post, g_pre, w_router_t, tb, blocks_per_batch):
    n, d = x2.shape
    tok = lambda a: pl.BlockSpec((tb, a.shape[1]), lambda i: (i, 0))
    full = lambda a: pl.BlockSpec(a.shape, lambda i: (0, 0))
    ms = lambda a: _mod_spec(a, blocks_per_batch)
    return pl.pallas_call(
        _post_mix_kernel,
        grid=(n // tb,),
        in_specs=[tok(o), tok(bonus), tok(g), tok(att), tok(gates), tok(x2),
                  ms(g1), ms(sc2), ms(sh2), full(ln_w), full(ln_b), full(bd), full(wa), full(wb),
                  full(wo), full(g_post), full(g_pre), full(w_router_t)],
        out_specs=[pl.BlockSpec((tb, d), lambda i: (i, 0)),
                   pl.BlockSpec((tb, d), lambda i: (i, 0)),
                   pl.BlockSpec((N_EXPERTS, tb), lambda i: (0, i))],
        out_shape=[jax.ShapeDtypeStruct((n, d), F32),
                   jax.ShapeDtypeStruct((n, d), BF16),
                   jax.ShapeDtypeStruct((N_EXPERTS, n), F32)],
        compiler_params=_cparams("arbitrary"),
        name="post_mix",
    )(o, bonus, g, att, gates, x2, g1, sc2, sh2, ln_w, ln_b, bd, wa, wb, wo,
      g_post, g_pre, w_router_t)


def _select_kernel(a_ref, pos_ref, sel_ref, *, cap, n_chunks):
    a = a_ref[...]
    rows = a.shape[0]
    a3 = a.reshape(N_EXPERTS, n_chunks, LANES)

    def count(mask):
        c = jnp.sum(jnp.where(mask, 1.0, 0.0), axis=2, keepdims=True)
        return jnp.sum(c, axis=1, keepdims=True)

    def body(i, thr):
        cand = thr | jnp.left_shift(jnp.int32(1), 30 - i)
        return jnp.where(count(a3 >= pltpu.bitcast(cand, F32)) >= cap, cand, thr)

    thr = pltpu.bitcast(lax.fori_loop(0, 31, body, jnp.zeros((N_EXPERTS, 1, LANES), I32)), F32)
    gt = a3 > thr
    eq = a3 == thr
    need = cap - count(gt)

    li = lax.broadcasted_iota(I32, (LANES, LANES), 0)
    lj = lax.broadcasted_iota(I32, (LANES, LANES), 1)
    lane_before = jnp.where(li < lj, 1.0, 0.0).astype(BF16)
    lane_all = jnp.ones((LANES, LANES), BF16)
    ci = lax.broadcasted_iota(I32, (n_chunks, n_chunks), 0)
    cj = lax.broadcasted_iota(I32, (n_chunks, n_chunks), 1)
    chunk_before = jnp.where(cj < ci, 1.0, 0.0).astype(BF16)

    def prefix(flags3):
        f2 = flags3.reshape(rows, LANES).astype(BF16)
        within = _dot(f2, lane_before).reshape(N_EXPERTS, n_chunks, LANES)
        tot = _dot(f2, lane_all).astype(BF16).reshape(N_EXPERTS, n_chunks, LANES)
        offs = [_dot(chunk_before, tot[e]) for e in range(N_EXPERTS)]
        return within + jnp.stack(offs, axis=0)

    eqf = jnp.where(eq, 1.0, 0.0)
    tie_ok = jnp.where(prefix(eqf) < need, eqf, 0.0)
    sel = jnp.where(gt, 1.0, tie_ok)
    pos_ref[...] = prefix(sel).reshape(rows, LANES).astype(I32)
    sel_ref[...] = sel.reshape(rows, LANES).astype(I32)


def _select(aff_t, cap):
    e, n = aff_t.shape
    n_chunks = n // LANES
    rows = e * n_chunks
    a2 = aff_t.reshape(rows, LANES)
    spec = pl.BlockSpec((rows, LANES), lambda i: (0, 0))
    pos, sel = pl.pallas_call(
        functools.partial(_select_kernel, cap=cap, n_chunks=n_chunks),
        grid=(1,),
        in_specs=[spec],
        out_specs=[spec, spec],
        out_shape=[jax.ShapeDtypeStruct((rows, LANES), I32)] * 2,
        compiler_params=_cparams("arbitrary"),
        name="expert_select",
    )(a2)
    return pos.reshape(e, n), sel.reshape(e, n)


def _ffn_kernel(lo_ref, hi_ref, slot_ref, aff_ref, h_ref, wg_ref, wu_ref, wd_ref, y_ref, acc_ref, wacc_ref,
                *, n_tiles, sub):
    e = pl.program_id(0)
    j = pl.program_id(1)
    rows = acc_ref.shape[0]
    n_blocks, tb = slot_ref.shape[1], slot_ref.shape[2]
    n_sub = rows // sub
    acc_ref[...] = jnp.zeros_like(acc_ref)
    wacc_ref[...] = jnp.zeros_like(wacc_ref)
    first = [lo_ref[(e * n_tiles + j) * n_sub + si] for si in range(n_sub)]
    last = [hi_ref[(e * n_tiles + j) * n_sub + si] for si in range(n_sub)]
    trips = functools.reduce(jnp.maximum, [hi - lo for lo, hi in zip(first, last)])
    row0 = lax.broadcasted_iota(I32, (sub, tb), 0) + j * rows

    def gather_step(i, carry):
        for si in range(n_sub):
            b = first[si] + i
            live = b < last[si]
            b = jnp.minimum(b, n_blocks - 1)
            part = slice(si * sub, (si + 1) * sub)
            hit = (row0 + si * sub) == jnp.where(live, slot_ref[0, pl.ds(b, 1), :], -1)
            tokens = h_ref[pl.ds(pl.multiple_of(b * tb, tb), tb), :]
            acc_ref[part, :] += _dot(jnp.where(hit, 1.0, 0.0).astype(BF16), tokens)
            wacc_ref[part, :] += jnp.sum(jnp.where(hit, aff_ref[0, pl.ds(b, 1), :], 0.0), axis=1, keepdims=True)
        return carry

    lax.fori_loop(0, trips, gather_step, 0)
    xe = acc_ref[...].astype(BF16)
    gate = _dot(xe, wg_ref[0])
    up = _dot(xe, wu_ref[0])
    hid = (gate * jax.nn.sigmoid(gate)) * up
    y_ref[0] = (_dot(hid.astype(BF16), wd_ref[0]) * wacc_ref[...]).astype(BF16)


def _expert_ffn(pos, slot, aff_t, h2, wg, wu, wd, cap, tb, rt):
    n, d = h2.shape
    nb = n // tb
    e, _, f = wg.shape
    n_tiles = cap // rt
    sub = min(GATHER_ROWS, rt)
    starts = jnp.concatenate([pos[:, ::tb], jnp.full((e, 1), cap, I32)], axis=1)
    edges = jnp.arange(cap // sub, dtype=I32) * sub
    lo = jnp.sum((starts[:, None, 1:] <= edges[None, :, None]).astype(I32), axis=2)
    hi = jnp.sum((starts[:, None, :-1] < (edges + sub)[None, :, None]).astype(I32), axis=2)
    grid_spec = pltpu.PrefetchScalarGridSpec(
        num_scalar_prefetch=2,
        grid=(e, n_tiles),
        in_specs=[pl.BlockSpec((1, nb, tb), lambda ei, j, lo, hi: (ei, 0, 0)),
                  pl.BlockSpec((1, nb, tb), lambda ei, j, lo, hi: (ei, 0, 0)),
                  pl.BlockSpec((n, d), lambda ei, j, lo, hi: (0, 0), pipeline_mode=pl.Buffered(1)),
                  pl.BlockSpec((1, d, f), lambda ei, j, lo, hi: (ei, 0, 0)),
                  pl.BlockSpec((1, d, f), lambda ei, j, lo, hi: (ei, 0, 0)),
                  pl.BlockSpec((1, f, d), lambda ei, j, lo, hi: (ei, 0, 0))],
        out_specs=pl.BlockSpec((1, rt, d), lambda ei, j, lo, hi: (ei, j, 0)),
        scratch_shapes=[pltpu.VMEM((rt, d), F32), pltpu.VMEM((rt, 1), F32)],
    )
    return pl.pallas_call(
        functools.partial(_ffn_kernel, n_tiles=n_tiles, sub=sub),
        grid_spec=grid_spec,
        out_shape=jax.ShapeDtypeStruct((e, cap, d), BF16),
        compiler_params=_cparams("arbitrary", "arbitrary"),
        name="expert_ffn",
    )(lo.reshape(-1), hi.reshape(-1), slot.reshape(e, nb, tb), aff_t.reshape(e, nb, tb), h2, wg, wu, wd)


def _combine_kernel(t0_ref, fetch_ref, need_ref, slot_ref, x_ref, g2_ref, gp_ref, *rest, n_win):
    y_refs, o_ref, acc_ref = rest[:-2], rest[-2], rest[-1]
    blk = pl.program_id(0)
    n_blk = pl.num_programs(0)
    tb = slot_ref.shape[0]
    rows = y_refs[0].shape[1]
    lane = lax.broadcasted_iota(I32, (tb, rows), 1)

    def picked(e, k):
        slot = slot_ref[:, e:e + 1] - (t0_ref[blk * N_EXPERTS + e] + k) * rows
        return _dot(jnp.where(lane == slot, 1.0, 0.0).astype(BF16), y_refs[e * n_win + k][0])

    acc = picked(0, 0)
    for e in range(1, N_EXPERTS):
        acc = acc + picked(e, 0)
    acc_ref[...] = acc
    for k in range(1, n_win):
        for e in range(N_EXPERTS):
            @pl.when(need_ref[(k * n_blk + blk) * N_EXPERTS + e] != 0)
            def _():
                acc_ref[...] += picked(e, k)
    o_ref[...] = x_ref[...] + g2_ref[0] * _rms(acc_ref[...], gp_ref[...])


def _combine_windows(pos, tb, rt, cap):
    n_tiles = cap // rt
    lo = pos[:, ::tb]
    hi = jnp.concatenate([lo[:, 1:], jnp.full((lo.shape[0], 1), cap, I32)], axis=1)
    t0 = jnp.minimum(lo // rt, n_tiles - 1)
    t_last = jnp.where(hi > lo, (hi - 1) // rt, t0)
    fetch, need = [t0], [jnp.ones_like(t0)]
    for k in range(1, tb // rt + 1):
        need.append(t_last >= t0 + k)
        fetch.append(lax.cummax(jnp.where(need[k], t0 + k, 0), axis=1))
    flat = lambda a: a.T.reshape(-1).astype(I32)
    return flat(t0), jnp.concatenate([flat(f) for f in fetch]), jnp.concatenate([flat(f) for f in need])


def _combine(windows, slot_t, ye, x1, g2, g_post, tb, rt, blocks_per_batch):
    t0, fetch, need = windows
    n, d = x1.shape
    e, cap, _ = ye.shape
    nb = n // tb
    n_win = tb // rt + 1
    if g2.shape[0] == 1:
        g2_spec = pl.BlockSpec((1, 1, d), lambda b, t, f, nd: (0, 0, 0))
    else:
        g2_spec = pl.BlockSpec((1, 1, d), lambda b, t, f, nd: (b // blocks_per_batch, 0, 0))

    def window(ei, k):
        mode = {} if k == 0 else {"pipeline_mode": pl.Buffered(1)}
        return pl.BlockSpec((1, rt, d), lambda b, t, f, nd: (ei, f[(k * nb + b) * e + ei], 0), **mode)

    grid_spec = pltpu.PrefetchScalarGridSpec(
        num_scalar_prefetch=3,
        grid=(nb,),
        in_specs=[pl.BlockSpec((tb, e), lambda b, t, f, nd: (b, 0)),
                  pl.BlockSpec((tb, d), lambda b, t, f, nd: (b, 0)),
                  g2_spec,
                  pl.BlockSpec((1, d), lambda b, t, f, nd: (0, 0))]
                 + [window(ei, k) for ei in range(e) for k in range(n_win)],
        out_specs=pl.BlockSpec((tb, d), lambda b, t, f, nd: (b, 0)),
        scratch_shapes=[pltpu.VMEM((tb, d), F32)],
    )
    return pl.pallas_call(
        functools.partial(_combine_kernel, n_win=n_win),
        grid_spec=grid_spec,
        out_shape=jax.ShapeDtypeStruct((n, d), F32),
        compiler_params=_cparams("arbitrary"),
        name="moe_combine",
    )(t0, fetch, need, slot_t, x1, g2, g_post, *([ye] * (e * n_win)))


def _rope_tables(t):
    half = QK_ROPE // 2
    pos = jnp.arange(t)
    row = (pos // GRID_W).astype(F32)
    col = (pos % GRID_W).astype(F32)
    inv = ROPE_BASE ** (-jnp.arange(0, half, 2, dtype=F32) / half)
    ang = jnp.concatenate([row[:, None] * inv, col[:, None] * inv], axis=-1)
    cos, sin = jnp.cos(ang), jnp.sin(ang)

    def tabs(first_lane):
        left, right = first_lane, HEAD_PAD - first_lane - 2 * half
        fill = lambda value, width: jnp.full((t, width), value, F32)
        c = jnp.concatenate([fill(1.0, left), cos, cos, fill(1.0, right)], axis=1)
        up = jnp.concatenate([fill(0.0, left + half), sin, fill(0.0, right)], axis=1)
        dn = jnp.concatenate([fill(0.0, left), -sin, fill(0.0, half + right)], axis=1)
        return c, up, dn

    return tabs(QK_NOPE + QK_ROPE) + tabs(QK_ROPE)


def _layout_weights(w_in, w_uq, w_ukv, w_branch_b, w_up, a_up):
    d = w_in.shape[0]
    deint = jnp.concatenate([jnp.arange(0, QK_ROPE, 2), jnp.arange(1, QK_ROPE, 2)])
    kpe0 = RWKV_COLS + Q_LORA + KV_LORA
    kpe_cols = w_in[:, kpe0:kpe0 + QK_ROPE]
    w_in_p = jnp.concatenate(
        [w_in[:, :kpe0 + QK_ROPE], kpe_cols[:, deint],
         jnp.zeros((d, M_COLS - MLA_COLS - QK_ROPE), F32), w_in[:, RWKV_COLS + MLA_COLS:]], axis=1)
    uq = w_uq.reshape(Q_LORA, B_HEADS, QK_NOPE + QK_ROPE)
    wuq_p = jnp.concatenate([uq, uq[:, :, QK_NOPE:][:, :, deint]], axis=2).reshape(Q_LORA, B_HEADS * HEAD_PAD)
    ukv = w_ukv.reshape(KV_LORA, B_HEADS, QK_NOPE + V_HEAD)
    zpad = jnp.zeros((KV_LORA, B_HEADS, HEAD_PAD - QK_NOPE), F32)
    wk_p = jnp.concatenate([ukv[:, :, :QK_NOPE], zpad], axis=2).reshape(KV_LORA, B_HEADS * HEAD_PAD)
    wv_p = jnp.concatenate([ukv[:, :, QK_NOPE:], zpad], axis=2).reshape(KV_LORA, B_HEADS * HEAD_PAD)
    wb_p = w_branch_b
    eye = jnp.eye(QK_ROPE, dtype=F32)
    z = jnp.zeros((QK_ROPE, QK_ROPE), F32)
    head_raw = jnp.concatenate([jnp.zeros((QK_ROPE, QK_NOPE), F32), eye, z], axis=1)
    head_rot = jnp.concatenate([jnp.zeros((QK_ROPE, QK_NOPE), F32), z, eye], axis=1)
    zrows = jnp.zeros((HEAD_PAD - 2 * QK_ROPE, B_HEADS * HEAD_PAD), F32)
    zr = jnp.zeros((QK_ROPE, B_HEADS * HEAD_PAD), F32)
    place_raw = jnp.concatenate([jnp.tile(head_raw, (1, B_HEADS)), zr, zrows], axis=0)
    place_rot = jnp.concatenate([zr, jnp.tile(head_rot, (1, B_HEADS)), zrows], axis=0)
    zl = jnp.zeros((DECAY_LORA, A_WIDTH), F32)
    wup_p = jnp.stack([jnp.concatenate([w_up[0], zl]), jnp.concatenate([zl, w_up[1]])])
    aup_p = jnp.stack([jnp.concatenate([a_up[0], zl]), jnp.concatenate([zl, a_up[1]])])
    bf = lambda a: a.astype(BF16)
    return (bf(w_in_p), bf(wuq_p), bf(wk_p), bf(wv_p), bf(wb_p), bf(place_raw), bf(place_rot),
            bf(wup_p), bf(aup_p))


def _group_layer(x, mods, lw, s0_f, s0_b, ctx_kv, rope_tabs):
    bsz, t, d = x.shape
    n = bsz * t
    tw = min(ROW_BLOCK, t)
    bpw = t // tw
    shared_mod = mods[0].shape[0] == 1
    tf = min(ROW_BLOCK, n) if shared_mod and rope_tabs is None else tw
    tb = min(TOKEN_BLOCK, n if shared_mod else t)
    bpb = max(t // tb, 1)
    sh1, sc1, g1, sh2, sc2, g2 = mods
    x2 = x.reshape(n, d)
    u, m, gates, edges = _inproj(x2, sc1, sh1, lw["g_pre_mix"], lw["w_in_p"], tf, bpw)
    prep = _rwkv_prep(u.reshape(bsz, t, RWKV_COLS), edges, lw["shift_w"], lw["w0"], lw["a0"], lw["wup_p"],
                      lw["aup_p"], lw["g_up"], lw["k_k"], lw["k_a"], lw["r_k"], lw["bd"], tw)
    r, v, nkk, kd_f, kd_b, b_f, b_b, lw_f, lw_b, g, bonus = prep
    o_f, s_f = _wkv_scan(r, v, nkk, kd_f, b_f, lw_f, s0_f, reverse=False)
    o_sum, s_b = _wkv_scan(r, v, nkk, kd_b, b_b, lw_b, s0_b, reverse=True, add_to=o_f)
    place = lw["place_raw"] if rope_tabs is None else lw["place_rot"]
    q, k, vv, ckv = _mla_prep(m, lw["g_qnorm"], lw["g_kvnorm"], lw["wuq_p"], lw["wk_p"], lw["wv_p"],
                              place, rope_tabs, tf, t)
    hw = B_HEADS * HEAD_PAD
    k2, v2 = ctx_kv if ctx_kv is not None else (None, None)
    att = _attention(q.reshape(bsz, t, hw), k.reshape(bsz, t, hw), vv.reshape(bsz, t, hw), k2, v2, tw)
    two = lambda a: a.reshape(n, a.shape[-1])
    x1, h2, aff_t = _post_mix(two(o_sum), two(bonus), two(g), two(att), gates, x2,
                              g1, sc2, sh2, lw["ln_x_w"], lw["ln_x_b"], lw["bd"], lw["w_branch_a"],
                              lw["wb_p"], lw["w_out"], lw["g_post_mix"], lw["g_pre_ffn"],
                              lw["w_router_t"], tf, bpw)
    cap = CAPACITY_FACTOR * n // N_EXPERTS
    rt = min(EXPERT_TILE, cap)
    pos, sel = _select(aff_t, cap)
    slot = jnp.where(sel > 0, pos, -1)
    ye = _expert_ffn(pos, slot, aff_t, h2, lw["w_exp_gate"], lw["w_exp_up"], lw["w_exp_down"], cap,
                     min(GATHER_BLOCK, n), min(FFN_TILE, cap))
    out = _combine(_combine_windows(pos, tb, rt, cap), slot.T, ye, x1, g2, lw["g_post_ffn"], tb, rt, bpb)
    kpe = m[:, Q_LORA + KV_LORA:Q_LORA + KV_LORA + QK_ROPE]
    return out.reshape(bsz, t, d), (ckv.reshape(bsz, t, KV_LORA), kpe.reshape(bsz, t, QK_ROPE), s_f, s_b)


def kernel(x_prompt, x_sample, cache_ckv, cache_kpe, state_wkv_fwd, state_wkv_bwd, c, c_ctx,
           w_mod, b_mod, g_pre_mix, g_post_mix, g_pre_ffn, g_post_ffn, w_in, shift_w,
           w0, w_up, a0, a_up, g_up, k_k, k_a, r_k, ln_x_w, ln_x_b, w_branch_a,
           g_qnorm, w_uq, g_kvnorm, w_ukv, w_branch_b, w_out,
           w_router, w_exp_gate, w_exp_up, w_exp_down):
    depth = w_mod.shape[0]
    d = x_prompt.shape[-1]
    dec_b, dec_t = x_sample.shape[0], x_sample.shape[1]
    xp, xs = x_prompt, x_sample
    c_rows = jnp.concatenate([c, c_ctx[None, :],
                              jnp.zeros((-(dec_b + 1) % 8, d), F32)], axis=0)
    rope_tabs = _rope_tables(dec_t)
    ii = lax.broadcasted_iota(I32, (A_WIDTH, A_WIDTH), 0) // A_HEAD_DIM
    jj = lax.broadcasted_iota(I32, (A_WIDTH, A_WIDTH), 1) // A_HEAD_DIM
    bd = (ii == jj).astype(BF16)
    row = lambda a: a.reshape(1, -1)
    bf = lambda a: a.astype(BF16)
    ckv_l, kpe_l, sf_l, sb_l = [], [], [], []
    for l in range(depth):
        (w_in_p, wuq_p, wk_p, wv_p, wb_p, place_raw, place_rot, wup_p, aup_p) = _layout_weights(
            w_in[l], w_uq[l], w_ukv[l], w_branch_b[l], w_up[l], a_up[l])
        lw = {
            "g_pre_mix": row(g_pre_mix[l]), "g_post_mix": row(g_post_mix[l]),
            "g_pre_ffn": row(g_pre_ffn[l]), "g_post_ffn": row(g_post_ffn[l]),
            "w_in_p": w_in_p, "shift_w": shift_w[l], "w0": w0[l], "a0": a0[l],
            "wup_p": wup_p, "aup_p": aup_p, "g_up": bf(g_up[l]),
            "k_k": row(k_k[l]), "k_a": row(k_a[l]), "r_k": row(r_k[l]), "bd": bd,
            "ln_x_w": row(ln_x_w[l]), "ln_x_b": row(ln_x_b[l]), "w_branch_a": bf(w_branch_a[l]),
            "g_qnorm": row(g_qnorm[l]), "g_kvnorm": row(g_kvnorm[l]),
            "wuq_p": wuq_p, "wk_p": wk_p, "wv_p": wv_p, "wb_p": wb_p,
            "place_raw": place_raw, "place_rot": place_rot,
            "w_out": bf(w_out[l]), "w_router_t": w_router[l].T,
            "w_exp_gate": bf(w_exp_gate[l]), "w_exp_up": bf(w_exp_up[l]), "w_exp_down": bf(w_exp_down[l]),
        }
        mod = _modulation(c_rows, w_mod[l], b_mod[l])
        mods_lat = [mod[:dec_b, i * d:(i + 1) * d].reshape(dec_b, 1, d) for i in range(6)]
        mods_ctx = [mod[dec_b:dec_b + 1, i * d:(i + 1) * d].reshape(1, 1, d) for i in range(6)]
        zeros_state = jnp.zeros((xp.shape[0], A_HEADS, A_HEAD_DIM, A_HEAD_DIM), F32)
        xp, (ckv, kpe, s_f, s_b) = _group_layer(xp, mods_ctx, lw, zeros_state, zeros_state, None, None)
        ckv_l.append(ckv)
        kpe_l.append(kpe)
        sf_l.append(s_f)
        sb_l.append(s_b)
        past = cache_ckv.shape[2]
        kpe_pad = jnp.concatenate(
            [cache_kpe[:, l], jnp.zeros((dec_b, past, HEAD_PAD - QK_ROPE), F32)], axis=-1)
        k_ctx, v_ctx = _kv_up(cache_ckv[:, l].reshape(dec_b * past, KV_LORA),
                              kpe_pad.reshape(dec_b * past, HEAD_PAD), wk_p, wv_p, place_raw,
                              min(TOKEN_BLOCK, past))
        hw = B_HEADS * HEAD_PAD
        ctx_kv = (k_ctx.reshape(dec_b, past, hw), v_ctx.reshape(dec_b, past, hw))
        xs, _ = _group_layer(xs, mods_lat, lw, state_wkv_fwd[:, l], state_wkv_bwd[:, l], ctx_kv, rope_tabs)
    return (xp, xs, jnp.stack(ckv_l, axis=1), jnp.stack(kpe_l, axis=1),
            jnp.stack(sf_l, axis=1), jnp.stack(sb_l, axis=1))
```

```python
import functools
import math

import jax
import jax.numpy as jnp
from jax import lax
from jax.experimental import pallas as pl
from jax.experimental.pallas import tpu as pltpu

F32 = jnp.float32
BF16 = jnp.bfloat16
I32 = jnp.int32

GRID_W = 64
A_HEADS = 8
A_HEAD_DIM = 64
A_WIDTH = A_HEADS * A_HEAD_DIM
DECAY_LORA = 64
ICLR_LORA = 64
GATE_LORA = 128
DECAY_SCALE = 0.6065306597126334
GN_EPS = 64e-5
B_HEADS = 8
Q_LORA = 256
KV_LORA = 128
QK_NOPE = 64
QK_ROPE = 32
V_HEAD = 64
ROPE_BASE = 10000.0
ATTN_SCALE = 1.0 / math.sqrt(QK_NOPE + QK_ROPE)
N_EXPERTS = 16
CAPACITY_FACTOR = 2
EPS = 1e-6
RWKV_COLS = 3 * A_WIDTH + 2 * DECAY_LORA + 2 * ICLR_LORA + GATE_LORA
MLA_COLS = Q_LORA + KV_LORA + QK_ROPE

LANES = 128
HEAD_PAD = 128
M_COLS = 512
VMEM_LIMIT = 56 * 1024 * 1024

SCAN_CHUNK = 64
SCAN_GROUP = 8
SCAN_PACK = 2
TOKEN_BLOCK = 256
ROW_BLOCK = 512
ATTN_BLOCK = 1024
POST_BLOCK = 1024
EXPERT_TILE = 256
FFN_TILE = 256
GATHER_BLOCK = 512
GATHER_ROWS = 128


def _cparams(*sem):
    return pltpu.CompilerParams(dimension_semantics=sem, vmem_limit_bytes=VMEM_LIMIT)


def _dot(a, b):
    return jnp.dot(a, b, preferred_element_type=F32)


def _dot_nt(a, b):
    return lax.dot_general(a, b, (((1,), (1,)), ((), ())), preferred_element_type=F32)


def _dot_tn(a, b):
    return lax.dot_general(a, b, (((0,), (0,)), ((), ())), preferred_element_type=F32)


def _split2(x):
    hi = x.astype(BF16)
    lo = (x - hi.astype(F32)).astype(BF16)
    return hi, lo


def _dot_x2(a, b_bf16):
    hi, lo = _split2(a)
    return _dot(hi, b_bf16) + _dot(lo, b_bf16)


def _dot_f32(a, b):
    ah, al = _split2(a)
    bh, bl = _split2(b)
    return _dot(ah, bh) + (_dot(ah, bl) + _dot(al, bh))


def _rms(x, g):
    return (x * lax.rsqrt(jnp.mean(x * x, axis=-1, keepdims=True) + EPS)) * g


def _mod_kernel(c_ref, w_ref, b_ref, o_ref):
    c = c_ref[...]
    s = c * jax.nn.sigmoid(c)
    o_ref[...] = _dot_f32(s, w_ref[...]) + b_ref[...]


def _modulation(c_rows, w_mod, b_mod):
    rows, d = c_rows.shape
    n_out = w_mod.shape[1]
    tn = n_out // 8
    return pl.pallas_call(
        _mod_kernel,
        grid=(n_out // tn,),
        in_specs=[pl.BlockSpec((rows, d), lambda j: (0, 0)),
                  pl.BlockSpec((d, tn), lambda j: (0, j)),
                  pl.BlockSpec((1, tn), lambda j: (0, j))],
        out_specs=pl.BlockSpec((rows, tn), lambda j: (0, j)),
        out_shape=jax.ShapeDtypeStruct((rows, n_out), F32),
        compiler_params=_cparams("arbitrary"),
        name="modulation",
    )(c_rows, w_mod, b_mod.reshape(1, n_out))


def _inproj_kernel(x_ref, sc_ref, sh_ref, g_ref, w_ref, u_ref, m_ref, gt_ref, edge_ref):
    h = _rms(x_ref[...], g_ref[...]) * (1.0 + sc_ref[0]) + sh_ref[0]
    hb = h.astype(BF16)
    u = _dot(hb, w_ref[:, :RWKV_COLS])
    u_ref[...] = u
    edge_ref[0, 0:1, :] = u[0:1]
    edge_ref[0, 1:2, :] = u[u.shape[0] - 1:]
    m_ref[...] = _dot(hb, w_ref[:, RWKV_COLS:RWKV_COLS + M_COLS])
    gt_ref[...] = _dot(hb, w_ref[:, RWKV_COLS + M_COLS:]).astype(gt_ref.dtype)


def _mod_spec(mod, blocks_per_batch):
    d = mod.shape[-1]
    if mod.shape[0] == 1:
        return pl.BlockSpec((1, 1, d), lambda i: (0, 0, 0))
    return pl.BlockSpec((1, 1, d), lambda i: (i // blocks_per_batch, 0, 0))


def _inproj(x2, sc, sh, g, w_in_p, tb, blocks_per_batch):
    n, d = x2.shape
    cols = w_in_p.shape[1]
    gate_cols = cols - RWKV_COLS - M_COLS
    return pl.pallas_call(
        _inproj_kernel,
        grid=(n // tb,),
        in_specs=[pl.BlockSpec((tb, d), lambda i: (i, 0)),
                  _mod_spec(sc, blocks_per_batch), _mod_spec(sh, blocks_per_batch),
                  pl.BlockSpec((1, d), lambda i: (0, 0)),
                  pl.BlockSpec((d, cols), lambda i: (0, 0))],
        out_specs=[pl.BlockSpec((tb, RWKV_COLS), lambda i: (i, 0)),
                   pl.BlockSpec((tb, M_COLS), lambda i: (i, 0)),
                   pl.BlockSpec((tb, gate_cols), lambda i: (i, 0)),
                   pl.BlockSpec((1, 2, RWKV_COLS), lambda i: (i, 0, 0))],
        out_shape=[jax.ShapeDtypeStruct((n, RWKV_COLS), F32),
                   jax.ShapeDtypeStruct((n, M_COLS), F32),
                   jax.ShapeDtypeStruct((n, gate_cols), BF16),
                   jax.ShapeDtypeStruct((n // tb, 2, RWKV_COLS), F32)],
        compiler_params=_cparams("arbitrary"),
        name="inproj",
    )(x2, sc, sh, g, w_in_p)


def _rwkv_prep_kernel(u_ref, hp_ref, hn_ref, sw_ref, w0_ref, a0_ref, wup_ref, aup_ref, gup_ref,
                      kk_ref, ka_ref, rk_ref, bd_ref,
                      r_o, v_o, nkk_o, kdf_o, kdb_o, bf_o, bb_o, lwf_o, lwb_o, g_o, bonus_o):
    u = u_ref[0]
    tb = u.shape[0]
    row = lax.broadcasted_iota(I32, u.shape, 0)
    prev = jnp.where(row == 0, hp_ref[0, 0], pltpu.roll(u, 1, 0))
    nxt = jnp.where(row == tb - 1, hn_ref[0, 0], pltpu.roll(u, tb - 1, 0))
    xs = sw_ref[0:1] * prev + sw_ref[1:2] * u + sw_ref[2:3] * nxt
    aw = A_WIDTH
    r = xs[:, 0:aw]
    k = xs[:, aw:2 * aw]
    v = xs[:, 2 * aw:3 * aw]
    o = 3 * aw
    dw = xs[:, o:o + 2 * DECAY_LORA]
    da = xs[:, o + 2 * DECAY_LORA:o + 2 * DECAY_LORA + 2 * ICLR_LORA]
    dg = xs[:, o + 2 * DECAY_LORA + 2 * ICLR_LORA:]
    bd = bd_ref[...]
    kkr = k * kk_ref[...]
    kk = kkr * lax.rsqrt(_dot((kkr * kkr).astype(BF16), bd) + 1e-12)
    tw = jnp.tanh(dw).astype(BF16)
    dab = da.astype(BF16)
    r_o[0] = r.astype(BF16)
    v_o[0] = v.astype(BF16)
    nkk_o[0] = (-kk).astype(BF16)
    for d, (lw_o, kd_o, b_o) in enumerate(((lwf_o, kdf_o, bf_o), (lwb_o, kdb_o, bb_o))):
        lw_o[0] = -DECAY_SCALE * jax.nn.sigmoid(w0_ref[d:d + 1] + _dot(tw, wup_ref[d]))
        a = jax.nn.sigmoid(a0_ref[d:d + 1] + _dot(dab, aup_ref[d]))
        kd_o[0] = (k * (1.0 + (a - 1.0) * ka_ref[...])).astype(BF16)
        b_o[0] = (kk * a).astype(BF16)
    g_o[0] = _dot(jax.nn.sigmoid(dg).astype(BF16), gup_ref[...]).astype(BF16)
    bonus_o[0] = (_dot_x2(r * k * rk_ref[...], bd) * v).astype(BF16)


def _rwkv_prep(u3, edges, shift_w, w0, a0, wup_p, aup_p, gup, k_k, k_a, r_k, bd, tb):
    b, t, cols = u3.shape
    nb = t // tb
    zero = jnp.zeros((b, 1, cols), F32)
    if nb == 1:
        halo_prev = halo_next = zero.reshape(b, 1, 1, cols)
    else:
        edges = edges.reshape(b, nb, 2, cols)
        halo_prev = jnp.concatenate([zero, edges[:, :nb - 1, 1]], axis=1).reshape(b, nb, 1, cols)
        halo_next = jnp.concatenate([edges[:, 1:, 0], zero], axis=1).reshape(b, nb, 1, cols)
    aw = A_WIDTH
    full2 = lambda s: pl.BlockSpec(s, lambda i, j: (0, 0))
    full3 = lambda s: pl.BlockSpec(s, lambda i, j: (0, 0, 0))
    out_spec = pl.BlockSpec((1, tb, aw), lambda i, j: (i, j, 0))
    sds = lambda dt: jax.ShapeDtypeStruct((b, t, aw), dt)
    out_dtypes = [BF16] * 7 + [F32] * 2 + [BF16] * 2
    return pl.pallas_call(
        _rwkv_prep_kernel,
        grid=(b, nb),
        in_specs=[pl.BlockSpec((1, tb, cols), lambda i, j: (i, j, 0)),
                  pl.BlockSpec((1, 1, 1, cols), lambda i, j: (i, j, 0, 0)),
                  pl.BlockSpec((1, 1, 1, cols), lambda i, j: (i, j, 0, 0)),
                  full2((3, cols)), full2((2, aw)), full2((2, aw)),
                  full3(wup_p.shape), full3(aup_p.shape), full2(gup.shape),
                  full2((1, aw)), full2((1, aw)), full2((1, aw)), full2((aw, aw))],
        out_specs=[out_spec] * 11,
        out_shape=[sds(dt) for dt in out_dtypes],
        compiler_params=_cparams("arbitrary", "arbitrary"),
        name="rwkv_prep",
    )(u3, halo_prev, halo_next, shift_w, w0, a0, wup_p, aup_p, gup, k_k, k_a, r_k, bd)


def _bdot(a, b):
    return lax.dot_general(a, b, (((2,), (1,)), ((0,), (0,))), preferred_element_type=F32)


def _bdot_nt(a, b):
    return lax.dot_general(a, b, (((2,), (2,)), ((0,), (0,))), preferred_element_type=F32)


def _bdot_tn(a, b):
    return lax.dot_general(a, b, (((1,), (1,)), ((0,), (0,))), preferred_element_type=F32)


def _split_groups(x, w):
    return jnp.stack([x[g][:, p * w:(p + 1) * w]
                      for g in range(x.shape[0]) for p in range(x.shape[2] // w)], axis=0)


def _scan_kernel(*refs, reverse, add_to):
    if add_to:
        r_ref, v_ref, nkk_ref, kd_ref, b_ref, lw_ref, s0_ref, other_ref, o_ref, sf_ref, s_scr = refs
    else:
        r_ref, v_ref, nkk_ref, kd_ref, b_ref, lw_ref, s0_ref, o_ref, sf_ref, s_scr = refs
    c = pl.program_id(1)
    grp, ch, aw = lw_ref.shape
    w = s_scr.shape[-1]
    hp = w // A_HEAD_DIM
    n_groups = aw // w
    state_shape = s_scr.shape

    @pl.when(c == 0)
    def _():
        s_scr[...] = s0_ref[...].reshape(state_shape)

    lw = lw_ref[...]
    ri = lax.broadcasted_iota(I32, (ch, ch), 0)
    ci = lax.broadcasted_iota(I32, (ch, ch), 1)
    tri = jnp.where((ci >= ri) if reverse else (ci <= ri), 1.0, 0.0).astype(BF16)
    hi = lw.astype(BF16)
    rem = lw - hi.astype(F32)
    mid = rem.astype(BF16)
    lo = (rem - mid.astype(F32)).astype(BF16)
    li = jnp.stack([_dot(tri, hi[g]) + (_dot(tri, mid[g]) + _dot(tri, lo[g])) for g in range(grp)], axis=0)
    lt = li[:, 0:1] if reverse else li[:, ch - 1:ch]
    rho = 0.5 * lt
    e1 = jnp.exp(li - rho)
    e2 = jnp.exp(rho - li)
    er = jnp.exp(rho)
    a_rel = nkk_ref[...] * (e1 * jnp.exp(-lw))
    r_rel = r_ref[...] * e1
    b_rel = b_ref[...] * e2
    k_rel = kd_ref[...] * e2
    pairs = lambda x: _split_groups(x.astype(BF16), w)
    a_abs, r_abs, b_end, k_end = pairs(a_rel * er), pairs(r_rel * er), pairs(b_rel * er), pairs(k_rel * er)
    a_rel, r_rel, b_rel, k_rel = pairs(a_rel), pairs(r_rel), pairs(b_rel), pairs(k_rel)
    v = pairs(v_ref[...])
    decay = _split_groups(jnp.exp(lt), w)
    head_of_lane = lax.broadcasted_iota(I32, (1, 1, w), 2) // A_HEAD_DIM

    def blockdiag(x):
        zero = jnp.zeros_like(x)
        return jnp.concatenate([jnp.where(head_of_lane == j, x, zero) for j in range(hp)], axis=1)

    ri2 = lax.broadcasted_iota(I32, (1, ch, hp * ch), 1)
    ci2 = lax.broadcasted_iota(I32, (1, ch, hp * ch), 2) & (ch - 1)
    if reverse:
        incl2, strict2 = ci2 >= ri2, ci2 > ri2
    else:
        incl2, strict2 = ci2 <= ri2, ci2 < ri2
    eye2 = jnp.where(ri2 == ci2, 1.0, 0.0)
    lhs = jnp.concatenate([a_rel, r_rel], axis=1)
    with_b = _bdot_nt(lhs, blockdiag(b_rel))
    with_k = _bdot_nt(lhs, blockdiag(k_rel))
    a_ab = jnp.where(strict2, with_b[:, :ch], 0.0)
    a_ak = jnp.where(strict2, with_k[:, :ch], 0.0).astype(BF16)
    a_r = jnp.concatenate([jnp.where(incl2, with_b[:, ch:], 0.0), jnp.where(incl2, with_k[:, ch:], 0.0)],
                          axis=2).astype(BF16)
    inv = eye2
    for k in range(int(math.log2(ch))):
        p, q = (ci2 >> k, ri2 >> k) if reverse else (ri2 >> k, ci2 >> k)
        joins = ((p ^ q) * 4 + (p - q)) == 5
        lk = jnp.where(joins, a_ab, 0.0)
        if k == 0:
            inv = inv + lk
        else:
            invb = inv.astype(BF16)
            inv = inv + _bdot(_bdot(invb, blockdiag(lk.astype(BF16))).astype(BF16), blockdiag(invb))
    s = s_scr[...]
    from_state = _bdot_nt(jnp.concatenate([a_abs, r_abs], axis=1), s.astype(BF16))
    v_bd = blockdiag(v)
    x = from_state[:, :ch] + _bdot(a_ak, v_bd)
    u = _bdot(inv.astype(BF16), blockdiag(x.astype(BF16))).astype(BF16)
    o = from_state[:, ch:] + _bdot(a_r, jnp.concatenate([blockdiag(u), v_bd], axis=1))
    same_head = (lax.broadcasted_iota(I32, (1, w, w), 1) // A_HEAD_DIM) == head_of_lane
    upd = _bdot_tn(jnp.concatenate([u, v], axis=1), jnp.concatenate([b_end, k_end], axis=1))
    s_scr[...] = s * decay + jnp.where(same_head, upd, 0.0)
    for g in range(grp):
        o_g = jnp.concatenate([o[g * n_groups + p] for p in range(n_groups)], axis=1)
        o_ref[g] = o_g + other_ref[g] if add_to else o_g

    @pl.when(c == pl.num_programs(1) - 1)
    def _():
        s_fin = s_scr[...]
        for g in range(grp):
            for p in range(n_groups):
                group = s_fin[g * n_groups + p]
                for j in range(hp):
                    part = slice(j * A_HEAD_DIM, (j + 1) * A_HEAD_DIM)
                    sf_ref[g, hp * p + j] = group[part, part]


def _wkv_scan(r, v, nkk, kd, b, lw, s0, reverse, add_to=None):
    bsz, t, aw = r.shape
    ch = SCAN_CHUNK
    assert t % ch == 0 and ch == A_HEAD_DIM
    nc = t // ch
    w = SCAN_PACK * A_HEAD_DIM
    n_groups = aw // w
    grp = SCAN_GROUP if bsz % SCAN_GROUP == 0 else 1
    tmap = (lambda i, c: (i, nc - 1 - c, 0)) if reverse else (lambda i, c: (i, c, 0))
    seq = pl.BlockSpec((grp, ch, aw), tmap)
    st_in = pl.BlockSpec((grp, n_groups, w, w), lambda i, c: (i, 0, 0, 0))
    st_out = pl.BlockSpec((grp,) + s0.shape[1:], lambda i, c: (i, 0, 0, 0))
    extra = [] if add_to is None else [add_to]
    return pl.pallas_call(
        functools.partial(_scan_kernel, reverse=reverse, add_to=add_to is not None),
        grid=(bsz // grp, nc),
        in_specs=[seq] * 6 + [st_in] + [seq] * len(extra),
        out_specs=[seq, st_out],
        out_shape=[jax.ShapeDtypeStruct((bsz, t, aw), F32), jax.ShapeDtypeStruct(s0.shape, F32)],
        scratch_shapes=[pltpu.VMEM((grp * n_groups, w, w), F32)],
        compiler_params=_cparams("arbitrary", "arbitrary"),
        name="wkv_scan_bwd" if reverse else "wkv_scan_fwd",
    )(r, v, nkk, kd, b, lw, _group_states(s0, SCAN_PACK), *extra)


def _group_states(s, hp):
    b, h, d, _ = s.shape
    eye = jnp.eye(hp, dtype=s.dtype)[None, None, :, None, :, None]
    return (s.reshape(b, h // hp, hp, d, 1, d) * eye).reshape(b, h // hp, hp * d, hp * d)


def _denominator_lane(width):
    lane = lax.broadcasted_iota(I32, (1, width), 1)
    return jnp.where((lane & (HEAD_PAD - 1)) == V_HEAD, 1.0, 0.0)


def _rope_lanes(x, cos_t, sin_up, sin_dn):
    w = x.shape[1]
    half = QK_ROPE // 2
    return x * cos_t + pltpu.roll(x, half, 1) * sin_up + pltpu.roll(x, w - half, 1) * sin_dn


def _mla_prep_kernel(*refs, rope):
    if rope:
        (m_ref, gq_ref, gkv_ref, wuq_ref, wk_ref, wv_ref, p_ref,
         qc_ref, qu_ref, qd_ref, kc_ref, ku_ref, kd_ref, q_o, k_o, v_o, ckv_o) = refs
    else:
        (m_ref, gq_ref, gkv_ref, wuq_ref, wk_ref, wv_ref, p_ref, q_o, k_o, v_o, ckv_o) = refs
    m = m_ref[...]
    qn = _rms(m[:, :Q_LORA], gq_ref[...])
    ckv = _rms(m[:, Q_LORA:Q_LORA + KV_LORA], gkv_ref[...])
    kp = m[:, Q_LORA + KV_LORA:]
    q = _dot(qn.astype(BF16), wuq_ref[...]) * ATTN_SCALE
    if rope:
        tile = lambda ref: jnp.concatenate([ref[...]] * B_HEADS, axis=1)
        q = _rope_lanes(q, tile(qc_ref), tile(qu_ref), tile(qd_ref))
        kp = _rope_lanes(kp, kc_ref[...], ku_ref[...], kd_ref[...])
    cb = ckv.astype(BF16)
    q_o[...] = q.astype(BF16)
    k_o[...] = (_dot(cb, wk_ref[...]) + _dot(kp.astype(BF16), p_ref[...])).astype(BF16)
    v_o[...] = (_dot(cb, wv_ref[...]) + _denominator_lane(v_o.shape[1])).astype(BF16)
    ckv_o[...] = ckv


def _mla_prep(m2, g_q, g_kv, wuq_p, wk_p, wv_p, place, rope_tabs, tb, t):
    n = m2.shape[0]
    hw = B_HEADS * HEAD_PAD
    full = lambda a: pl.BlockSpec(a.shape, lambda i: (0, 0))
    ins = [m2, g_q, g_kv, wuq_p, wk_p, wv_p, place]
    specs = [pl.BlockSpec((tb, M_COLS), lambda i: (i, 0))] + [full(a) for a in ins[1:]]
    if rope_tabs is not None:
        nbt = t // tb
        ins += list(rope_tabs)
        specs += [pl.BlockSpec((tb, HEAD_PAD), lambda i: (i % nbt, 0))] * 6
    big = pl.BlockSpec((tb, hw), lambda i: (i, 0))
    return pl.pallas_call(
        functools.partial(_mla_prep_kernel, rope=rope_tabs is not None),
        grid=(n // tb,),
        in_specs=specs,
        out_specs=[big, big, big, pl.BlockSpec((tb, KV_LORA), lambda i: (i, 0))],
        out_shape=[jax.ShapeDtypeStruct((n, hw), BF16)] * 3 + [jax.ShapeDtypeStruct((n, KV_LORA), F32)],
        compiler_params=_cparams("arbitrary"),
        name="mla_prep",
    )(*ins)


def _kv_up_kernel(ckv_ref, kp_ref, wk_ref, wv_ref, p_ref, k_o, v_o):
    cb = ckv_ref[...].astype(BF16)
    k_o[...] = (_dot(cb, wk_ref[...]) + _dot(kp_ref[...].astype(BF16), p_ref[...])).astype(BF16)
    v_o[...] = (_dot(cb, wv_ref[...]) + _denominator_lane(v_o.shape[1])).astype(BF16)


def _kv_up(ckv2, kpe_pad, wk_p, wv_p, place, tb):
    n = ckv2.shape[0]
    hw = B_HEADS * HEAD_PAD
    full = lambda a: pl.BlockSpec(a.shape, lambda i: (0, 0))
    big = pl.BlockSpec((tb, hw), lambda i: (i, 0))
    return pl.pallas_call(
        _kv_up_kernel,
        grid=(n // tb,),
        in_specs=[pl.BlockSpec((tb, KV_LORA), lambda i: (i, 0)),
                  pl.BlockSpec((tb, HEAD_PAD), lambda i: (i, 0)),
                  full(wk_p), full(wv_p), full(place)],
        out_specs=[big, big],
        out_shape=[jax.ShapeDtypeStruct((n, hw), BF16)] * 2,
        compiler_params=_cparams("arbitrary"),
        name="kv_up",
    )(ckv2, kpe_pad, wk_p, wv_p, place)


def _attn_kernel(*refs, two):
    if two:
        q_ref, k1_ref, v1_ref, k2_ref, v2_ref, o_ref = refs
    else:
        q_ref, k1_ref, v1_ref, o_ref = refs
        heads = [slice(h * HEAD_PAD, (h + 1) * HEAD_PAD) for h in range(B_HEADS)]
        scores = [_dot_nt(q_ref[0, :, hs], k1_ref[0, :, hs]) for hs in heads]
        probs = [jnp.exp(s - jnp.max(s, axis=-1, keepdims=True)).astype(BF16) for s in scores]
        accs = [_dot(p, v1_ref[0, :, hs]) for p, hs in zip(probs, heads)]
        _store_heads(o_ref, [acc / acc[:, V_HEAD:V_HEAD + 1] for acc in accs])
        return
    outs = []
    for h in range(B_HEADS):
        hs = slice(h * HEAD_PAD, (h + 1) * HEAD_PAD)
        q = q_ref[0, :, hs]
        s1 = _dot_nt(q, k1_ref[0, :, hs])
        mx = jnp.max(s1, axis=-1, keepdims=True)
        if two:
            s2 = _dot_nt(q, k2_ref[0, :, hs])
            mx = jnp.maximum(mx, jnp.max(s2, axis=-1, keepdims=True))
        acc = _dot(jnp.exp(s1 - mx).astype(BF16), v1_ref[0, :, hs])
        if two:
            acc = acc + _dot(jnp.exp(s2 - mx).astype(BF16), v2_ref[0, :, hs])
        outs.append(acc / acc[:, V_HEAD:V_HEAD + 1])
        if h % 2 == 1:
            _store_heads(o_ref, outs, first_head=h - 1)
            outs = []


def _store_heads(o_ref, outs, first_head=0):
    for i in range(0, len(outs), 2):
        lanes = slice((first_head + i) * V_HEAD, (first_head + i + 2) * V_HEAD)
        o_ref[0, :, lanes] = jnp.concatenate([outs[i][:, :V_HEAD], outs[i + 1][:, :V_HEAD]], axis=1).astype(BF16)


def _attention(q3, k1, v1, k2, v2, tq):
    b, t, hw = q3.shape
    two = k2 is not None
    qspec = pl.BlockSpec((1, tq, hw), lambda i, j: (i, j, 0))
    kv = lambda a: pl.BlockSpec((1, a.shape[1], hw), lambda i, j: (i, 0, 0))
    ins = [q3, k1, v1] + ([k2, v2] if two else [])
    return pl.pallas_call(
        functools.partial(_attn_kernel, two=two),
        grid=(b, t // tq),
        in_specs=[qspec] + [kv(a) for a in ins[1:]],
        out_specs=pl.BlockSpec((1, tq, B_HEADS * V_HEAD), lambda i, j: (i, j, 0)),
        out_shape=jax.ShapeDtypeStruct((b, t, B_HEADS * V_HEAD), BF16),
        compiler_params=_cparams("arbitrary", "arbitrary"),
        name="attention",
    )(*ins)


def _post_mix_kernel(o_ref, bonus_ref, g_ref, att_ref, gates_ref, x_ref,
                     g1_ref, sc2_ref, sh2_ref, lnw_ref, lnb_ref, bd_ref, wa_ref, wb_ref, wo_ref,
                     gpost_ref, gpre_ref, wr_ref, x1_o, h2_o, aff_o):
    bd = bd_ref[...]
    inv_n = 1.0 / A_HEAD_DIM
    o = o_ref[...]
    mu = _dot_x2(o, bd) * inv_n
    oc = o - mu
    var = _dot((oc * oc).astype(BF16), bd) * inv_n
    on = (oc * lax.rsqrt(var + GN_EPS)) * lnw_ref[...] + lnb_ref[...]
    ya = _dot(((on + bonus_ref[...]) * g_ref[...]).astype(BF16), wa_ref[...])
    yb = _dot(att_ref[...], wb_ref[...])
    d = ya.shape[1]
    gates = gates_ref[...].astype(F32)
    mix = jax.nn.sigmoid(gates[:, :d]) * ya + jax.nn.sigmoid(gates[:, d:]) * yb
    y = _dot(mix.astype(BF16), wo_ref[...])
    x1 = x_ref[...] + g1_ref[0] * _rms(y, gpost_ref[...])
    x1_o[...] = x1
    h2 = _rms(x1, gpre_ref[...]) * (1.0 + sc2_ref[0]) + sh2_ref[0]
    h2_o[...] = h2.astype(BF16)
    hh, hl = _split2(h2)
    wh, wl = _split2(wr_ref[...])
    logits = _dot_nt(wh, hh) + (_dot_nt(wh, hl) + _dot_nt(wl, hh))
    z = jnp.exp(logits - jnp.max(logits, axis=0, keepdims=True))
    aff_o[...] = z / jnp.sum(z, axis=0, keepdims=True)


def _post_mix(o, bonus, g, att, gates, x2, g1, sc2, sh2, ln_w, ln_b, bd, wa, wb, wo,
              g_post, g_pre, w_router_t, tb, blocks_per_batch):
    n, d = x2.shape
    tok = lambda a: pl.BlockSpec((tb, a.shape[1]), lambda i: (i, 0))
    full = lambda a: pl.BlockSpec(a.shape, lambda i: (0, 0))
    ms = lambda a: _mod_spec(a, blocks_per_batch)
    return pl.pallas_call(
        _post_mix_kernel,
        grid=(n // tb,),
        in_specs=[tok(o), tok(bonus), tok(g), tok(att), tok(gates), tok(x2),
                  ms(g1), ms(sc2), ms(sh2), full(ln_w), full(ln_b), full(bd), full(wa), full(wb),
                  full(wo), full(g_post), full(g_pre), full(w_router_t)],
        out_specs=[pl.BlockSpec((tb, d), lambda i: (i, 0)),
                   pl.BlockSpec((tb, d), lambda i: (i, 0)),
                   pl.BlockSpec((N_EXPERTS, tb), lambda i: (0, i))],
        out_shape=[jax.ShapeDtypeStruct((n, d), F32),
                   jax.ShapeDtypeStruct((n, d), BF16),
                   jax.ShapeDtypeStruct((N_EXPERTS, n), F32)],
        compiler_params=_cparams("arbitrary"),
        name="post_mix",
    )(o, bonus, g, att, gates, x2, g1, sc2, sh2, ln_w, ln_b, bd, wa, wb, wo,
      g_post, g_pre, w_router_t)


def _select_kernel(a_ref, pos_ref, sel_ref, *, cap, n_chunks):
    a = a_ref[...]
    rows = a.shape[0]
    a3 = a.reshape(N_EXPERTS, n_chunks, LANES)

    def count(mask):
        c = jnp.sum(jnp.where(mask, 1.0, 0.0), axis=2, keepdims=True)
        return jnp.sum(c, axis=1, keepdims=True)

    def body(i, thr):
        cand = thr | jnp.left_shift(jnp.int32(1), 30 - i)
        return jnp.where(count(a3 >= pltpu.bitcast(cand, F32)) >= cap, cand, thr)

    thr = pltpu.bitcast(lax.fori_loop(0, 31, body, jnp.zeros((N_EXPERTS, 1, LANES), I32)), F32)
    gt = a3 > thr
    eq = a3 == thr
    need = cap - count(gt)

    li = lax.broadcasted_iota(I32, (LANES, LANES), 0)
    lj = lax.broadcasted_iota(I32, (LANES, LANES), 1)
    lane_before = jnp.where(li < lj, 1.0, 0.0).astype(BF16)
    lane_all = jnp.ones((LANES, LANES), BF16)
    ci = lax.broadcasted_iota(I32, (n_chunks, n_chunks), 0)
    cj = lax.broadcasted_iota(I32, (n_chunks, n_chunks), 1)
    chunk_before = jnp.where(cj < ci, 1.0, 0.0).astype(BF16)

    def prefix(flags3):
        f2 = flags3.reshape(rows, LANES).astype(BF16)
        within = _dot(f2, lane_before).reshape(N_EXPERTS, n_chunks, LANES)
        tot = _dot(f2, lane_all).astype(BF16).reshape(N_EXPERTS, n_chunks, LANES)
        offs = [_dot(chunk_before, tot[e]) for e in range(N_EXPERTS)]
        return within + jnp.stack(offs, axis=0)

    eqf = jnp.where(eq, 1.0, 0.0)
    tie_ok = jnp.where(prefix(eqf) < need, eqf, 0.0)
    sel = jnp.where(gt, 1.0, tie_ok)
    pos_ref[...] = prefix(sel).reshape(rows, LANES).astype(I32)
    sel_ref[...] = sel.reshape(rows, LANES).astype(I32)


def _select(aff_t, cap):
    e, n = aff_t.shape
    n_chunks = n // LANES
    rows = e * n_chunks
    a2 = aff_t.reshape(rows, LANES)
    spec = pl.BlockSpec((rows, LANES), lambda i: (0, 0))
    pos, sel = pl.pallas_call(
        functools.partial(_select_kernel, cap=cap, n_chunks=n_chunks),
        grid=(1,),
        in_specs=[spec],
        out_specs=[spec, spec],
        out_shape=[jax.ShapeDtypeStruct((rows, LANES), I32)] * 2,
        compiler_params=_cparams("arbitrary"),
        name="expert_select",
    )(a2)
    return pos.reshape(e, n), sel.reshape(e, n)


def _ffn_kernel(lo_ref, hi_ref, slot_ref, aff_ref, h_ref, wg_ref, wu_ref, wd_ref, y_ref, acc_ref, wacc_ref,
                *, n_tiles, sub):
    e = pl.program_id(0)
    j = pl.program_id(1)
    rows = acc_ref.shape[0]
    n_blocks, tb = slot_ref.shape[1], slot_ref.shape[2]
    n_sub = rows // sub
    acc_ref[...] = jnp.zeros_like(acc_ref)
    wacc_ref[...] = jnp.zeros_like(wacc_ref)
    first = [lo_ref[(e * n_tiles + j) * n_sub + si] for si in range(n_sub)]
    last = [hi_ref[(e * n_tiles + j) * n_sub + si] for si in range(n_sub)]
    trips = functools.reduce(jnp.maximum, [hi - lo for lo, hi in zip(first, last)])
    row0 = lax.broadcasted_iota(I32, (sub, tb), 0) + j * rows

    def gather_step(i, carry):
        for si in range(n_sub):
            b = first[si] + i
            live = b < last[si]
            b = jnp.minimum(b, n_blocks - 1)
            part = slice(si * sub, (si + 1) * sub)
            hit = (row0 + si * sub) == jnp.where(live, slot_ref[0, pl.ds(b, 1), :], -1)
            tokens = h_ref[pl.ds(pl.multiple_of(b * tb, tb), tb), :]
            acc_ref[part, :] += _dot(jnp.where(hit, 1.0, 0.0).astype(BF16), tokens)
            wacc_ref[part, :] += jnp.sum(jnp.where(hit, aff_ref[0, pl.ds(b, 1), :], 0.0), axis=1, keepdims=True)
        return carry

    lax.fori_loop(0, trips, gather_step, 0)
    xe = acc_ref[...].astype(BF16)
    gate = _dot(xe, wg_ref[0])
    up = _dot(xe, wu_ref[0])
    hid = (gate * jax.nn.sigmoid(gate)) * up
    y_ref[0] = (_dot(hid.astype(BF16), wd_ref[0]) * wacc_ref[...]).astype(BF16)


def _expert_ffn(pos, slot, aff_t, h2, wg, wu, wd, cap, tb, rt):
    n, d = h2.shape
    nb = n // tb
    e, _, f = wg.shape
    n_tiles = cap // rt
    sub = min(GATHER_ROWS, rt)
    starts = jnp.concatenate([pos[:, ::tb], jnp.full((e, 1), cap, I32)], axis=1)
    edges = jnp.arange(cap // sub, dtype=I32) * sub
    lo = jnp.sum((starts[:, None, 1:] <= edges[None, :, None]).astype(I32), axis=2)
    hi = jnp.sum((starts[:, None, :-1] < (edges + sub)[None, :, None]).astype(I32), axis=2)
    grid_spec = pltpu.PrefetchScalarGridSpec(
        num_scalar_prefetch=2,
        grid=(e, n_tiles),
        in_specs=[pl.BlockSpec((1, nb, tb), lambda ei, j, lo, hi: (ei, 0, 0)),
                  pl.BlockSpec((1, nb, tb), lambda ei, j, lo, hi: (ei, 0, 0)),
                  pl.BlockSpec((n, d), lambda ei, j, lo, hi: (0, 0), pipeline_mode=pl.Buffered(1)),
                  pl.BlockSpec((1, d, f), lambda ei, j, lo, hi: (ei, 0, 0)),
                  pl.BlockSpec((1, d, f), lambda ei, j, lo, hi: (ei, 0, 0)),
                  pl.BlockSpec((1, f, d), lambda ei, j, lo, hi: (ei, 0, 0))],
        out_specs=pl.BlockSpec((1, rt, d), lambda ei, j, lo, hi: (ei, j, 0)),
        scratch_shapes=[pltpu.VMEM((rt, d), F32), pltpu.VMEM((rt, 1), F32)],
    )
    return pl.pallas_call(
        functools.partial(_ffn_kernel, n_tiles=n_tiles, sub=sub),
        grid_spec=grid_spec,
        out_shape=jax.ShapeDtypeStruct((e, cap, d), BF16),
        compiler_params=_cparams("arbitrary", "arbitrary"),
        name="expert_ffn",
    )(lo.reshape(-1), hi.reshape(-1), slot.reshape(e, nb, tb), aff_t.reshape(e, nb, tb), h2, wg, wu, wd)


def _combine_kernel(t0_ref, fetch_ref, need_ref, slot_ref, x_ref, g2_ref, gp_ref, *rest, n_win):
    y_refs, o_ref, acc_ref = rest[:-2], rest[-2], rest[-1]
    blk = pl.program_id(0)
    n_blk = pl.num_programs(0)
    tb = slot_ref.shape[0]
    rows = y_refs[0].shape[1]
    lane = lax.broadcasted_iota(I32, (tb, rows), 1)

    def picked(e, k):
        slot = slot_ref[:, e:e + 1] - (t0_ref[blk * N_EXPERTS + e] + k) * rows
        return _dot(jnp.where(lane == slot, 1.0, 0.0).astype(BF16), y_refs[e * n_win + k][0])

    acc = picked(0, 0)
    for e in range(1, N_EXPERTS):
        acc = acc + picked(e, 0)
    acc_ref[...] = acc
    for k in range(1, n_win):
        for e in range(N_EXPERTS):
            @pl.when(need_ref[(k * n_blk + blk) * N_EXPERTS + e] != 0)
            def _():
                acc_ref[...] += picked(e, k)
    o_ref[...] = x_ref[...] + g2_ref[0] * _rms(acc_ref[...], gp_ref[...])


def _combine_windows(pos, tb, rt, cap):
    n_tiles = cap // rt
    lo = pos[:, ::tb]
    hi = jnp.concatenate([lo[:, 1:], jnp.full((lo.shape[0], 1), cap, I32)], axis=1)
    t0 = jnp.minimum(lo // rt, n_tiles - 1)
    t_last = jnp.where(hi > lo, (hi - 1) // rt, t0)
    fetch, need = [t0], [jnp.ones_like(t0)]
    for k in range(1, tb // rt + 1):
        need.append(t_last >= t0 + k)
        fetch.append(lax.cummax(jnp.where(need[k], t0 + k, 0), axis=1))
    flat = lambda a: a.T.reshape(-1).astype(I32)
    return flat(t0), jnp.concatenate([flat(f) for f in fetch]), jnp.concatenate([flat(f) for f in need])


def _combine(windows, slot_t, ye, x1, g2, g_post, tb, rt, blocks_per_batch):
    t0, fetch, need = windows
    n, d = x1.shape
    e, cap, _ = ye.shape
    nb = n // tb
    n_win = tb // rt + 1
    if g2.shape[0] == 1:
        g2_spec = pl.BlockSpec((1, 1, d), lambda b, t, f, nd: (0, 0, 0))
    else:
        g2_spec = pl.BlockSpec((1, 1, d), lambda b, t, f, nd: (b // blocks_per_batch, 0, 0))

    def window(ei, k):
        return pl.BlockSpec((1, rt, d), lambda b, t, f, nd: (ei, f[(k * nb + b) * e + ei], 0))

    grid_spec = pltpu.PrefetchScalarGridSpec(
        num_scalar_prefetch=3,
        grid=(nb,),
        in_specs=[pl.BlockSpec((tb, e), lambda b, t, f, nd: (b, 0)),
                  pl.BlockSpec((tb, d), lambda b, t, f, nd: (b, 0)),
                  g2_spec,
                  pl.BlockSpec((1, d), lambda b, t, f, nd: (0, 0))]
                 + [window(ei, k) for ei in range(e) for k in range(n_win)],
        out_specs=pl.BlockSpec((tb, d), lambda b, t, f, nd: (b, 0)),
        scratch_shapes=[pltpu.VMEM((tb, d), F32)],
    )
    return pl.pallas_call(
        functools.partial(_combine_kernel, n_win=n_win),
        grid_spec=grid_spec,
        out_shape=jax.ShapeDtypeStruct((n, d), F32),
        compiler_params=_cparams("arbitrary"),
        name="moe_combine",
    )(t0, fetch, need, slot_t, x1, g2, g_post, *([ye] * (e * n_win)))


def _rope_tables(t):
    half = QK_ROPE // 2
    pos = jnp.arange(t)
    row = (pos // GRID_W).astype(F32)
    col = (pos % GRID_W).astype(F32)
    inv = ROPE_BASE ** (-jnp.arange(0, half, 2, dtype=F32) / half)
    ang = jnp.concatenate([row[:, None] * inv, col[:, None] * inv], axis=-1)
    cos, sin = jnp.cos(ang), jnp.sin(ang)

    def tabs(first_lane):
        left, right = first_lane, HEAD_PAD - first_lane - 2 * half
        fill = lambda value, width: jnp.full((t, width), value, F32)
        c = jnp.concatenate([fill(1.0, left), cos, cos, fill(1.0, right)], axis=1)
        up = jnp.concatenate([fill(0.0, left + half), sin, fill(0.0, right)], axis=1)
        dn = jnp.concatenate([fill(0.0, left), -sin, fill(0.0, half + right)], axis=1)
        return c, up, dn

    return tabs(QK_NOPE + QK_ROPE) + tabs(QK_ROPE)


def _layout_weights(w_in, w_uq, w_ukv, w_branch_b, w_up, a_up):
    d = w_in.shape[0]
    deint = jnp.concatenate([jnp.arange(0, QK_ROPE, 2), jnp.arange(1, QK_ROPE, 2)])
    kpe0 = RWKV_COLS + Q_LORA + KV_LORA
    kpe_cols = w_in[:, kpe0:kpe0 + QK_ROPE]
    w_in_p = jnp.concatenate(
        [w_in[:, :kpe0 + QK_ROPE], kpe_cols[:, deint],
         jnp.zeros((d, M_COLS - MLA_COLS - QK_ROPE), F32), w_in[:, RWKV_COLS + MLA_COLS:]], axis=1)
    uq = w_uq.reshape(Q_LORA, B_HEADS, QK_NOPE + QK_ROPE)
    wuq_p = jnp.concatenate([uq, uq[:, :, QK_NOPE:][:, :, deint]], axis=2).reshape(Q_LORA, B_HEADS * HEAD_PAD)
    ukv = w_ukv.reshape(KV_LORA, B_HEADS, QK_NOPE + V_HEAD)
    zpad = jnp.zeros((KV_LORA, B_HEADS, HEAD_PAD - QK_NOPE), F32)
    wk_p = jnp.concatenate([ukv[:, :, :QK_NOPE], zpad], axis=2).reshape(KV_LORA, B_HEADS * HEAD_PAD)
    wv_p = jnp.concatenate([ukv[:, :, QK_NOPE:], zpad], axis=2).reshape(KV_LORA, B_HEADS * HEAD_PAD)
    wb_p = w_branch_b
    eye = jnp.eye(QK_ROPE, dtype=F32)
    z = jnp.zeros((QK_ROPE, QK_ROPE), F32)
    head_raw = jnp.concatenate([jnp.zeros((QK_ROPE, QK_NOPE), F32), eye, z], axis=1)
    head_rot = jnp.concatenate([jnp.zeros((QK_ROPE, QK_NOPE), F32), z, eye], axis=1)
    zrows = jnp.zeros((HEAD_PAD - 2 * QK_ROPE, B_HEADS * HEAD_PAD), F32)
    zr = jnp.zeros((QK_ROPE, B_HEADS * HEAD_PAD), F32)
    place_raw = jnp.concatenate([jnp.tile(head_raw, (1, B_HEADS)), zr, zrows], axis=0)
    place_rot = jnp.concatenate([zr, jnp.tile(head_rot, (1, B_HEADS)), zrows], axis=0)
    zl = jnp.zeros((DECAY_LORA, A_WIDTH), F32)
    wup_p = jnp.stack([jnp.concatenate([w_up[0], zl]), jnp.concatenate([zl, w_up[1]])])
    aup_p = jnp.stack([jnp.concatenate([a_up[0], zl]), jnp.concatenate([zl, a_up[1]])])
    bf = lambda a: a.astype(BF16)
    return (bf(w_in_p), bf(wuq_p), bf(wk_p), bf(wv_p), bf(wb_p), bf(place_raw), bf(place_rot),
            bf(wup_p), bf(aup_p))


def _group_layer(x, mods, lw, s0_f, s0_b, ctx_kv, rope_tabs):
    bsz, t, d = x.shape
    n = bsz * t
    tb = min(TOKEN_BLOCK, t)
    tw = min(ROW_BLOCK, t)
    bpb = t // tb
    bpw = t // tw
    tf = min(ROW_BLOCK, n) if mods[0].shape[0] == 1 and rope_tabs is None else tw
    tp = min(POST_BLOCK, n if mods[0].shape[0] == 1 else t)
    sh1, sc1, g1, sh2, sc2, g2 = mods
    x2 = x.reshape(n, d)
    u, m, gates, edges = _inproj(x2, sc1, sh1, lw["g_pre_mix"], lw["w_in_p"], tf, bpw)
    prep = _rwkv_prep(u.reshape(bsz, t, RWKV_COLS), edges, lw["shift_w"], lw["w0"], lw["a0"], lw["wup_p"],
                      lw["aup_p"], lw["g_up"], lw["k_k"], lw["k_a"], lw["r_k"], lw["bd"], tw)
    r, v, nkk, kd_f, kd_b, b_f, b_b, lw_f, lw_b, g, bonus = prep
    o_f, s_f = _wkv_scan(r, v, nkk, kd_f, b_f, lw_f, s0_f, reverse=False)
    o_sum, s_b = _wkv_scan(r, v, nkk, kd_b, b_b, lw_b, s0_b, reverse=True, add_to=o_f)
    place = lw["place_raw"] if rope_tabs is None else lw["place_rot"]
    q, k, vv, ckv = _mla_prep(m, lw["g_qnorm"], lw["g_kvnorm"], lw["wuq_p"], lw["wk_p"], lw["wv_p"],
                              place, rope_tabs, tf, t)
    hw = B_HEADS * HEAD_PAD
    k2, v2 = ctx_kv if ctx_kv is not None else (None, None)
    att = _attention(q.reshape(bsz, t, hw), k.reshape(bsz, t, hw), vv.reshape(bsz, t, hw), k2, v2,
                     min(ATTN_BLOCK, t))
    two = lambda a: a.reshape(n, a.shape[-1])
    x1, h2, aff_t = _post_mix(two(o_sum), two(bonus), two(g), two(att), gates, x2,
                              g1, sc2, sh2, lw["ln_x_w"], lw["ln_x_b"], lw["bd"], lw["w_branch_a"],
                              lw["wb_p"], lw["w_out"], lw["g_post_mix"], lw["g_pre_ffn"],
                              lw["w_router_t"], tp, t // tp if tp <= t else 1)
    cap = CAPACITY_FACTOR * n // N_EXPERTS
    rt = min(EXPERT_TILE, cap)
    pos, sel = _select(aff_t, cap)
    slot = jnp.where(sel > 0, pos, -1)
    ye = _expert_ffn(pos, slot, aff_t, h2, lw["w_exp_gate"], lw["w_exp_up"], lw["w_exp_down"], cap,
                     min(GATHER_BLOCK, n), min(FFN_TILE, cap))
    out = _combine(_combine_windows(pos, tb, rt, cap), slot.T, ye, x1, g2, lw["g_post_ffn"], tb, rt, bpb)
    kpe = m[:, Q_LORA + KV_LORA:Q_LORA + KV_LORA + QK_ROPE]
    return out.reshape(bsz, t, d), (ckv.reshape(bsz, t, KV_LORA), kpe.reshape(bsz, t, QK_ROPE), s_f, s_b)


def kernel(x_prompt, x_sample, cache_ckv, cache_kpe, state_wkv_fwd, state_wkv_bwd, c, c_ctx,
           w_mod, b_mod, g_pre_mix, g_post_mix, g_pre_ffn, g_post_ffn, w_in, shift_w,
           w0, w_up, a0, a_up, g_up, k_k, k_a, r_k, ln_x_w, ln_x_b, w_branch_a,
           g_qnorm, w_uq, g_kvnorm, w_ukv, w_branch_b, w_out,
           w_router, w_exp_gate, w_exp_up, w_exp_down):
    depth = w_mod.shape[0]
    d = x_prompt.shape[-1]
    dec_b, dec_t = x_sample.shape[0], x_sample.shape[1]
    xp, xs = x_prompt, x_sample
    c_rows = jnp.concatenate([c, c_ctx[None, :],
                              jnp.zeros((-(dec_b + 1) % 8, d), F32)], axis=0)
    rope_tabs = _rope_tables(dec_t)
    ii = lax.broadcasted_iota(I32, (A_WIDTH, A_WIDTH), 0) // A_HEAD_DIM
    jj = lax.broadcasted_iota(I32, (A_WIDTH, A_WIDTH), 1) // A_HEAD_DIM
    bd = (ii == jj).astype(BF16)
    row = lambda a: a.reshape(1, -1)
    bf = lambda a: a.astype(BF16)
    ckv_l, kpe_l, sf_l, sb_l = [], [], [], []
    for l in range(depth):
        (w_in_p, wuq_p, wk_p, wv_p, wb_p, place_raw, place_rot, wup_p, aup_p) = _layout_weights(
            w_in[l], w_uq[l], w_ukv[l], w_branch_b[l], w_up[l], a_up[l])
        lw = {
            "g_pre_mix": row(g_pre_mix[l]), "g_post_mix": row(g_post_mix[l]),
            "g_pre_ffn": row(g_pre_ffn[l]), "g_post_ffn": row(g_post_ffn[l]),
            "w_in_p": w_in_p, "shift_w": shift_w[l], "w0": w0[l], "a0": a0[l],
            "wup_p": wup_p, "aup_p": aup_p, "g_up": bf(g_up[l]),
            "k_k": row(k_k[l]), "k_a": row(k_a[l]), "r_k": row(r_k[l]), "bd": bd,
            "ln_x_w": row(ln_x_w[l]), "ln_x_b": row(ln_x_b[l]), "w_branch_a": bf(w_branch_a[l]),
            "g_qnorm": row(g_qnorm[l]), "g_kvnorm": row(g_kvnorm[l]),
            "wuq_p": wuq_p, "wk_p": wk_p, "wv_p": wv_p, "wb_p": wb_p,
            "place_raw": place_raw, "place_rot": place_rot,
            "w_out": bf(w_out[l]), "w_router_t": w_router[l].T,
            "w_exp_gate": bf(w_exp_gate[l]), "w_exp_up": bf(w_exp_up[l]), "w_exp_down": bf(w_exp_down[l]),
        }
        mod = _modulation(c_rows, w_mod[l], b_mod[l])
        mods_lat = [mod[:dec_b, i * d:(i + 1) * d].reshape(dec_b, 1, d) for i in range(6)]
        mods_ctx = [mod[dec_b:dec_b + 1, i * d:(i + 1) * d].reshape(1, 1, d) for i in range(6)]
        zeros_state = jnp.zeros((xp.shape[0], A_HEADS, A_HEAD_DIM, A_HEAD_DIM), F32)
        xp, (ckv, kpe, s_f, s_b) = _group_layer(xp, mods_ctx, lw, zeros_state, zeros_state, None, None)
        ckv_l.append(ckv)
        kpe_l.append(kpe)
        sf_l.append(s_f)
        sb_l.append(s_b)
        past = cache_ckv.shape[2]
        kpe_pad = jnp.concatenate(
            [cache_kpe[:, l], jnp.zeros((dec_b, past, HEAD_PAD - QK_ROPE), F32)], axis=-1)
        k_ctx, v_ctx = _kv_up(cache_ckv[:, l].reshape(dec_b * past, KV_LORA),
                              kpe_pad.reshape(dec_b * past, HEAD_PAD), wk_p, wv_p, place_raw,
                              min(TOKEN_BLOCK, past))
        hw = B_HEADS * HEAD_PAD
        ctx_kv = (k_ctx.reshape(dec_b, past, hw), v_ctx.reshape(dec_b, past, hw))
        xs, _ = _group_layer(xs, mods_lat, lw, state_wkv_fwd[:, l], state_wkv_bwd[:, l], ctx_kv, rope_tabs)
    return (xp, xs, jnp.stack(ckv_l, axis=1), jnp.stack(kpe_l, axis=1),
            jnp.stack(sf_l, axis=1), jnp.stack(sb_l, axis=1))
```
